```python
import numpy as np
import jax
import jax.numpy as jnp
from jax import lax

D_MODEL = 1024
BATCH = 2
SEQ = 8192
DEPTH = 2

HEAD_DIM = 64
NSA_HEADS = D_MODEL // (2 * HEAD_DIM)
NSA_KV_GROUPS = max(1, NSA_HEADS // 4)
NSA_HPG = NSA_HEADS // NSA_KV_GROUPS
NSA_CMP_LEN = 32
NSA_CMP_STRIDE = 16
NSA_CMP_HID = 128
NSA_SEL_BLOCK = 64
NSA_SEL_TOP = 16
NSA_WINDOW = 512
NSA_Q_CHUNK = 64
MOBA_HEADS = D_MODEL // (2 * HEAD_DIM)
MOBA_BLOCK = 256
MOBA_TOP = 3
MOBA_Q_CHUNK = 32
N_EXPERTS = 32
TOP_K = 4
D_FF = 1024
SWIGLU_LIMIT = 7.0
SWIGLU_ALPHA = 1.702
MOE_BLOCK = 128
EPS = 1e-6
NEG = -1e30
FORCE_SCORE = 1e4

NSA_W = NSA_HEADS * HEAD_DIM
NSA_KV_W = NSA_KV_GROUPS * HEAD_DIM
MOBA_W = MOBA_HEADS * HEAD_DIM
MIX_W = NSA_W + MOBA_W
IN_WIDTHS = (NSA_W, NSA_KV_W, NSA_KV_W, NSA_KV_W, NSA_KV_W, NSA_KV_W, NSA_KV_W,
             3 * NSA_HEADS, MOBA_W, MOBA_W, MOBA_W)
IN_W = sum(IN_WIDTHS)

kernel_name = 'hybrid_nsa_moba_moe_block'


def rms_norm(x, g):
    xf = x.astype(jnp.float32)
    y = xf * lax.rsqrt(jnp.mean(xf * xf, axis=-1, keepdims=True) + EPS)
    return (y * g.astype(jnp.float32)).astype(x.dtype)


def alibi_slopes(n):
    return jnp.asarray(2.0 ** (-8.0 * np.arange(1, n + 1) / n), dtype=jnp.float32)


def masked_softmax(s, mask):
    s = jnp.where(mask, s.astype(jnp.float32), NEG)
    p = jax.nn.softmax(s, axis=-1)
    return jnp.where(mask, p, 0.0)


def _heads(t, n_heads):
    b, s, _ = t.shape
    return t.reshape(b, s, n_heads, HEAD_DIM).transpose(0, 2, 1, 3)


_gather_blocks = jax.vmap(jax.vmap(lambda blocks, idx: blocks[idx]))


def _compress(kv, pos, w1, w2):
    s = kv.shape[2]
    n_cmp = (s - NSA_CMP_LEN) // NSA_CMP_STRIDE + 1
    idx = NSA_CMP_STRIDE * np.arange(n_cmp)[:, None] + np.arange(NSA_CMP_LEN)[None, :]
    blocks = kv[:, :, idx] + pos
    flat = blocks.reshape(kv.shape[0], kv.shape[1], n_cmp, NSA_CMP_LEN * HEAD_DIM)
    return jax.nn.gelu(flat @ w1) @ w2


def nsa_mixer(q, k_cmp, v_cmp, k_slc, v_slc, k_win, v_win, gates, k_gain, cmp_pos, cmp_w1, cmp_w2):
    B, G, Hg, S, dh = q.shape
    scale = dh ** -0.5
    sl = alibi_slopes(G * Hg).reshape(1, G, Hg, 1, 1)
    kc = rms_norm(_compress(k_cmp, cmp_pos[0], cmp_w1[0], cmp_w2[0]), k_gain)
    vc = _compress(v_cmp, cmp_pos[1], cmp_w1[1], cmp_w2[1])
    n_cmp = kc.shape[2]
    cmp_start = NSA_CMP_STRIDE * np.arange(n_cmp)
    cmp_end = jnp.asarray(cmp_start + NSA_CMP_LEN - 1, dtype=jnp.int32)
    cmp_center = jnp.asarray(cmp_start + 0.5 * (NSA_CMP_LEN - 1), dtype=jnp.float32)
    n_sel = S // NSA_SEL_BLOCK
    sel_start = NSA_SEL_BLOCK * np.arange(n_sel)
    incid = jnp.asarray((cmp_start[:, None] <= sel_start[None, :] + NSA_SEL_BLOCK - 1)
                        & (cmp_start[:, None] + NSA_CMP_LEN - 1 >= sel_start[None, :]), dtype=jnp.float32)
    n_top = min(NSA_SEL_TOP, n_sel)
    ks = rms_norm(k_slc, k_gain).reshape(B, G, n_sel, NSA_SEL_BLOCK, dh)
    vs = v_slc.reshape(B, G, n_sel, NSA_SEL_BLOCK, dh)
    pad = ((0, 0), (0, 0), (NSA_WINDOW, 0), (0, 0))
    kw = jnp.pad(rms_norm(k_win, k_gain), pad)
    vw = jnp.pad(v_win, pad)
    blk_ids = jnp.arange(n_sel)[None, :]
    in_blk = jnp.arange(NSA_SEL_BLOCK)
    win_off = jnp.arange(NSA_WINDOW + NSA_Q_CHUNK)
    C = NSA_Q_CHUNK
    m_sel = n_top * NSA_SEL_BLOCK

    def chunk(q0):
        qc = lax.dynamic_slice_in_dim(q, q0, C, axis=3)
        gc = lax.dynamic_slice_in_dim(gates, q0, C, axis=3)
        t = q0 + jnp.arange(C)
        tf = t.astype(jnp.float32)
        s = jnp.einsum('bghcd,bgnd->bghcn', qc, kc, preferred_element_type=jnp.float32) * scale
        s = s - sl * (tf[:, None] - cmp_center[None, :])
        p_cmp = masked_softmax(s, cmp_end[None, :] <= t[:, None])
        o_cmp = jnp.einsum('bghcn,bgnd->bghcd', p_cmp.astype(vc.dtype), vc)
        imp = jnp.einsum('bghcn,nj->bgcj', p_cmp, incid)
        cur = (t // NSA_SEL_BLOCK)[:, None]
        forced = (blk_ids == 0) | (blk_ids == cur) | (blk_ids == cur - 1)
        imp = jnp.where(forced, FORCE_SCORE, imp)
        imp = jnp.where(blk_ids <= cur, imp, NEG)
        _, sel = lax.top_k(imp, n_top)
        ksel = _gather_blocks(ks, sel).reshape(B, G, C, m_sel, dh)
        vsel = _gather_blocks(vs, sel).reshape(B, G, C, m_sel, dh)
        pos = (sel[..., None] * NSA_SEL_BLOCK + in_blk).reshape(B, G, C, m_sel)
        dist = (t[None, None, :, None] - pos)[:, :, None]
        s = jnp.einsum('bghcd,bgcmd->bghcm', qc, ksel, preferred_element_type=jnp.float32) * scale
        s = s - sl * dist.astype(jnp.float32)
        p = masked_softmax(s, dist >= 0)
        o_sel = jnp.einsum('bghcm,bgcmd->bghcd', p.astype(vsel.dtype), vsel)
        kwc = lax.dynamic_slice_in_dim(kw, q0, NSA_WINDOW + C, axis=2)
        vwc = lax.dynamic_slice_in_dim(vw, q0, NSA_WINDOW + C, axis=2)
        pos_w = q0 - NSA_WINDOW + win_off
        dist_w = t[:, None] - pos_w[None, :]
        mask_w = (dist_w >= 0) & (dist_w < NSA_WINDOW) & (pos_w >= 0)[None, :]
        s = jnp.einsum('bghcd,bgkd->bghck', qc, kwc, preferred_element_type=jnp.float32) * scale
        s = s - sl * dist_w.astype(jnp.float32)
        p = masked_softmax(s, mask_w)
        o_win = jnp.einsum('bghck,bgkd->bghcd', p.astype(vwc.dtype), vwc)
        return gc[..., 0:1] * o_cmp + gc[..., 1:2] * o_sel + gc[..., 2:3] * o_win

    out = lax.map(chunk, jnp.arange(0, S, C))
    return out.transpose(1, 0, 4, 2, 3, 5).reshape(B, S, G * Hg * dh)


def moba_mixer(q, k, v):
    B, H, S, dh = q.shape
    scale = dh ** -0.5
    sl = alibi_slopes(H).reshape(1, H, 1, 1)
    n_blk = -(-S // MOBA_BLOCK)
    pad = ((0, 0), (0, 0), (0, n_blk * MOBA_BLOCK - S), (0, 0))
    kp = jnp.pad(k, pad)
    vp = jnp.pad(v, pad)
    kb = kp.reshape(B, H, n_blk, MOBA_BLOCK, dh)
    vb = vp.reshape(B, H, n_blk, MOBA_BLOCK, dh)
    kmean = jnp.mean(kb.astype(jnp.float32), axis=3)
    n_top = min(MOBA_TOP, n_blk - 1)
    in_blk = jnp.arange(MOBA_BLOCK)
    blk_ids = jnp.arange(n_blk)
    C = MOBA_Q_CHUNK
    m_g = n_top * MOBA_BLOCK

    def chunk(q0):
        qc = lax.dynamic_slice_in_dim(q, q0, C, axis=2)
        t = q0 + jnp.arange(C)
        cb = q0 // MOBA_BLOCK
        kcur = lax.dynamic_slice_in_dim(kp, cb * MOBA_BLOCK, MOBA_BLOCK, axis=2)
        vcur = lax.dynamic_slice_in_dim(vp, cb * MOBA_BLOCK, MOBA_BLOCK, axis=2)
        dist_c = t[:, None] - (cb * MOBA_BLOCK + in_blk)[None, :]
        s_c = jnp.einsum('bhcd,bhkd->bhck', qc, kcur, preferred_element_type=jnp.float32) * scale
        s_c = s_c - sl * dist_c.astype(jnp.float32)
        mask_c = jnp.broadcast_to(dist_c >= 0, s_c.shape)
        if n_top > 0:
            gate = jnp.einsum('bhcd,bhnd->bhcn', qc, kmean, preferred_element_type=jnp.float32)
            gate = jnp.where(blk_ids < cb, gate, NEG)
            _, sel = lax.top_k(gate, n_top)
            kg = _gather_blocks(kb, sel).reshape(B, H, C, m_g, dh)
            vg = _gather_blocks(vb, sel).reshape(B, H, C, m_g, dh)
            pos_g = (sel[..., None] * MOBA_BLOCK + in_blk).reshape(B, H, C, m_g)
            dist_g = t[None, None, :, None] - pos_g
            s_g = jnp.einsum('bhcd,bhcmd->bhcm', qc, kg, preferred_element_type=jnp.float32) * scale
            s_g = s_g - sl * dist_g.astype(jnp.float32)
            mask_g = jnp.repeat(sel < cb, MOBA_BLOCK, axis=-1)
            p = masked_softmax(jnp.concatenate([s_g, s_c], axis=-1),
                               jnp.concatenate([mask_g, mask_c], axis=-1))
            o = (jnp.einsum('bhcm,bhcmd->bhcd', p[..., :m_g].astype(vg.dtype), vg)
                 + jnp.einsum('bhck,bhkd->bhcd', p[..., m_g:].astype(vcur.dtype), vcur))
        else:
            p = masked_softmax(s_c, mask_c)
            o = jnp.einsum('bhck,bhkd->bhcd', p.astype(vcur.dtype), vcur)
        return o

    out = lax.map(chunk, jnp.arange(0, S, C))
    return out.transpose(1, 0, 3, 2, 4).reshape(B, S, H * dh)


def hybrid_mixer(h, w_in, w_out, nsa_q_gain, nsa_k_gain, cmp_pos, cmp_w1, cmp_w2, moba_q_gain, moba_k_gain):
    B, S, _ = h.shape
    z = h @ w_in
    points = [int(v) for v in np.cumsum(IN_WIDTHS)[:-1]]
    q_n, kc, vc, ks, vs, kw, vw, g_n, q_m, k_m, v_m = jnp.split(z, points, axis=-1)
    qn = rms_norm(_heads(q_n, NSA_HEADS), nsa_q_gain).reshape(B, NSA_KV_GROUPS, NSA_HPG, S, HEAD_DIM)
    gates = jax.nn.sigmoid(g_n.reshape(B, S, NSA_KV_GROUPS, NSA_HPG, 3)).transpose(0, 2, 3, 1, 4)
    o_nsa = nsa_mixer(qn, _heads(kc, NSA_KV_GROUPS), _heads(vc, NSA_KV_GROUPS),
                      _heads(ks, NSA_KV_GROUPS), _heads(vs, NSA_KV_GROUPS),
                      _heads(kw, NSA_KV_GROUPS), _heads(vw, NSA_KV_GROUPS),
                      gates, nsa_k_gain, cmp_pos, cmp_w1, cmp_w2)
    o_moba = moba_mixer(rms_norm(_heads(q_m, MOBA_HEADS), moba_q_gain),
                        rms_norm(_heads(k_m, MOBA_HEADS), moba_k_gain),
                        _heads(v_m, MOBA_HEADS))
    return jnp.concatenate([o_nsa, o_moba], axis=-1) @ w_out


def sparse_moe(h, router_w, router_b, w1, b1, w2, b2):
    B, S, D = h.shape
    T = B * S
    hf = h.reshape(T, D)
    logits = (hf @ router_w + router_b).astype(jnp.float32)
    top_val, top_idx = lax.top_k(logits, TOP_K)
    gate = jax.nn.softmax(top_val, axis=-1)
    A = T * TOP_K
    e_flat = top_idx.reshape(A)
    tok_flat = jnp.arange(A, dtype=jnp.int32) // TOP_K
    w_flat = gate.reshape(A).astype(h.dtype)
    order = jnp.argsort(e_flat)
    e_s, tok_s, w_s = e_flat[order], tok_flat[order], w_flat[order]
    counts = jnp.bincount(e_flat, length=N_EXPERTS)
    padded = (counts + MOE_BLOCK - 1) // MOE_BLOCK * MOE_BLOCK
    start = jnp.cumsum(counts) - counts
    pend = jnp.cumsum(padded)
    pstart = pend - padded
    dest = pstart[e_s] + jnp.arange(A) - start[e_s]
    R = A + N_EXPERTS * MOE_BLOCK
    n_blocks = R // MOE_BLOCK
    row_tok = jnp.zeros((R,), jnp.int32).at[dest].set(tok_s)
    row_w = jnp.zeros((R,), h.dtype).at[dest].set(w_s)
    blk_e = jnp.minimum(jnp.searchsorted(pend, jnp.arange(n_blocks) * MOE_BLOCK, side='right'), N_EXPERTS - 1)

    def expert_block(args):
        e, toks, wts = args
        u = hf[toks] @ w1[e] + b1[e]
        g_, lin = u[:, :D_FF], u[:, D_FF:]
        g_ = jnp.minimum(g_, SWIGLU_LIMIT)
        lin = jnp.clip(lin, -SWIGLU_LIMIT, SWIGLU_LIMIT)
        y = ((lin + 1.0) * g_ * jax.nn.sigmoid(SWIGLU_ALPHA * g_)) @ w2[e] + b2[e]
        return y * wts[:, None]

    y = lax.map(expert_block, (blk_e, row_tok.reshape(n_blocks, MOE_BLOCK), row_w.reshape(n_blocks, MOE_BLOCK)))
    out = jax.ops.segment_sum(y.reshape(R, D), row_tok, num_segments=T)
    return out.reshape(B, S, D)


def setup_inputs(seed: int = 0) -> dict:
    key = jax.random.key(seed)
    ks = jax.random.split(key, 21)
    n = jax.random.normal
    f32 = jnp.float32
    D = D_MODEL
    return {
        'x': n(ks[0], (BATCH, SEQ, D), f32),
        'c': n(ks[1], (BATCH, D), f32),
        'ada_w': n(ks[2], (DEPTH, D, 6 * D), f32) * (0.5 * D ** -0.5),
        'ada_b': n(ks[3], (DEPTH, 6 * D), f32) * 0.01,
        'ln1_g': 1.0 + 0.01 * n(ks[4], (DEPTH, D), f32),
        'ln2_g': 1.0 + 0.01 * n(ks[5], (DEPTH, D), f32),
        'w_in': n(ks[6], (DEPTH, D, IN_W), f32) * D ** -0.5,
        'nsa_q_gain': 1.0 + 0.01 * n(ks[7], (DEPTH, HEAD_DIM), f32),
        'nsa_k_gain': 1.0 + 0.01 * n(ks[8], (DEPTH, HEAD_DIM), f32),
        'nsa_cmp_pos': 0.1 * n(ks[9], (DEPTH, 2, NSA_CMP_LEN, HEAD_DIM), f32),
        'nsa_cmp_w1': n(ks[10], (DEPTH, 2, NSA_CMP_LEN * HEAD_DIM, NSA_CMP_HID), f32) * (NSA_CMP_LEN * HEAD_DIM) ** -0.5,
        'nsa_cmp_w2': n(ks[11], (DEPTH, 2, NSA_CMP_HID, HEAD_DIM), f32) * NSA_CMP_HID ** -0.5,
        'moba_q_gain': 1.0 + 0.01 * n(ks[12], (DEPTH, HEAD_DIM), f32),
        'moba_k_gain': 1.0 + 0.01 * n(ks[13], (DEPTH, HEAD_DIM), f32),
        'w_out': n(ks[14], (DEPTH, MIX_W, D), f32) * MIX_W ** -0.5,
        'router_w': n(ks[15], (DEPTH, D, N_EXPERTS), f32) * D ** -0.5,
        'router_b': 0.01 * n(ks[16], (DEPTH, N_EXPERTS), f32),
        'exp_w1': n(ks[17], (DEPTH, N_EXPERTS, D, 2 * D_FF), f32) * D ** -0.5,
        'exp_b1': 0.01 * n(ks[18], (DEPTH, N_EXPERTS, 2 * D_FF), f32),
        'exp_w2': n(ks[19], (DEPTH, N_EXPERTS, D_FF, D), f32) * D_FF ** -0.5,
        'exp_b2': 0.01 * n(ks[20], (DEPTH, N_EXPERTS, D), f32),
    }


def reference(x, c, ada_w, ada_b, ln1_g, ln2_g, w_in, nsa_q_gain, nsa_k_gain, nsa_cmp_pos, nsa_cmp_w1,
              nsa_cmp_w2, moba_q_gain, moba_k_gain, w_out, router_w, router_b, exp_w1, exp_b1, exp_w2, exp_b2):
    cond = jax.nn.silu(c)
    for l in range(DEPTH):
        mod = (cond @ ada_w[l] + ada_b[l])[:, None, :]
        sh1, sc1, g1, sh2, sc2, g2 = jnp.split(mod, 6, axis=-1)
        h = rms_norm(x, ln1_g[l]) * (1.0 + sc1) + sh1
        x = x + g1 * hybrid_mixer(h, w_in[l], w_out[l], nsa_q_gain[l], nsa_k_gain[l], nsa_cmp_pos[l],
                                  nsa_cmp_w1[l], nsa_cmp_w2[l], moba_q_gain[l], moba_k_gain[l])
        h = rms_norm(x, ln2_g[l]) * (1.0 + sc2) + sh2
        x = x + g2 * sparse_moe(h, router_w[l], router_b[l], exp_w1[l], exp_b1[l], exp_w2[l], exp_b2[l])
    return x
```

```python
import functools

import numpy as np
import jax
import jax.numpy as jnp
from jax import lax
from jax.experimental import pallas as pl
from jax.experimental.pallas import tpu as pltpu

F32 = jnp.float32
BF16 = jnp.bfloat16
I32 = jnp.int32

HEAD_DIM = 64
NSA_HEADS = 8
NSA_GROUPS = 2
NSA_HPG = NSA_HEADS // NSA_GROUPS
CMP_LEN = 32
CMP_STRIDE = 16
CMP_HID = 128
SEL_BLOCK = 64
SEL_TOP = 16
WINDOW = 512
MOBA_HEADS = 8
MOBA_BLOCK = 256
MOBA_TOP = 3
N_EXPERTS = 32
TOP_K = 4
SWIGLU_LIMIT = 7.0
SWIGLU_ALPHA = 1.702
EPS = 1e-6
NEG = -1e30
FORCE_SCORE = 1e4

LANES = 128
SUBLANES = 8
TQ = 256
TK = 256
TM_PROJ = 512
TM_MOE = 256
TC_COMB = 64
VMEM_LIMIT = 56 * 1024 * 1024


def _dot(a, b):
    return jnp.dot(a, b, preferred_element_type=F32)


def _hi_lo(x):
    hi = x.astype(BF16)
    lo = (x - hi.astype(F32)).astype(BF16)
    return hi, lo


def _sigmoid(x):
    return 1.0 / (1.0 + jnp.exp(-x))


def _params(*sem):
    return pltpu.CompilerParams(dimension_semantics=sem, vmem_limit_bytes=VMEM_LIMIT)


def _adaln_kernel(c_ref, w_ref, b_ref, o_ref):
    c = c_ref[...]
    cond = c * _sigmoid(c)
    ch, cl = _hi_lo(cond)
    wh, wl = _hi_lo(w_ref[...])
    o_ref[...] = _dot(ch, wh) + _dot(ch, wl) + _dot(cl, wh) + b_ref[...]


def _adaln(c, ada_w, ada_b):
    depth, d, n6 = ada_w.shape
    b = c.shape[0]
    tn = 1024
    c_pad = jnp.zeros((SUBLANES, d), F32).at[:b].set(c)
    out = pl.pallas_call(
        _adaln_kernel,
        grid=(depth, n6 // tn),
        in_specs=[
            pl.BlockSpec((SUBLANES, d), lambda l, j: (0, 0)),
            pl.BlockSpec((None, d, tn), lambda l, j: (l, 0, j)),
            pl.BlockSpec((None, 1, tn), lambda l, j: (l, 0, j)),
        ],
        out_specs=pl.BlockSpec((None, SUBLANES, tn), lambda l, j: (l, 0, j)),
        out_shape=jax.ShapeDtypeStruct((depth, SUBLANES, n6), F32),
        compiler_params=_params("arbitrary", "arbitrary"),
        name="adaln",
    )(c_pad, ada_w, ada_b.reshape(depth, 1, n6))
    return out[:, :b]


def _block_diag_ones():
    r = lax.broadcasted_iota(I32, (LANES, LANES), 0) // HEAD_DIM
    c = lax.broadcasted_iota(I32, (LANES, LANES), 1) // HEAD_DIM
    return jnp.where(r == c, 1.0, 0.0).astype(BF16)


def _head_norm(z, bd, gain):
    cols = []
    for c0 in range(0, z.shape[1], LANES):
        zc = z[:, c0:c0 + LANES]
        hi, lo = _hi_lo(zc * zc)
        ss = _dot(hi, bd) + _dot(lo, bd)
        cols.append(zc * lax.rsqrt(ss * (1.0 / HEAD_DIM) + EPS))
    zn = cols[0] if len(cols) == 1 else jnp.concatenate(cols, axis=1)
    return zn * gain


def _modulated_norm(x, g, sc1p, sh):
    ms = jnp.mean(x * x, axis=-1, keepdims=True)
    y = x * lax.rsqrt(ms + EPS)
    return (y * g) * sc1p + sh


def _inproj_kernel(x_ref, g_ref, sc_ref, sh_ref, w_ref, gq_ref, gk_ref, gmq_ref, gmk_ref,
                   q_ref, kcv_ref, ks_ref, vs_ref, kw_ref, vw_ref, gate_ref,
                   mq_ref, mk_ref, mv_ref, kmean_ref):
    h = _modulated_norm(x_ref[...], g_ref[...], sc_ref[...], sh_ref[...])
    hb = h.astype(BF16)
    bd = _block_diag_ones()

    def proj(a, b):
        return _dot(hb, w_ref[:, a:b])

    q_ref[...] = _head_norm(proj(0, 512), bd, gq_ref[...]).astype(BF16)
    kv = proj(512, 1280)
    kcv_ref[...] = kv[:, 0:256]
    ks_ref[...] = _head_norm(kv[:, 256:384], bd, gk_ref[...]).astype(BF16)
    vs_ref[...] = kv[:, 384:512].astype(BF16)
    kw_ref[...] = _head_norm(kv[:, 512:640], bd, gk_ref[...]).astype(BF16)
    vw_ref[...] = kv[:, 640:768].astype(BF16)
    gate_ref[...] = _sigmoid(proj(1280, 1408))
    mq_ref[...] = _head_norm(proj(1408, 1920), bd, gmq_ref[...]).astype(BF16)
    mkn = _head_norm(proj(1920, 2432), bd, gmk_ref[...])
    mk_ref[...] = mkn.astype(BF16)
    mv_ref[...] = proj(2432, 2944).astype(BF16)
    means = [jnp.sum(mkn[r:r + MOBA_BLOCK], axis=0, keepdims=True) * (1.0 / MOBA_BLOCK)
             for r in range(0, mkn.shape[0], MOBA_BLOCK)]
    kmean_ref[...] = jnp.concatenate(means, axis=0)


def _pack_w_in(w_in):
    d = w_in.shape[0]
    wg = jnp.zeros((d, LANES), w_in.dtype).at[:, :3 * NSA_HEADS].set(w_in[:, 1280:1304])
    return jnp.concatenate([w_in[:, :1280], wg, w_in[:, 1304:]], axis=1).astype(BF16)


def _inproj(x2d, ln_g, sc1p, sh, w_pack, gq, gk, gmq, gmk, seq):
    t, d = x2d.shape
    tm = TM_PROJ
    nb = seq // tm
    row = lambda i: (i, 0)
    bat = lambda i: (i // nb, 0, 0)
    const = lambda i: (0, 0)
    outs = [
        (512, BF16), (256, F32), (128, BF16), (128, BF16), (128, BF16), (128, BF16),
        (128, F32), (512, BF16), (512, BF16), (512, BF16),
    ]
    out_shape = [jax.ShapeDtypeStruct((t, w), dt) for w, dt in outs]
    out_specs = [pl.BlockSpec((tm, w), row) for w, _ in outs]
    nmean = tm // MOBA_BLOCK
    out_shape.append(jax.ShapeDtypeStruct((t // tm, nmean, 512), F32))
    out_specs.append(pl.BlockSpec((None, nmean, 512), lambda i: (i, 0, 0)))
    return pl.pallas_call(
        _inproj_kernel,
        grid=(t // tm,),
        in_specs=[
            pl.BlockSpec((tm, d), row),
            pl.BlockSpec((1, d), const),
            pl.BlockSpec((None, 1, d), bat),
            pl.BlockSpec((None, 1, d), bat),
            pl.BlockSpec(w_pack.shape, const),
            pl.BlockSpec((1, 512), const),
            pl.BlockSpec((1, 128), const),
            pl.BlockSpec((1, 512), const),
            pl.BlockSpec((1, 512), const),
        ],
        out_specs=out_specs,
        out_shape=out_shape,
        compiler_params=_params("arbitrary"),
        name="inproj",
    )(x2d, ln_g, sc1p, sh, w_pack, gq, gk, gmq, gmk)


def _compress_kernel(x_ref, pos_ref, w1_ref, w2_ref, gk_ref, o_ref):
    kv = pl.program_id(1)
    half = CMP_STRIDE * HEAD_DIM
    x = x_ref[...]
    xl = (x + pos_ref[:, :half]).astype(BF16)
    xh = (x + pos_ref[:, half:]).astype(BF16)
    a = _dot(xl, w1_ref[:half, :])
    b = _dot(xh, w1_ref[half:, :])
    hid = a + pltpu.roll(b, b.shape[0] - 1, 0)
    out = _dot(jax.nn.gelu(hid, approximate=True).astype(BF16), w2_ref[...])
    ms = jnp.mean(out * out, axis=-1, keepdims=True)
    normed = (out * lax.rsqrt(ms + EPS)) * gk_ref[...]
    o_ref[...] = jnp.where(kv == 0, normed, out)


def _compress(xc, pos, w1, w2, gk):
    b, _, g, nr, w = xc.shape
    return pl.pallas_call(
        _compress_kernel,
        grid=(b, 2, g),
        in_specs=[
            pl.BlockSpec((None, None, None, nr, w), lambda i, k, j: (i, k, j, 0, 0)),
            pl.BlockSpec((None, 1, w * 2), lambda i, k, j: (k, 0, 0)),
            pl.BlockSpec((None, w * 2, CMP_HID), lambda i, k, j: (k, 0, 0)),
            pl.BlockSpec((None, CMP_HID, HEAD_DIM), lambda i, k, j: (k, 0, 0)),
            pl.BlockSpec((1, HEAD_DIM), lambda i, k, j: (0, 0)),
        ],
        out_specs=pl.BlockSpec((None, None, None, nr, HEAD_DIM), lambda i, k, j: (i, k, j, 0, 0)),
        out_shape=jax.ShapeDtypeStruct((b, 2, g, nr, HEAD_DIM), F32),
        compiler_params=_params("arbitrary", "arbitrary", "arbitrary"),
        name="nsa_compress",
    )(xc, pos, w1, w2, gk)


def _nsa_slope(g, h):
    return 2.0 ** (-(g * NSA_HPG + h + 1))


def _rank_select(imp, n_top):
    nb = imp.shape[0]
    pieces = [imp[r:r + SUBLANES] for r in range(0, nb, SUBLANES)]
    counts = [jnp.zeros(p.shape, I32) for p in pieces]
    sub = lax.broadcasted_iota(I32, pieces[0].shape, 0)
    for jp in range(nb):
        row = imp[jp:jp + 1]
        rj = jp // SUBLANES
        for r, p in enumerate(pieces):
            if r < rj:
                inc = jnp.where(row > p, 1, 0)
            elif r > rj:
                inc = jnp.where(row >= p, 1, 0)
            else:
                inc = jnp.where(sub > jp % SUBLANES, jnp.where(row >= p, 1, 0), jnp.where(row > p, 1, 0))
            counts[r] = counts[r] + inc
    cnt = jnp.concatenate(counts, axis=0)
    return cnt < n_top


def _nsa_cmp_kernel(q_ref, kc_ref, vct_ref, inct_ref, gate_ref, acc_ref, sel_ref, *, n_top):
    g = pl.program_id(1)
    q0 = pl.program_id(2) * TQ
    nc = kc_ref.shape[0]
    nsel = inct_ref.shape[0]
    t = q0 + lax.broadcasted_iota(I32, (1, TQ), 1)
    n = lax.broadcasted_iota(I32, (nc, 1), 0)
    center = (CMP_STRIDE * n).astype(F32) + 0.5 * (CMP_LEN - 1)
    dist = t.astype(F32) - center
    mask = (CMP_STRIDE * n + (CMP_LEN - 1)) <= t
    kc = kc_ref[...]
    vct = vct_ref[...]
    psum = jnp.zeros((nc, TQ), F32)
    for h in range(NSA_HPG):
        slope = jnp.where(g == 0, _nsa_slope(0, h), _nsa_slope(1, h))
        s = _dot(kc, q_ref[h]) - slope * dist
        s = jnp.where(mask, s, NEG)
        m = jnp.max(s, axis=0, keepdims=True)
        e = jnp.exp(s - m)
        l = jnp.sum(e, axis=0, keepdims=True)
        p = jnp.where(mask, e, 0.0) * (1.0 / l)
        o = _dot(vct, p.astype(BF16))
        gate = gate_ref[pl.ds(g * (3 * NSA_HPG) + 3 * h, 1), :]
        acc_ref[h] = gate * o
        psum = psum + p
    hi, lo = _hi_lo(psum)
    inct = inct_ref[...]
    imp = _dot(inct, hi) + _dot(inct, lo)
    blk = lax.broadcasted_iota(I32, (nsel, 1), 0)
    cur = t // SEL_BLOCK
    forced = (blk == 0) | (blk == cur) | (blk == cur - 1)
    imp = jnp.where(forced, FORCE_SCORE, imp)
    imp = jnp.where(blk <= cur, imp, NEG)
    chosen = _rank_select(imp, n_top) & (blk <= cur)
    sel_ref[...] = jnp.where(chosen, 1.0, 0.0)


def _incidence_t(nc, nsel):
    cmp_start = CMP_STRIDE * np.arange(nc)
    sel_start = SEL_BLOCK * np.arange(nsel)
    inc = ((cmp_start[:, None] <= sel_start[None, :] + SEL_BLOCK - 1)
           & (cmp_start[:, None] + CMP_LEN - 1 >= sel_start[None, :]))
    inc[nc - 1] = False
    return jnp.asarray(inc.T, dtype=BF16)


def _nsa_cmp(qt, kc, vct, gates_t):
    b, _, _, s = qt.shape
    nc = kc.shape[2]
    nsel = s // SEL_BLOCK
    n_top = min(SEL_TOP, nsel)
    inct = _incidence_t(nc, nsel)
    return pl.pallas_call(
        functools.partial(_nsa_cmp_kernel, n_top=n_top),
        grid=(b, NSA_GROUPS, s // TQ),
        in_specs=[
            pl.BlockSpec((None, NSA_HPG, HEAD_DIM, TQ), lambda i, g, j: (i, g, 0, j)),
            pl.BlockSpec((None, None, nc, HEAD_DIM), lambda i, g, j: (i, g, 0, 0)),
            pl.BlockSpec((None, None, HEAD_DIM, nc), lambda i, g, j: (i, g, 0, 0)),
            pl.BlockSpec((nsel, nc), lambda i, g, j: (0, 0)),
            pl.BlockSpec((None, LANES, TQ), lambda i, g, j: (i, 0, j)),
        ],
        out_specs=[
            pl.BlockSpec((None, NSA_HPG, HEAD_DIM, TQ), lambda i, g, j: (i, g, 0, j)),
            pl.BlockSpec((None, None, nsel, TQ), lambda i, g, j: (i, g, 0, j)),
        ],
        out_shape=[
            jax.ShapeDtypeStruct((b, NSA_HEADS, HEAD_DIM, s), F32),
            jax.ShapeDtypeStruct((b, NSA_GROUPS, nsel, s), F32),
        ],
        compiler_params=_params("arbitrary", "arbitrary", "arbitrary"),
        name="nsa_cmp_select",
    )(qt, kc, vct, inct, gates_t)


def _flash_init(s, mask, vt, m_ref, l_ref, acc_ref, h):
    s = jnp.where(mask, s, NEG)
    m = jnp.max(s, axis=0, keepdims=True)
    p = jnp.where(mask, jnp.exp(s - m), 0.0)
    m_ref[h] = m
    l_ref[h] = jnp.sum(p, axis=0, keepdims=True)
    acc_ref[h] = _dot(vt, p.astype(BF16))


def _flash_update(s, mask, vt, m_ref, l_ref, acc_ref, h):
    if mask is not None:
        s = jnp.where(mask, s, NEG)
    m_old = m_ref[h]
    m_new = jnp.maximum(m_old, jnp.max(s, axis=0, keepdims=True))
    alpha = jnp.exp(m_old - m_new)
    p = jnp.exp(s - m_new)
    m_ref[h] = m_new
    l_ref[h] = alpha * l_ref[h] + jnp.sum(p, axis=0, keepdims=True)
    acc_ref[h] = alpha * acc_ref[h] + _dot(vt, p.astype(BF16))


def _tile_dist():
    return (lax.broadcasted_iota(I32, (TK, TQ), 1) - lax.broadcasted_iota(I32, (TK, TQ), 0))


def _nsa_selwin_kernel(q_ref, ks_ref, vst_ref, kw_ref, vwt_ref, sel_ref, gate_ref, accin_ref, o_ref,
                       ms_ref, ls_ref, as_ref, mw_ref, lw_ref, aw_ref):
    g = pl.program_id(1)
    i = pl.program_id(2)
    bps = TK // SEL_BLOCK
    dist0 = _tile_dist()
    dist0f = dist0.astype(F32)
    slopes = [jnp.where(g == 0, _nsa_slope(0, h), _nsa_slope(1, h)) for h in range(NSA_HPG)]

    def sel_rows(j):
        return [sel_ref[pl.ds(j * bps + b, 1), :] for b in range(bps)]

    rows = sel_rows(i)
    selm = jnp.concatenate([jnp.broadcast_to(r, (SEL_BLOCK, TQ)) for r in rows], axis=0) > 0.5
    mask_d = selm & (dist0 >= 0)
    causal = dist0 >= 0
    k_d = ks_ref[i]
    v_d = vst_ref[i]
    kw_d = kw_ref[i]
    vw_d = vwt_ref[i]
    for h in range(NSA_HPG):
        q = q_ref[h]
        _flash_init(_dot(k_d, q) - slopes[h] * dist0f, mask_d, v_d, ms_ref, ls_ref, as_ref, h)
        _flash_init(_dot(kw_d, q) - slopes[h] * dist0f, causal, vw_d, mw_ref, lw_ref, aw_ref, h)

    def past(j, carry):
        bias = jnp.concatenate(
            [jnp.broadcast_to((r - 1.0) * (-NEG), (SEL_BLOCK, TQ)) for r in sel_rows(j)], axis=0)
        distf = dist0f + ((i - j) * TK).astype(F32)
        k_j = ks_ref[j]
        v_j = vst_ref[j]
        for h in range(NSA_HPG):
            s = _dot(k_j, q_ref[h]) - slopes[h] * distf + bias
            _flash_update(s, None, v_j, ms_ref, ls_ref, as_ref, h)
        return carry

    lax.fori_loop(0, i, past, 0)

    for d in range(1, WINDOW // TK + 1):
        @pl.when(i >= d)
        def _():
            dist = dist0 + d * TK
            mask = dist < WINDOW
            distf = dist.astype(F32)
            k_j = kw_ref[i - d]
            v_j = vwt_ref[i - d]
            for h in range(NSA_HPG):
                s = _dot(k_j, q_ref[h]) - slopes[h] * distf
                _flash_update(s, mask, v_j, mw_ref, lw_ref, aw_ref, h)

    for h in range(NSA_HPG):
        row = g * (3 * NSA_HPG) + 3 * h
        g_sel = gate_ref[pl.ds(row + 1, 1), :]
        g_win = gate_ref[pl.ds(row + 2, 1), :]
        o = (accin_ref[h] + (g_sel / ls_ref[h]) * as_ref[h] + (g_win / lw_ref[h]) * aw_ref[h])
        o_ref[h] = o.astype(o_ref.dtype)


def _nsa_selwin(qt, ks_t, vs_t, kw_t, vw_t, sel_t, gates_t, acc_t):
    b, _, _, s = qt.shape
    nt = s // TK
    nsel = s // SEL_BLOCK
    kspec = pl.BlockSpec((None, None, nt, TK, HEAD_DIM), lambda i, g, j: (i, g, 0, 0, 0))
    vspec = pl.BlockSpec((None, None, nt, HEAD_DIM, TK), lambda i, g, j: (i, g, 0, 0, 0))
    qspec = pl.BlockSpec((None, NSA_HPG, HEAD_DIM, TQ), lambda i, g, j: (i, g, 0, j))
    stat = pltpu.VMEM((NSA_HPG, 1, TQ), F32)
    accs = pltpu.VMEM((NSA_HPG, HEAD_DIM, TQ), F32)
    return pl.pallas_call(
        _nsa_selwin_kernel,
        grid=(b, NSA_GROUPS, s // TQ),
        in_specs=[
            qspec, kspec, vspec, kspec, vspec,
            pl.BlockSpec((None, None, nsel, TQ), lambda i, g, j: (i, g, 0, j)),
            pl.BlockSpec((None, LANES, TQ), lambda i, g, j: (i, 0, j)),
            qspec,
        ],
        out_specs=qspec,
        out_shape=jax.ShapeDtypeStruct((b, NSA_HEADS, HEAD_DIM, s), BF16),
        scratch_shapes=[stat, stat, accs, stat, stat, accs],
        compiler_params=_params("arbitrary", "arbitrary", "arbitrary"),
        name="nsa_select_window",
    )(qt, ks_t, vs_t, kw_t, vw_t, sel_t, gates_t, acc_t)


def _moba_kernel(q_ref, k_ref, vt_ref, kmean_ref, o_ref, m_ref, l_ref, acc_ref, bias_ref, *, n_top):
    hd = pl.program_id(1)
    i = pl.program_id(2)
    nblk = kmean_ref.shape[0]
    q = q_ref[...]
    slope = jnp.float32(2.0 ** -MOBA_HEADS)
    for hh in range(MOBA_HEADS - 1):
        slope = jnp.where(hd == hh, 2.0 ** -(hh + 1), slope)
    dist0 = _tile_dist()
    dist0f = dist0.astype(F32)

    gate = _dot(kmean_ref[...].astype(BF16), q)
    blk = lax.broadcasted_iota(I32, (nblk, 1), 0)
    gate = jnp.where(blk < i, gate, NEG)
    bias = jnp.full(gate.shape, NEG, F32)
    for _ in range(n_top):
        mx = jnp.max(gate, axis=0, keepdims=True)
        first = jnp.min(jnp.where(gate == mx, blk, nblk), axis=0, keepdims=True)
        pick = blk == first
        bias = jnp.where(pick, 0.0, bias)
        gate = jnp.where(pick, -jnp.inf, gate)
    bias_ref[...] = jnp.where(blk < i, bias, NEG)

    _flash_init(_dot(k_ref[i], q) - slope * dist0f, dist0 >= 0, vt_ref[i], m_ref, l_ref, acc_ref, 0)

    def past(j, carry):
        distf = dist0f + ((i - j) * TK).astype(F32)
        s = _dot(k_ref[j], q) - slope * distf + bias_ref[pl.ds(j, 1), :]
        _flash_update(s, None, vt_ref[j], m_ref, l_ref, acc_ref, 0)
        return carry

    lax.fori_loop(0, i, past, 0)
    o_ref[...] = (acc_ref[0] / l_ref[0]).astype(o_ref.dtype)


def _moba(qt, k_t, v_t, kmean):
    b, nh, _, s = qt.shape
    nt = s // TK
    n_top = min(MOBA_TOP, nt - 1)
    return pl.pallas_call(
        functools.partial(_moba_kernel, n_top=n_top),
        grid=(b, nh, s // TQ),
        in_specs=[
            pl.BlockSpec((None, None, HEAD_DIM, TQ), lambda i, h, j: (i, h, 0, j)),
            pl.BlockSpec((None, None, nt, TK, HEAD_DIM), lambda i, h, j: (i, h, 0, 0, 0)),
            pl.BlockSpec((None, None, nt, HEAD_DIM, TK), lambda i, h, j: (i, h, 0, 0, 0)),
            pl.BlockSpec((None, None, nt, HEAD_DIM), lambda i, h, j: (i, h, 0, 0)),
        ],
        out_specs=pl.BlockSpec((None, None, HEAD_DIM, TQ), lambda i, h, j: (i, h, 0, j)),
        out_shape=jax.ShapeDtypeStruct((b, nh, HEAD_DIM, s), BF16),
        scratch_shapes=[
            pltpu.VMEM((1, 1, TQ), F32), pltpu.VMEM((1, 1, TQ), F32),
            pltpu.VMEM((1, HEAD_DIM, TQ), F32), pltpu.VMEM((nt, TQ), F32),
        ],
        compiler_params=_params("arbitrary", "arbitrary", "arbitrary"),
        name="moba",
    )(qt, k_t, v_t, kmean)


def _outproj_kernel(on_ref, om_ref, w_ref, x_ref, g1_ref, ln_ref, sc_ref, sh_ref,
                    wrh_ref, wrl_ref, rb_ref, x1_ref, h2_ref, lg_ref):
    half = on_ref.shape[1]
    y = _dot(on_ref[...], w_ref[:half, :]) + _dot(om_ref[...], w_ref[half:, :])
    x1 = x_ref[...] + g1_ref[...] * y
    x1_ref[...] = x1
    h2 = _modulated_norm(x1, ln_ref[...], sc_ref[...], sh_ref[...])
    h2_ref[...] = h2
    hh, hl = _hi_lo(h2)
    lg_ref[...] = _dot(hh, wrh_ref[...]) + _dot(hh, wrl_ref[...]) + _dot(hl, wrh_ref[...]) + rb_ref[...]


def _outproj(o_nsa, o_moba, w_out, x2d, g1, ln_g, sc2p, sh2, wr_hi, wr_lo, rb, seq):
    t, d = x2d.shape
    tm = TM_PROJ
    nb = seq // tm
    row = lambda i: (i, 0)
    bat = lambda i: (i // nb, 0, 0)
    const = lambda i: (0, 0)
    half = o_nsa.shape[1]
    return pl.pallas_call(
        _outproj_kernel,
        grid=(t // tm,),
        in_specs=[
            pl.BlockSpec((tm, half), row), pl.BlockSpec((tm, half), row),
            pl.BlockSpec(w_out.shape, const),
            pl.BlockSpec((tm, d), row),
            pl.BlockSpec((None, 1, d), bat),
            pl.BlockSpec((1, d), const),
            pl.BlockSpec((None, 1, d), bat), pl.BlockSpec((None, 1, d), bat),
            pl.BlockSpec(wr_hi.shape, const), pl.BlockSpec(wr_lo.shape, const),
            pl.BlockSpec((1, LANES), const),
        ],
        out_specs=[pl.BlockSpec((tm, d), row), pl.BlockSpec((tm, d), row), pl.BlockSpec((tm, LANES), row)],
        out_shape=[jax.ShapeDtypeStruct((t, d), F32), jax.ShapeDtypeStruct((t, d), F32),
                   jax.ShapeDtypeStruct((t, LANES), F32)],
        compiler_params=_params("arbitrary"),
        name="outproj_router",
    )(o_nsa, o_moba, w_out, x2d, g1, ln_g, sc2p, sh2, wr_hi, wr_lo, rb)


def _route_kernel(lg_ref, idx_ref, gate_ref):
    lg = lg_ref[...]
    ne = lg.shape[0]
    eidx = lax.broadcasted_iota(I32, lg.shape, 0)
    vals, idxs = [], []
    for _ in range(TOP_K):
        mx = jnp.max(lg, axis=0, keepdims=True)
        first = jnp.min(jnp.where(lg == mx, eidx, ne), axis=0, keepdims=True)
        vals.append(mx)
        idxs.append(first)
        lg = jnp.where(eidx == first, -jnp.inf, lg)
    e = [jnp.exp(v - vals[0]) for v in vals]
    tot = e[0]
    for x in e[1:]:
        tot = tot + x
    pad_i = jnp.zeros((SUBLANES - TOP_K, lg.shape[1]), I32)
    pad_f = jnp.zeros((SUBLANES - TOP_K, lg.shape[1]), F32)
    idx_ref[...] = jnp.concatenate(idxs + [pad_i], axis=0)
    gate_ref[...] = jnp.concatenate([x / tot for x in e] + [pad_f], axis=0)


def _route(logits_t):
    ne, t = logits_t.shape
    tt = min(2048, t)
    return pl.pallas_call(
        _route_kernel,
        grid=(t // tt,),
        in_specs=[pl.BlockSpec((ne, tt), lambda i: (0, i))],
        out_specs=[pl.BlockSpec((SUBLANES, tt), lambda i: (0, i)), pl.BlockSpec((SUBLANES, tt), lambda i: (0, i))],
        out_shape=[jax.ShapeDtypeStruct((SUBLANES, t), I32), jax.ShapeDtypeStruct((SUBLANES, t), F32)],
        compiler_params=_params("arbitrary"),
        name="route_topk",
    )(logits_t)


def _row_copy(src_hbm, row, dst, slot, sem):
    return pltpu.make_async_copy(src_hbm.at[pl.ds(row, 1)], dst.at[pl.ds(slot, 1)], sem)


def _moe_ffn_kernel(blk_e_ref, nused_ref, tok_ref, h_hbm, w1_ref, b1_ref, w2_ref, b2_ref, y_ref,
                    xbuf, sem, w1b, w2b):
    i = pl.program_id(0)
    rows = xbuf.shape[0]
    dff = w2_ref.shape[0]

    @pl.when(i < nused_ref[0])
    def _():
        def issue(r, c):
            _row_copy(h_hbm, tok_ref[0, r], xbuf, r, sem).start()
            return c

        lax.fori_loop(0, rows, issue, 0)

        prev = blk_e_ref[jnp.maximum(i - 1, 0)]

        @pl.when((i == 0) | (blk_e_ref[i] != prev))
        def _():
            w1b[...] = w1_ref[...].astype(BF16)
            w2b[...] = w2_ref[...].astype(BF16)

        def drain(r, c):
            _row_copy(h_hbm, 0, xbuf, r, sem).wait()
            return c

        lax.fori_loop(0, rows, drain, 0)

        u = _dot(xbuf[...].astype(BF16), w1b[...]) + b1_ref[...]
        gl = jnp.minimum(u[:, :dff], SWIGLU_LIMIT)
        lin = jnp.clip(u[:, dff:], -SWIGLU_LIMIT, SWIGLU_LIMIT)
        act = (lin + 1.0) * gl * _sigmoid(SWIGLU_ALPHA * gl)
        y_ref[...] = _dot(act.astype(BF16), w2b[...]) + b2_ref[...]

    @pl.when(i >= nused_ref[0])
    def _():
        y_ref[...] = jnp.zeros(y_ref.shape, y_ref.dtype)


def _moe_ffn(blk_e, nused, row_tok, h2, w1, b1, w2, b2):
    t, d = h2.shape
    ne, _, f2 = w1.shape
    dff = f2 // 2
    nb = row_tok.shape[0]
    tm = TM_MOE
    grid_spec = pltpu.PrefetchScalarGridSpec(
        num_scalar_prefetch=2,
        grid=(nb,),
        in_specs=[
            pl.BlockSpec((None, 1, tm), lambda i, be, nu: (i, 0, 0), memory_space=pltpu.SMEM),
            pl.BlockSpec(memory_space=pl.ANY),
            pl.BlockSpec((None, d, f2), lambda i, be, nu: (be[i], 0, 0)),
            pl.BlockSpec((None, 1, f2), lambda i, be, nu: (be[i], 0, 0)),
            pl.BlockSpec((None, dff, d), lambda i, be, nu: (be[i], 0, 0)),
            pl.BlockSpec((None, 1, d), lambda i, be, nu: (be[i], 0, 0)),
        ],
        out_specs=pl.BlockSpec((tm, d), lambda i, be, nu: (i, 0)),
        scratch_shapes=[
            pltpu.VMEM((tm, d), F32), pltpu.SemaphoreType.DMA(()),
            pltpu.VMEM((d, f2), BF16), pltpu.VMEM((dff, d), BF16),
        ],
    )
    return pl.pallas_call(
        _moe_ffn_kernel,
        grid_spec=grid_spec,
        out_shape=jax.ShapeDtypeStruct((nb * tm, d), F32),
        compiler_params=_params("arbitrary"),
        name="moe_ffn",
    )(blk_e, nused, row_tok, h2, w1, b1.reshape(ne, 1, f2), w2, b2.reshape(ne, 1, d))


def _moe_combine_kernel(pos_ref, y_hbm, gate_ref, x1_ref, g2_ref, o_ref, buf, sem):
    tc = x1_ref.shape[0]

    def issue(a, c):
        k = a % TOP_K
        tl = a // TOP_K
        pltpu.make_async_copy(y_hbm.at[pl.ds(pos_ref[0, a], 1)], buf.at[k, pl.ds(tl, 1)], sem).start()
        return c

    lax.fori_loop(0, tc * TOP_K, issue, 0)

    def drain(a, c):
        pltpu.make_async_copy(y_hbm.at[pl.ds(0, 1)], buf.at[0, pl.ds(0, 1)], sem).wait()
        return c

    lax.fori_loop(0, tc * TOP_K, drain, 0)
    gate = gate_ref[...]
    moe = gate[:, 0:1] * buf[0]
    for k in range(1, TOP_K):
        moe = moe + gate[:, k:k + 1] * buf[k]
    o_ref[...] = x1_ref[...] + g2_ref[...] * moe


def _moe_combine(pos, y, gate, x1, g2, seq):
    t, d = x1.shape
    tc = TC_COMB
    nb = seq // tc
    return pl.pallas_call(
        _moe_combine_kernel,
        grid=(t // tc,),
        in_specs=[
            pl.BlockSpec((None, 1, tc * TOP_K), lambda i: (i, 0, 0), memory_space=pltpu.SMEM),
            pl.BlockSpec(memory_space=pl.ANY),
            pl.BlockSpec((tc, TOP_K), lambda i: (i, 0)),
            pl.BlockSpec((tc, d), lambda i: (i, 0)),
            pl.BlockSpec((None, 1, d), lambda i: (i // nb, 0, 0)),
        ],
        out_specs=pl.BlockSpec((tc, d), lambda i: (i, 0)),
        out_shape=jax.ShapeDtypeStruct((t, d), F32),
        scratch_shapes=[pltpu.VMEM((TOP_K, tc, d), F32), pltpu.SemaphoreType.DMA(())],
        compiler_params=_params("arbitrary"),
        name="moe_combine",
    )(pos.reshape(t // tc, 1, tc * TOP_K), y, gate, x1, g2)


def _dispatch_plan(top_idx):
    t = top_idx.shape[0]
    a = t * TOP_K
    tm = TM_MOE
    e_flat = top_idx.reshape(a)
    order = jnp.argsort(e_flat)
    e_s = e_flat[order]
    counts = jnp.bincount(e_flat, length=N_EXPERTS)
    padded = (counts + tm - 1) // tm * tm
    start = jnp.cumsum(counts) - counts
    pend = jnp.cumsum(padded)
    pstart = pend - padded
    dest = (pstart[e_s] + jnp.arange(a) - start[e_s]).astype(I32)
    r = a + N_EXPERTS * tm
    nb = r // tm
    row_tok = jnp.zeros((r,), I32).at[dest].set((order // TOP_K).astype(I32))
    pos = jnp.zeros((a,), I32).at[order].set(dest)
    blk_e = jnp.minimum(jnp.searchsorted(pend, jnp.arange(nb) * tm, side='right'), N_EXPERTS - 1).astype(I32)
    nused = (pend[-1] // tm).astype(I32).reshape(1)
    return blk_e, nused, row_tok.reshape(nb, 1, tm), pos


def _tile_gain(gain, width, scale=1.0):
    return (jnp.tile(gain, width // HEAD_DIM) * scale).reshape(1, width).astype(F32)


def _heads_t(z, b, s, nh):
    return z.reshape(b, s, nh, HEAD_DIM).transpose(0, 2, 3, 1)


def _key_tiles(z, b, s, nh):
    return z.reshape(b, s // TK, TK, nh, HEAD_DIM).transpose(0, 3, 1, 2, 4)


def _value_tiles_t(z, b, s, nh):
    return z.reshape(b, s // TK, TK, nh, HEAD_DIM).transpose(0, 3, 1, 4, 2)


def _layer(x2d, mod, b, s, ln1_g, ln2_g, w_in, nsa_q_gain, nsa_k_gain, cmp_pos, cmp_w1, cmp_w2,
           moba_q_gain, moba_k_gain, w_out, router_w, router_b, exp_w1, exp_b1, exp_w2, exp_b2):
    t, d = x2d.shape
    sh1, sc1, g1, sh2, sc2, g2 = [m.reshape(b, 1, d) for m in jnp.split(mod, 6, axis=-1)]
    scale = HEAD_DIM ** -0.5

    (q, kcv, ks, vs, kw, vw, gates, mq, mk, mv, kmean) = _inproj(
        x2d, ln1_g.reshape(1, d), 1.0 + sc1, sh1, _pack_w_in(w_in),
        _tile_gain(nsa_q_gain, 512, scale), _tile_gain(nsa_k_gain, 128),
        _tile_gain(moba_q_gain, 512, scale), _tile_gain(moba_k_gain, 512), s)

    g = NSA_GROUPS
    xc = kcv.reshape(b, s, 2, g, HEAD_DIM).transpose(0, 2, 3, 1, 4).reshape(
        b, 2, g, s // CMP_STRIDE, CMP_STRIDE * HEAD_DIM)
    cmp = _compress(xc, cmp_pos.reshape(2, 1, CMP_LEN * HEAD_DIM), cmp_w1.astype(BF16),
                    cmp_w2.astype(BF16), nsa_k_gain.reshape(1, HEAD_DIM))
    kc = cmp[:, 0].astype(BF16)
    vct = cmp[:, 1].transpose(0, 1, 3, 2).astype(BF16)
    qt = _heads_t(q, b, s, NSA_HEADS)
    gates_t = gates.reshape(b, s, LANES).transpose(0, 2, 1)
    acc_t, sel_t = _nsa_cmp(qt, kc, vct, gates_t)
    o_nsa_t = _nsa_selwin(qt, _key_tiles(ks, b, s, g), _value_tiles_t(vs, b, s, g),
                          _key_tiles(kw, b, s, g), _value_tiles_t(vw, b, s, g), sel_t, gates_t, acc_t)
    o_nsa = o_nsa_t.transpose(0, 3, 1, 2).reshape(t, NSA_HEADS * HEAD_DIM)

    km = kmean.reshape(b, s // MOBA_BLOCK, MOBA_HEADS, HEAD_DIM).transpose(0, 2, 1, 3)
    o_moba_t = _moba(_heads_t(mq, b, s, MOBA_HEADS), _key_tiles(mk, b, s, MOBA_HEADS),
                     _value_tiles_t(mv, b, s, MOBA_HEADS), km)
    o_moba = o_moba_t.transpose(0, 3, 1, 2).reshape(t, MOBA_HEADS * HEAD_DIM)

    wr = jnp.zeros((d, LANES), F32).at[:, :N_EXPERTS].set(router_w)
    wr_hi = wr.astype(BF16)
    wr_lo = (wr - wr_hi.astype(F32)).astype(BF16)
    rb = jnp.zeros((1, LANES), F32).at[0, :N_EXPERTS].set(router_b)
    x1, h2, logits = _outproj(o_nsa, o_moba, w_out.astype(BF16), x2d, g1, ln2_g.reshape(1, d),
                              1.0 + sc2, sh2, wr_hi, wr_lo, rb, s)

    idx8, gate8 = _route(logits[:, :N_EXPERTS].T)
    top_idx = idx8[:TOP_K].T
    gate = gate8[:TOP_K].T
    blk_e, nused, row_tok, pos = _dispatch_plan(top_idx)
    y = _moe_ffn(blk_e, nused, row_tok, h2, exp_w1, exp_b1, exp_w2, exp_b2)
    return _moe_combine(pos, y, gate, x1, g2, s)


def kernel(x, c, ada_w, ada_b, ln1_g, ln2_g, w_in, nsa_q_gain, nsa_k_gain, nsa_cmp_pos, nsa_cmp_w1,
           nsa_cmp_w2, moba_q_gain, moba_k_gain, w_out, router_w, router_b, exp_w1, exp_b1, exp_w2, exp_b2):
    b, s, d = x.shape
    depth = ada_w.shape[0]
    mods = _adaln(c, ada_w, ada_b)
    x2d = x.reshape(b * s, d)
    for l in range(depth):
        x2d = _layer(x2d, mods[l], b, s, ln1_g[l], ln2_g[l], w_in[l], nsa_q_gain[l], nsa_k_gain[l],
                     nsa_cmp_pos[l], nsa_cmp_w1[l], nsa_cmp_w2[l], moba_q_gain[l], moba_k_gain[l],
                     w_out[l], router_w[l], router_b[l], exp_w1[l], exp_b1[l], exp_w2[l], exp_b2[l])
    return x2d.reshape(b, s, d)
```

```python
import functools

import numpy as np
import jax
import jax.numpy as jnp
from jax import lax
from jax.experimental import pallas as pl
from jax.experimental.pallas import tpu as pltpu

F32 = jnp.float32
BF16 = jnp.bfloat16
I32 = jnp.int32

HEAD_DIM = 64
NSA_HEADS = 8
NSA_GROUPS = 2
NSA_HPG = NSA_HEADS // NSA_GROUPS
CMP_LEN = 32
CMP_STRIDE = 16
CMP_HID = 128
SEL_BLOCK = 64
SEL_TOP = 16
WINDOW = 512
MOBA_HEADS = 8
MOBA_BLOCK = 256
MOBA_TOP = 3
N_EXPERTS = 32
TOP_K = 4
SWIGLU_LIMIT = 7.0
SWIGLU_ALPHA = 1.702
EPS = 1e-6
NEG = -1e30
FORCE_SCORE = 1e4

LANES = 128
SUBLANES = 8
TQ = 256
TK = 256
POS_SPLIT = 256
BIG = 1e30
MAX_SEL_BLOCKS = 128
MAX_MOBA_BLOCKS = 32
MOBA_HB = 4
TM_PROJ = 512
TM_MOE = 256
TC_COMB = 64
VMEM_LIMIT = 56 * 1024 * 1024


def _dot(a, b):
    return jnp.dot(a, b, preferred_element_type=F32)


def _hi_lo(x):
    hi = x.astype(BF16)
    lo = (x - hi.astype(F32)).astype(BF16)
    return hi, lo


def _sigmoid(x):
    return 1.0 / (1.0 + jnp.exp(-x))


def _params(*sem):
    return pltpu.CompilerParams(dimension_semantics=sem, vmem_limit_bytes=VMEM_LIMIT)


def _adaln_kernel(c_ref, w_ref, b_ref, o_ref):
    c = c_ref[...]
    cond = c * _sigmoid(c)
    ch, cl = _hi_lo(cond)
    wh, wl = _hi_lo(w_ref[...])
    o_ref[...] = _dot(ch, wh) + _dot(ch, wl) + _dot(cl, wh) + b_ref[...]


def _adaln(c, ada_w, ada_b):
    depth, d, n6 = ada_w.shape
    b = c.shape[0]
    tn = 1024
    c_pad = jnp.zeros((SUBLANES, d), F32).at[:b].set(c)
    out = pl.pallas_call(
        _adaln_kernel,
        grid=(depth, n6 // tn),
        in_specs=[
            pl.BlockSpec((SUBLANES, d), lambda l, j: (0, 0)),
            pl.BlockSpec((None, d, tn), lambda l, j: (l, 0, j)),
            pl.BlockSpec((None, 1, tn), lambda l, j: (l, 0, j)),
        ],
        out_specs=pl.BlockSpec((None, SUBLANES, tn), lambda l, j: (l, 0, j)),
        out_shape=jax.ShapeDtypeStruct((depth, SUBLANES, n6), F32),
        compiler_params=_params("arbitrary", "arbitrary"),
        name="adaln",
    )(c_pad, ada_w, ada_b.reshape(depth, 1, n6))
    return out[:, :b]


def _block_diag_ones():
    r = lax.broadcasted_iota(I32, (LANES, LANES), 0) // HEAD_DIM
    c = lax.broadcasted_iota(I32, (LANES, LANES), 1) // HEAD_DIM
    return jnp.where(r == c, 1.0, 0.0).astype(BF16)


def _head_norm(z, bd, gain):
    cols = []
    for c0 in range(0, z.shape[1], LANES):
        zc = z[:, c0:c0 + LANES]
        hi, lo = _hi_lo(zc * zc)
        ss = _dot(hi, bd) + _dot(lo, bd)
        cols.append(zc * lax.rsqrt(ss * (1.0 / HEAD_DIM) + EPS))
    zn = cols[0] if len(cols) == 1 else jnp.concatenate(cols, axis=1)
    return zn * gain


def _modulated_norm(x, g, sc1p, sh):
    ms = jnp.mean(x * x, axis=-1, keepdims=True)
    y = x * lax.rsqrt(ms + EPS)
    return (y * g) * sc1p + sh


def _inproj_kernel(x_ref, g_ref, sc_ref, sh_ref, w_ref, gq_ref, gk_ref, gmq_ref, gmk_ref,
                   q_ref, kcv_ref, ks_ref, vs_ref, kw_ref, vw_ref, gate_ref,
                   mq_ref, mk_ref, mv_ref, kmean_ref):
    h = _modulated_norm(x_ref[...], g_ref[...], sc_ref[...], sh_ref[...])
    hb = h.astype(BF16)
    bd = _block_diag_ones()

    def proj(a, b):
        return _dot(hb, w_ref[:, a:b])

    q_ref[...] = _head_norm(proj(0, 512), bd, gq_ref[...]).astype(BF16)
    kv = proj(512, 1280)
    kcv_ref[...] = kv[:, 0:256]
    ks_ref[...] = _head_norm(kv[:, 256:384], bd, gk_ref[...]).astype(BF16)
    vs_ref[...] = kv[:, 384:512].astype(BF16)
    kw_ref[...] = _head_norm(kv[:, 512:640], bd, gk_ref[...]).astype(BF16)
    vw_ref[...] = kv[:, 640:768].astype(BF16)
    gate_ref[...] = _sigmoid(proj(1280, 1408))
    mq_ref[...] = _head_norm(proj(1408, 1920), bd, gmq_ref[...]).astype(BF16)
    mkn = _head_norm(proj(1920, 2432), bd, gmk_ref[...])
    mk_ref[...] = mkn.astype(BF16)
    mv_ref[...] = proj(2432, 2944).astype(BF16)
    means = [jnp.sum(mkn[r:r + MOBA_BLOCK], axis=0, keepdims=True) * (1.0 / MOBA_BLOCK)
             for r in range(0, mkn.shape[0], MOBA_BLOCK)]
    kmean_ref[...] = jnp.concatenate(means, axis=0)


def _pack_w_in(w_in):
    d = w_in.shape[0]
    wg = jnp.zeros((d, LANES), w_in.dtype).at[:, :3 * NSA_HEADS].set(w_in[:, 1280:1304])
    return jnp.concatenate([w_in[:, :1280], wg, w_in[:, 1304:]], axis=1).astype(BF16)


def _inproj(x2d, ln_g, sc1p, sh, w_pack, gq, gk, gmq, gmk, seq):
    t, d = x2d.shape
    tm = TM_PROJ
    nb = seq // tm
    row = lambda i: (i, 0)
    bat = lambda i: (i // nb, 0, 0)
    const = lambda i: (0, 0)
    outs = [
        (512, BF16), (256, F32), (128, BF16), (128, BF16), (128, BF16), (128, BF16),
        (128, F32), (512, BF16), (512, BF16), (512, BF16),
    ]
    out_shape = [jax.ShapeDtypeStruct((t, w), dt) for w, dt in outs]
    out_specs = [pl.BlockSpec((tm, w), row) for w, _ in outs]
    nmean = tm // MOBA_BLOCK
    out_shape.append(jax.ShapeDtypeStruct((t // tm, nmean, 512), F32))
    out_specs.append(pl.BlockSpec((None, nmean, 512), lambda i: (i, 0, 0)))
    return pl.pallas_call(
        _inproj_kernel,
        grid=(t // tm,),
        in_specs=[
            pl.BlockSpec((tm, d), row),
            pl.BlockSpec((1, d), const),
            pl.BlockSpec((None, 1, d), bat),
            pl.BlockSpec((None, 1, d), bat),
            pl.BlockSpec(w_pack.shape, const),
            pl.BlockSpec((1, 512), const),
            pl.BlockSpec((1, 128), const),
            pl.BlockSpec((1, 512), const),
            pl.BlockSpec((1, 512), const),
        ],
        out_specs=out_specs,
        out_shape=out_shape,
        compiler_params=_params("arbitrary"),
        name="inproj",
    )(x2d, ln_g, sc1p, sh, w_pack, gq, gk, gmq, gmk)


def _compress_kernel(x_ref, pos_ref, w1_ref, w2_ref, gk_ref, o_ref):
    kv = pl.program_id(1)
    half = CMP_STRIDE * HEAD_DIM
    x = x_ref[...]
    xl = (x + pos_ref[:, :half]).astype(BF16)
    xh = (x + pos_ref[:, half:]).astype(BF16)
    a = _dot(xl, w1_ref[:half, :])
    b = _dot(xh, w1_ref[half:, :])
    hid = a + pltpu.roll(b, b.shape[0] - 1, 0)
    out = _dot(jax.nn.gelu(hid, approximate=True).astype(BF16), w2_ref[...])
    ms = jnp.mean(out * out, axis=-1, keepdims=True)
    normed = (out * lax.rsqrt(ms + EPS)) * gk_ref[...]
    o_ref[...] = jnp.where(kv == 0, normed, out)


def _compress(xc, pos, w1, w2, gk):
    b, _, g, nr, w = xc.shape
    return pl.pallas_call(
        _compress_kernel,
        grid=(b, 2, g),
        in_specs=[
            pl.BlockSpec((None, None, None, nr, w), lambda i, k, j: (i, k, j, 0, 0)),
            pl.BlockSpec((None, 1, w * 2), lambda i, k, j: (k, 0, 0)),
            pl.BlockSpec((None, w * 2, CMP_HID), lambda i, k, j: (k, 0, 0)),
            pl.BlockSpec((None, CMP_HID, HEAD_DIM), lambda i, k, j: (k, 0, 0)),
            pl.BlockSpec((1, HEAD_DIM), lambda i, k, j: (0, 0)),
        ],
        out_specs=pl.BlockSpec((None, None, None, nr, HEAD_DIM), lambda i, k, j: (i, k, j, 0, 0)),
        out_shape=jax.ShapeDtypeStruct((b, 2, g, nr, HEAD_DIM), F32),
        compiler_params=_params("arbitrary", "arbitrary", "arbitrary"),
        name="nsa_compress",
    )(xc, pos, w1, w2, gk)


def _nsa_slope(g, h):
    return 2.0 ** (-(g * NSA_HPG + h + 1))


def _rank_select(imp, n_top):
    nb = imp.shape[0]
    pieces = [imp[r:r + SUBLANES] for r in range(0, nb, SUBLANES)]
    counts = [jnp.zeros(p.shape, I32) for p in pieces]
    sub = lax.broadcasted_iota(I32, pieces[0].shape, 0)
    for jp in range(nb):
        row = imp[jp:jp + 1]
        rj = jp // SUBLANES
        for r, p in enumerate(pieces):
            if r < rj:
                inc = jnp.where(row > p, 1, 0)
            elif r > rj:
                inc = jnp.where(row >= p, 1, 0)
            else:
                inc = jnp.where(sub > jp % SUBLANES, jnp.where(row >= p, 1, 0), jnp.where(row > p, 1, 0))
            counts[r] = counts[r] + inc
    cnt = jnp.concatenate(counts, axis=0)
    return cnt < n_top


def _nsa_cmp_kernel(q_ref, kc_ref, vct_ref, inct_ref, gate_ref, acc_ref, sel_ref, *, n_top):
    g = pl.program_id(1)
    q0 = pl.program_id(2) * TQ
    nc = kc_ref.shape[0]
    nsel = inct_ref.shape[0]
    t = q0 + lax.broadcasted_iota(I32, (1, TQ), 1)
    n = lax.broadcasted_iota(I32, (nc, 1), 0)
    center = (CMP_STRIDE * n).astype(F32) + 0.5 * (CMP_LEN - 1)
    dist = t.astype(F32) - center
    mask = (CMP_STRIDE * n + (CMP_LEN - 1)) <= t
    kc = kc_ref[...]
    vct = vct_ref[...]
    psum = jnp.zeros((nc, TQ), F32)
    for h in range(NSA_HPG):
        slope = jnp.where(g == 0, _nsa_slope(0, h), _nsa_slope(1, h))
        s = _dot(kc, q_ref[h]) - slope * dist
        s = jnp.where(mask, s, NEG)
        m = jnp.max(s, axis=0, keepdims=True)
        e = jnp.exp(s - m)
        l = jnp.sum(e, axis=0, keepdims=True)
        p = jnp.where(mask, e, 0.0) * (1.0 / l)
        o = _dot(vct, p.astype(BF16))
        gate = gate_ref[pl.ds(g * (3 * NSA_HPG) + 3 * h, 1), :]
        acc_ref[h] = gate * o
        psum = psum + p
    hi, lo = _hi_lo(psum)
    inct = inct_ref[...]
    imp = _dot(inct, hi) + _dot(inct, lo)
    blk = lax.broadcasted_iota(I32, (nsel, 1), 0)
    cur = t // SEL_BLOCK
    forced = (blk == 0) | (blk == cur) | (blk == cur - 1)
    imp = jnp.where(forced, FORCE_SCORE, imp)
    imp = jnp.where(blk <= cur, imp, NEG)
    chosen = _rank_select(imp, n_top) & (blk <= cur)
    sel_ref[...] = jnp.where(chosen, 1.0, 0.0)


def _incidence_t(nc, nsel):
    cmp_start = CMP_STRIDE * np.arange(nc)
    sel_start = SEL_BLOCK * np.arange(nsel)
    inc = ((cmp_start[:, None] <= sel_start[None, :] + SEL_BLOCK - 1)
           & (cmp_start[:, None] + CMP_LEN - 1 >= sel_start[None, :]))
    inc[nc - 1] = False
    return jnp.asarray(inc.T, dtype=BF16)


def _nsa_cmp(qt, kc, vct, gates_t):
    b, _, _, s = qt.shape
    nc = kc.shape[2]
    nsel = s // SEL_BLOCK
    n_top = min(SEL_TOP, nsel)
    inct = _incidence_t(nc, nsel)
    return pl.pallas_call(
        functools.partial(_nsa_cmp_kernel, n_top=n_top),
        grid=(b, NSA_GROUPS, s // TQ),
        in_specs=[
            pl.BlockSpec((None, NSA_HPG, HEAD_DIM, TQ), lambda i, g, j: (i, g, 0, j)),
            pl.BlockSpec((None, None, nc, HEAD_DIM), lambda i, g, j: (i, g, 0, 0)),
            pl.BlockSpec((None, None, HEAD_DIM, nc), lambda i, g, j: (i, g, 0, 0)),
            pl.BlockSpec((nsel, nc), lambda i, g, j: (0, 0)),
            pl.BlockSpec((None, LANES, TQ), lambda i, g, j: (i, 0, j)),
        ],
        out_specs=[
            pl.BlockSpec((None, NSA_HPG, HEAD_DIM, TQ), lambda i, g, j: (i, g, 0, j)),
            pl.BlockSpec((None, None, nsel, TQ), lambda i, g, j: (i, g, 0, j)),
        ],
        out_shape=[
            jax.ShapeDtypeStruct((b, NSA_HEADS, HEAD_DIM, s), F32),
            jax.ShapeDtypeStruct((b, NSA_GROUPS, nsel, s), F32),
        ],
        compiler_params=_params("arbitrary", "arbitrary", "arbitrary"),
        name="nsa_cmp_select",
    )(qt, kc, vct, inct, gates_t)


def _alibi_rows(slope, q0, rows):
    r = lax.broadcasted_iota(I32, (rows, TQ), 0)
    return jnp.where(r == 0, slope * POS_SPLIT, jnp.where(r == 1, slope, jnp.where(r == 2, -slope * q0, 0.0)))


def _wide_dist(width):
    lane = lax.broadcasted_iota(I32, (TK, width), 1) & (TQ - 1)
    return lane - lax.broadcasted_iota(I32, (TK, width), 0)


def _flash_init(s, pv, m_ref, l_ref, acc_ref):
    m = jnp.max(s, axis=0, keepdims=True)
    p = jnp.exp(s - m)
    m_ref[...] = m
    l_ref[...] = jnp.sum(p, axis=0, keepdims=True)
    acc_ref[...] = pv(p.astype(BF16))


def _flash_update(s, pv, m_ref, l_ref, acc_ref):
    m_old = m_ref[...]
    m_new = jnp.maximum(m_old, jnp.max(s, axis=0, keepdims=True))
    alpha = jnp.exp(m_old - m_new)
    p = jnp.exp(s - m_new)
    m_ref[...] = m_new
    l_ref[...] = alpha * l_ref[...] + jnp.sum(p, axis=0, keepdims=True)
    acc_ref[...] = alpha * acc_ref[...] + pv(p.astype(BF16))


def _pos_cols(s, width):
    pos = np.arange(s)
    cols = np.zeros((s, width), np.float32)
    cols[:, 0] = pos // POS_SPLIT
    cols[:, 1] = pos % POS_SPLIT
    cols[:, 2] = 1.0
    return cols


def _onehot_cols(s, block, width):
    cols = np.zeros((s, width), np.float32)
    cols[np.arange(s), np.arange(s) // block] = 1.0
    return cols


def _aug_key_tiles(z, extra, b, s, nh):
    e = extra.shape[1]
    zk = z.reshape(b, s, nh, HEAD_DIM)
    ex = jnp.broadcast_to(jnp.asarray(extra, BF16)[None, :, None, :], (b, s, nh, e))
    za = jnp.concatenate([zk, ex], axis=-1)
    return za.reshape(b, s // TK, TK, nh, HEAD_DIM + e).transpose(0, 3, 1, 2, 4)


def _nsa_selwin_kernel(q_ref, ks_ref, vst_ref, kw_ref, vwt_ref, sel_ref, gate_ref, accin_ref, o_ref,
                       qa_ref, ms_ref, ls_ref, as_ref, mw_ref, lw_ref, aw_ref):
    g = pl.program_id(1)
    i = pl.program_id(2)
    q0 = (i * TQ).astype(F32)
    width = NSA_HPG * TQ
    selbias = ((sel_ref[...] - 1.0) * BIG).astype(BF16)
    for h in range(NSA_HPG):
        slope = jnp.where(g == 0, _nsa_slope(0, h), _nsa_slope(1, h))
        lanes = slice(h * TQ, (h + 1) * TQ)
        qa_ref[0:HEAD_DIM, lanes] = q_ref[h]
        qa_ref[HEAD_DIM:2 * HEAD_DIM, lanes] = _alibi_rows(slope, q0, HEAD_DIM).astype(BF16)
        qa_ref[2 * HEAD_DIM:, lanes] = selbias
    qa = qa_ref[...]
    qw = qa_ref[0:2 * HEAD_DIM, :]
    dist = _wide_dist(width)

    def pv_sel(j):
        return lambda p: _dot(vst_ref[j], p)

    def pv_win(j):
        return lambda p: _dot(vwt_ref[j], p)

    causal = dist >= 0
    _flash_init(jnp.where(causal, _dot(ks_ref[i], qa), NEG), pv_sel(i), ms_ref, ls_ref, as_ref)
    _flash_init(jnp.where(causal, _dot(kw_ref[i], qw), NEG), pv_win(i), mw_ref, lw_ref, aw_ref)

    def past(j, carry):
        _flash_update(_dot(ks_ref[j], qa), pv_sel(j), ms_ref, ls_ref, as_ref)
        return carry

    lax.fori_loop(0, i, past, 0)

    for d in range(1, WINDOW // TK + 1):
        @pl.when(i >= d)
        def _():
            s = jnp.where(dist + d * TK < WINDOW, _dot(kw_ref[i - d], qw), NEG)
            _flash_update(s, pv_win(i - d), mw_ref, lw_ref, aw_ref)

    o_sel = as_ref[...] / ls_ref[...]
    o_win = aw_ref[...] / lw_ref[...]
    for h in range(NSA_HPG):
        lanes = slice(h * TQ, (h + 1) * TQ)
        row = g * (3 * NSA_HPG) + 3 * h
        g_sel = gate_ref[pl.ds(row + 1, 1), :]
        g_win = gate_ref[pl.ds(row + 2, 1), :]
        o_ref[h] = (accin_ref[h] + g_sel * o_sel[:, lanes] + g_win * o_win[:, lanes]).astype(o_ref.dtype)


def _nsa_selwin(qt, ks_t, vs_t, kw_t, vw_t, sel_t, gates_t, acc_t):
    b, _, _, s = qt.shape
    nt = s // TK
    width = NSA_HPG * TQ
    ksw = ks_t.shape[-1]
    kww = kw_t.shape[-1]
    vspec = pl.BlockSpec((None, None, nt, HEAD_DIM, TK), lambda i, g, j: (i, g, 0, 0, 0))
    qspec = pl.BlockSpec((None, NSA_HPG, HEAD_DIM, TQ), lambda i, g, j: (i, g, 0, j))
    stat = pltpu.VMEM((1, width), F32)
    accs = pltpu.VMEM((HEAD_DIM, width), F32)
    return pl.pallas_call(
        _nsa_selwin_kernel,
        grid=(b, NSA_GROUPS, s // TQ),
        in_specs=[
            qspec,
            pl.BlockSpec((None, None, nt, TK, ksw), lambda i, g, j: (i, g, 0, 0, 0)), vspec,
            pl.BlockSpec((None, None, nt, TK, kww), lambda i, g, j: (i, g, 0, 0, 0)), vspec,
            pl.BlockSpec((None, None, MAX_SEL_BLOCKS, TQ), lambda i, g, j: (i, g, 0, j)),
            pl.BlockSpec((None, LANES, TQ), lambda i, g, j: (i, 0, j)),
            qspec,
        ],
        out_specs=qspec,
        out_shape=jax.ShapeDtypeStruct((b, NSA_HEADS, HEAD_DIM, s), BF16),
        scratch_shapes=[pltpu.VMEM((ksw, width), BF16), stat, stat, accs, stat, stat, accs],
        compiler_params=_params("arbitrary", "arbitrary", "arbitrary"),
        name="nsa_select_window",
    )(qt, ks_t, vs_t, kw_t, vw_t, sel_t, gates_t, acc_t)


def _moba_kernel(q_ref, k_ref, vt_ref, kmean_ref, o_ref, qa_ref, m_ref, l_ref, acc_ref, *, n_top):
    hg = pl.program_id(1)
    i = pl.program_id(2)
    nblk = kmean_ref.shape[1]
    q0 = (i * TQ).astype(F32)
    blk = lax.broadcasted_iota(I32, (nblk, 1), 0)
    for h in range(MOBA_HB):
        head = hg * MOBA_HB + h
        slope = jnp.float32(2.0 ** -MOBA_HEADS)
        for hh in range(MOBA_HEADS - 1):
            slope = jnp.where(head == hh, 2.0 ** -(hh + 1), slope)
        q = q_ref[h]
        gate = jnp.where(blk < i, _dot(kmean_ref[h].astype(BF16), q), NEG)
        bias = jnp.full(gate.shape, -BIG, F32)
        for _ in range(n_top):
            mx = jnp.max(gate, axis=0, keepdims=True)
            first = jnp.min(jnp.where(gate == mx, blk, nblk), axis=0, keepdims=True)
            pick = blk == first
            bias = jnp.where(pick, 0.0, bias)
            gate = jnp.where(pick, -jnp.inf, gate)
        bias = jnp.where(blk < i, bias, 0.0)
        if nblk < MAX_MOBA_BLOCKS:
            bias = jnp.concatenate([bias, jnp.zeros((MAX_MOBA_BLOCKS - nblk, TQ), F32)], axis=0)
        qa_ref[h, 0:HEAD_DIM, :] = q
        qa_ref[h, HEAD_DIM:HEAD_DIM + MAX_MOBA_BLOCKS, :] = _alibi_rows(slope, q0, MAX_MOBA_BLOCKS).astype(BF16)
        qa_ref[h, HEAD_DIM + MAX_MOBA_BLOCKS:, :] = bias.astype(BF16)

    def scores(j):
        return jnp.concatenate([_dot(k_ref[h, j], qa_ref[h]) for h in range(MOBA_HB)], axis=1)

    def pv(j):
        return lambda p: jnp.concatenate(
            [_dot(vt_ref[h, j], p[:, h * TQ:(h + 1) * TQ]) for h in range(MOBA_HB)], axis=1)

    s = jnp.where(_wide_dist(MOBA_HB * TQ) >= 0, scores(i), NEG)
    _flash_init(s, pv(i), m_ref, l_ref, acc_ref)

    def past(j, carry):
        _flash_update(scores(j), pv(j), m_ref, l_ref, acc_ref)
        return carry

    lax.fori_loop(0, i, past, 0)
    o = acc_ref[...] / l_ref[...]
    for h in range(MOBA_HB):
        o_ref[h] = o[:, h * TQ:(h + 1) * TQ].astype(o_ref.dtype)


def _moba(qt, k_t, v_t, kmean):
    b, nh, _, s = qt.shape
    nt = s // TK
    n_top = min(MOBA_TOP, nt - 1)
    hb = MOBA_HB
    kw = k_t.shape[-1]
    width = hb * TQ
    return pl.pallas_call(
        functools.partial(_moba_kernel, n_top=n_top),
        grid=(b, nh // hb, s // TQ),
        in_specs=[
            pl.BlockSpec((None, hb, HEAD_DIM, TQ), lambda i, h, j: (i, h, 0, j)),
            pl.BlockSpec((None, hb, nt, TK, kw), lambda i, h, j: (i, h, 0, 0, 0)),
            pl.BlockSpec((None, hb, nt, HEAD_DIM, TK), lambda i, h, j: (i, h, 0, 0, 0)),
            pl.BlockSpec((None, hb, nt, HEAD_DIM), lambda i, h, j: (i, h, 0, 0)),
        ],
        out_specs=pl.BlockSpec((None, hb, HEAD_DIM, TQ), lambda i, h, j: (i, h, 0, j)),
        out_shape=jax.ShapeDtypeStruct((b, nh, HEAD_DIM, s), BF16),
        scratch_shapes=[
            pltpu.VMEM((hb, kw, TQ), BF16), pltpu.VMEM((1, width), F32), pltpu.VMEM((1, width), F32),
            pltpu.VMEM((HEAD_DIM, width), F32),
        ],
        compiler_params=_params("arbitrary", "arbitrary", "arbitrary"),
        name="moba",
    )(qt, k_t, v_t, kmean)


def _outproj_kernel(on_ref, om_ref, w_ref, x_ref, g1_ref, ln_ref, sc_ref, sh_ref,
                    wrh_ref, wrl_ref, rb_ref, x1_ref, h2_ref, lg_ref):
    half = on_ref.shape[1]
    y = _dot(on_ref[...], w_ref[:half, :]) + _dot(om_ref[...], w_ref[half:, :])
    x1 = x_ref[...] + g1_ref[...] * y
    x1_ref[...] = x1
    h2 = _modulated_norm(x1, ln_ref[...], sc_ref[...], sh_ref[...])
    h2_ref[...] = h2
    hh, hl = _hi_lo(h2)
    lg_ref[...] = _dot(hh, wrh_ref[...]) + _dot(hh, wrl_ref[...]) + _dot(hl, wrh_ref[...]) + rb_ref[...]


def _outproj(o_nsa, o_moba, w_out, x2d, g1, ln_g, sc2p, sh2, wr_hi, wr_lo, rb, seq):
    t, d = x2d.shape
    tm = TM_PROJ
    nb = seq // tm
    row = lambda i: (i, 0)
    bat = lambda i: (i // nb, 0, 0)
    const = lambda i: (0, 0)
    half = o_nsa.shape[1]
    return pl.pallas_call(
        _outproj_kernel,
        grid=(t // tm,),
        in_specs=[
            pl.BlockSpec((tm, half), row), pl.BlockSpec((tm, half), row),
            pl.BlockSpec(w_out.shape, const),
            pl.BlockSpec((tm, d), row),
            pl.BlockSpec((None, 1, d), bat),
            pl.BlockSpec((1, d), const),
            pl.BlockSpec((None, 1, d), bat), pl.BlockSpec((None, 1, d), bat),
            pl.BlockSpec(wr_hi.shape, const), pl.BlockSpec(wr_lo.shape, const),
            pl.BlockSpec((1, LANES), const),
        ],
        out_specs=[pl.BlockSpec((tm, d), row), pl.BlockSpec((tm, d), row), pl.BlockSpec((tm, LANES), row)],
        out_shape=[jax.ShapeDtypeStruct((t, d), F32), jax.ShapeDtypeStruct((t, d), F32),
                   jax.ShapeDtypeStruct((t, LANES), F32)],
        compiler_params=_params("arbitrary"),
        name="outproj_router",
    )(o_nsa, o_moba, w_out, x2d, g1, ln_g, sc2p, sh2, wr_hi, wr_lo, rb)


def _route_kernel(lg_ref, idx_ref, gate_ref):
    lg = lg_ref[...]
    ne = lg.shape[0]
    eidx = lax.broadcasted_iota(I32, lg.shape, 0)
    vals, idxs = [], []
    for _ in range(TOP_K):
        mx = jnp.max(lg, axis=0, keepdims=True)
        first = jnp.min(jnp.where(lg == mx, eidx, ne), axis=0, keepdims=True)
        vals.append(mx)
        idxs.append(first)
        lg = jnp.where(eidx == first, -jnp.inf, lg)
    e = [jnp.exp(v - vals[0]) for v in vals]
    tot = e[0]
    for x in e[1:]:
        tot = tot + x
    pad_i = jnp.zeros((SUBLANES - TOP_K, lg.shape[1]), I32)
    pad_f = jnp.zeros((SUBLANES - TOP_K, lg.shape[1]), F32)
    idx_ref[...] = jnp.concatenate(idxs + [pad_i], axis=0)
    gate_ref[...] = jnp.concatenate([x / tot for x in e] + [pad_f], axis=0)


def _route(logits_t):
    ne, t = logits_t.shape
    tt = min(2048, t)
    return pl.pallas_call(
        _route_kernel,
        grid=(t // tt,),
        in_specs=[pl.BlockSpec((ne, tt), lambda i: (0, i))],
        out_specs=[pl.BlockSpec((SUBLANES, tt), lambda i: (0, i)), pl.BlockSpec((SUBLANES, tt), lambda i: (0, i))],
        out_shape=[jax.ShapeDtypeStruct((SUBLANES, t), I32), jax.ShapeDtypeStruct((SUBLANES, t), F32)],
        compiler_params=_params("arbitrary"),
        name="route_topk",
    )(logits_t)


def _row_copy(src_hbm, row, dst, slot, sem):
    return pltpu.make_async_copy(src_hbm.at[pl.ds(row, 1)], dst.at[pl.ds(slot, 1)], sem)


def _moe_ffn_kernel(blk_e_ref, nused_ref, tok_ref, h_hbm, w1_ref, b1_ref, w2_ref, b2_ref, y_ref,
                    xbuf, sem, w1b, w2b):
    i = pl.program_id(0)
    rows = xbuf.shape[0]
    dff = w2_ref.shape[0]

    @pl.when(i < nused_ref[0])
    def _():
        def issue(r, c):
            _row_copy(h_hbm, tok_ref[0, r], xbuf, r, sem).start()
            return c

        lax.fori_loop(0, rows, issue, 0)

        prev = blk_e_ref[jnp.maximum(i - 1, 0)]

        @pl.when((i == 0) | (blk_e_ref[i] != prev))
        def _():
            w1b[...] = w1_ref[...].astype(BF16)
            w2b[...] = w2_ref[...].astype(BF16)

        def drain(r, c):
            _row_copy(h_hbm, 0, xbuf, r, sem).wait()
            return c

        lax.fori_loop(0, rows, drain, 0)

        u = _dot(xbuf[...].astype(BF16), w1b[...]) + b1_ref[...]
        gl = jnp.minimum(u[:, :dff], SWIGLU_LIMIT)
        lin = jnp.clip(u[:, dff:], -SWIGLU_LIMIT, SWIGLU_LIMIT)
        act = (lin + 1.0) * gl * _sigmoid(SWIGLU_ALPHA * gl)
        y_ref[...] = _dot(act.astype(BF16), w2b[...]) + b2_ref[...]

    @pl.when(i >= nused_ref[0])
    def _():
        y_ref[...] = jnp.zeros(y_ref.shape, y_ref.dtype)


def _moe_ffn(blk_e, nused, row_tok, h2, w1, b1, w2, b2):
    t, d = h2.shape
    ne, _, f2 = w1.shape
    dff = f2 // 2
    nb = row_tok.shape[0]
    tm = TM_MOE
    grid_spec = pltpu.PrefetchScalarGridSpec(
        num_scalar_prefetch=2,
        grid=(nb,),
        in_specs=[
            pl.BlockSpec((None, 1, tm), lambda i, be, nu: (i, 0, 0), memory_space=pltpu.SMEM),
            pl.BlockSpec(memory_space=pl.ANY),
            pl.BlockSpec((None, d, f2), lambda i, be, nu: (be[i], 0, 0)),
            pl.BlockSpec((None, 1, f2), lambda i, be, nu: (be[i], 0, 0)),
            pl.BlockSpec((None, dff, d), lambda i, be, nu: (be[i], 0, 0)),
            pl.BlockSpec((None, 1, d), lambda i, be, nu: (be[i], 0, 0)),
        ],
        out_specs=pl.BlockSpec((tm, d), lambda i, be, nu: (i, 0)),
        scratch_shapes=[
            pltpu.VMEM((tm, d), F32), pltpu.SemaphoreType.DMA(()),
            pltpu.VMEM((d, f2), BF16), pltpu.VMEM((dff, d), BF16),
        ],
    )
    return pl.pallas_call(
        _moe_ffn_kernel,
        grid_spec=grid_spec,
        out_shape=jax.ShapeDtypeStruct((nb * tm, d), F32),
        compiler_params=_params("arbitrary"),
        name="moe_ffn",
    )(blk_e, nused, row_tok, h2, w1, b1.reshape(ne, 1, f2), w2, b2.reshape(ne, 1, d))


def _moe_combine_kernel(pos_ref, y_hbm, gate_ref, x1_ref, g2_ref, o_ref, buf, sem):
    tc = x1_ref.shape[0]

    def issue(a, c):
        k = a % TOP_K
        tl = a // TOP_K
        pltpu.make_async_copy(y_hbm.at[pl.ds(pos_ref[0, a], 1)], buf.at[k, pl.ds(tl, 1)], sem).start()
        return c

    lax.fori_loop(0, tc * TOP_K, issue, 0)

    def drain(a, c):
        pltpu.make_async_copy(y_hbm.at[pl.ds(0, 1)], buf.at[0, pl.ds(0, 1)], sem).wait()
        return c

    lax.fori_loop(0, tc * TOP_K, drain, 0)
    gate = gate_ref[...]
    moe = gate[:, 0:1] * buf[0]
    for k in range(1, TOP_K):
        moe = moe + gate[:, k:k + 1] * buf[k]
    o_ref[...] = x1_ref[...] + g2_ref[...] * moe


def _moe_combine(pos, y, gate, x1, g2, seq):
    t, d = x1.shape
    tc = TC_COMB
    nb = seq // tc
    return pl.pallas_call(
        _moe_combine_kernel,
        grid=(t // tc,),
        in_specs=[
            pl.BlockSpec((None, 1, tc * TOP_K), lambda i: (i, 0, 0), memory_space=pltpu.SMEM),
            pl.BlockSpec(memory_space=pl.ANY),
            pl.BlockSpec((tc, TOP_K), lambda i: (i, 0)),
            pl.BlockSpec((tc, d), lambda i: (i, 0)),
            pl.BlockSpec((None, 1, d), lambda i: (i // nb, 0, 0)),
        ],
        out_specs=pl.BlockSpec((tc, d), lambda i: (i, 0)),
        out_shape=jax.ShapeDtypeStruct((t, d), F32),
        scratch_shapes=[pltpu.VMEM((TOP_K, tc, d), F32), pltpu.SemaphoreType.DMA(())],
        compiler_params=_params("arbitrary"),
        name="moe_combine",
    )(pos.reshape(t // tc, 1, tc * TOP_K), y, gate, x1, g2)


def _dispatch_plan(top_idx):
    t = top_idx.shape[0]
    a = t * TOP_K
    tm = TM_MOE
    e_flat = top_idx.reshape(a)
    order = jnp.argsort(e_flat)
    e_s = e_flat[order]
    counts = jnp.bincount(e_flat, length=N_EXPERTS)
    padded = (counts + tm - 1) // tm * tm
    start = jnp.cumsum(counts) - counts
    pend = jnp.cumsum(padded)
    pstart = pend - padded
    dest = (pstart[e_s] + jnp.arange(a) - start[e_s]).astype(I32)
    r = a + N_EXPERTS * tm
    nb = r // tm
    row_tok = jnp.zeros((r,), I32).at[dest].set((order // TOP_K).astype(I32))
    pos = jnp.zeros((a,), I32).at[order].set(dest)
    blk_e = jnp.minimum(jnp.searchsorted(pend, jnp.arange(nb) * tm, side='right'), N_EXPERTS - 1).astype(I32)
    nused = (pend[-1] // tm).astype(I32).reshape(1)
    return blk_e, nused, row_tok.reshape(nb, 1, tm), pos


def _tile_gain(gain, width, scale=1.0):
    return (jnp.tile(gain, width // HEAD_DIM) * scale).reshape(1, width).astype(F32)


def _heads_t(z, b, s, nh):
    return z.reshape(b, s, nh, HEAD_DIM).transpose(0, 2, 3, 1)


def _key_tiles(z, b, s, nh):
    return z.reshape(b, s // TK, TK, nh, HEAD_DIM).transpose(0, 3, 1, 2, 4)


def _value_tiles_t(z, b, s, nh):
    return z.reshape(b, s // TK, TK, nh, HEAD_DIM).transpose(0, 3, 1, 4, 2)


def _layer(x2d, mod, b, s, ln1_g, ln2_g, w_in, nsa_q_gain, nsa_k_gain, cmp_pos, cmp_w1, cmp_w2,
           moba_q_gain, moba_k_gain, w_out, router_w, router_b, exp_w1, exp_b1, exp_w2, exp_b2):
    t, d = x2d.shape
    sh1, sc1, g1, sh2, sc2, g2 = [m.reshape(b, 1, d) for m in jnp.split(mod, 6, axis=-1)]
    scale = HEAD_DIM ** -0.5

    (q, kcv, ks, vs, kw, vw, gates, mq, mk, mv, kmean) = _inproj(
        x2d, ln1_g.reshape(1, d), 1.0 + sc1, sh1, _pack_w_in(w_in),
        _tile_gain(nsa_q_gain, 512, scale), _tile_gain(nsa_k_gain, 128),
        _tile_gain(moba_q_gain, 512, scale), _tile_gain(moba_k_gain, 512), s)

    g = NSA_GROUPS
    xc = kcv.reshape(b, s, 2, g, HEAD_DIM).transpose(0, 2, 3, 1, 4).reshape(
        b, 2, g, s // CMP_STRIDE, CMP_STRIDE * HEAD_DIM)
    cmp = _compress(xc, cmp_pos.reshape(2, 1, CMP_LEN * HEAD_DIM), cmp_w1.astype(BF16),
                    cmp_w2.astype(BF16), nsa_k_gain.reshape(1, HEAD_DIM))
    kc = cmp[:, 0].astype(BF16)
    vct = cmp[:, 1].transpose(0, 1, 3, 2).astype(BF16)
    qt = _heads_t(q, b, s, NSA_HEADS)
    gates_t = gates.reshape(b, s, LANES).transpose(0, 2, 1)
    acc_t, sel_t = _nsa_cmp(qt, kc, vct, gates_t)
    nsel = sel_t.shape[2]
    if nsel < MAX_SEL_BLOCKS:
        sel_t = jnp.concatenate([sel_t, jnp.ones((b, g, MAX_SEL_BLOCKS - nsel, s), F32)], axis=2)
    pos_cols = _pos_cols(s, HEAD_DIM)
    sel_cols = np.concatenate([pos_cols, _onehot_cols(s, SEL_BLOCK, MAX_SEL_BLOCKS)], axis=1)
    o_nsa_t = _nsa_selwin(qt, _aug_key_tiles(ks, sel_cols, b, s, g), _value_tiles_t(vs, b, s, g),
                          _aug_key_tiles(kw, pos_cols, b, s, g), _value_tiles_t(vw, b, s, g),
                          sel_t, gates_t, acc_t)
    o_nsa = o_nsa_t.transpose(0, 3, 1, 2).reshape(t, NSA_HEADS * HEAD_DIM)

    km = kmean.reshape(b, s // MOBA_BLOCK, MOBA_HEADS, HEAD_DIM).transpose(0, 2, 1, 3)
    moba_cols = np.concatenate([_pos_cols(s, MAX_MOBA_BLOCKS),
                                _onehot_cols(s, MOBA_BLOCK, MAX_MOBA_BLOCKS)], axis=1)
    o_moba_t = _moba(_heads_t(mq, b, s, MOBA_HEADS), _aug_key_tiles(mk, moba_cols, b, s, MOBA_HEADS),
                     _value_tiles_t(mv, b, s, MOBA_HEADS), km)
    o_moba = o_moba_t.transpose(0, 3, 1, 2).reshape(t, MOBA_HEADS * HEAD_DIM)

    wr = jnp.zeros((d, LANES), F32).at[:, :N_EXPERTS].set(router_w)
    wr_hi = wr.astype(BF16)
    wr_lo = (wr - wr_hi.astype(F32)).astype(BF16)
    rb = jnp.zeros((1, LANES), F32).at[0, :N_EXPERTS].set(router_b)
    x1, h2, logits = _outproj(o_nsa, o_moba, w_out.astype(BF16), x2d, g1, ln2_g.reshape(1, d),
                              1.0 + sc2, sh2, wr_hi, wr_lo, rb, s)

    idx8, gate8 = _route(logits[:, :N_EXPERTS].T)
    top_idx = idx8[:TOP_K].T
    gate = gate8[:TOP_K].T
    blk_e, nused, row_tok, pos = _dispatch_plan(top_idx)
    y = _moe_ffn(blk_e, nused, row_tok, h2, exp_w1, exp_b1, exp_w2, exp_b2)
    return _moe_combine(pos, y, gate, x1, g2, s)


def kernel(x, c, ada_w, ada_b, ln1_g, ln2_g, w_in, nsa_q_gain, nsa_k_gain, nsa_cmp_pos, nsa_cmp_w1,
           nsa_cmp_w2, moba_q_gain, moba_k_gain, w_out, router_w, router_b, exp_w1, exp_b1, exp_w2, exp_b2):
    b, s, d = x.shape
    depth = ada_w.shape[0]
    mods = _adaln(c, ada_w, ada_b)
    x2d = x.reshape(b * s, d)
    for l in range(depth):
        x2d = _layer(x2d, mods[l], b, s, ln1_g[l], ln2_g[l], w_in[l], nsa_q_gain[l], nsa_k_gain[l],
                     nsa_cmp_pos[l], nsa_cmp_w1[l], nsa_cmp_w2[l], moba_q_gain[l], moba_k_gain[l],
                     w_out[l], router_w[l], router_b[l], exp_w1[l], exp_b1[l], exp_w2[l], exp_b2[l])
    return x2d.reshape(b, s, d)
```

```python
import functools

import numpy as np
import jax
import jax.numpy as jnp
from jax import lax
from jax.experimental import pallas as pl
from jax.experimental.pallas import tpu as pltpu

F32 = jnp.float32
BF16 = jnp.bfloat16
I32 = jnp.int32

HEAD_DIM = 64
NSA_HEADS = 8
NSA_GROUPS = 2
NSA_HPG = NSA_HEADS // NSA_GROUPS
CMP_LEN = 32
CMP_STRIDE = 16
CMP_HID = 128
SEL_BLOCK = 64
SEL_TOP = 16
WINDOW = 512
MOBA_HEADS = 8
MOBA_BLOCK = 256
MOBA_TOP = 3
N_EXPERTS = 32
TOP_K = 4
SWIGLU_LIMIT = 7.0
SWIGLU_ALPHA = 1.702
EPS = 1e-6
NEG = -1e30
FORCE_SCORE = 1e4

LANES = 128
SUBLANES = 8
TQ = 256
TK = 256
POS_SPLIT = 256
BIG = 1e30
MAX_SEL_BLOCKS = 128
MAX_MOBA_BLOCKS = 32
MOBA_HB = 4
TM_PROJ = 512
TM_MOE = 256
TC_COMB = 64
TT_ROUTE = 512
TD_DISPATCH = 256
VMEM_LIMIT = 56 * 1024 * 1024


def _dot(a, b):
    return jnp.dot(a, b, preferred_element_type=F32)


def _hi_lo(x):
    hi = x.astype(BF16)
    lo = (x - hi.astype(F32)).astype(BF16)
    return hi, lo


def _sigmoid(x):
    return 1.0 / (1.0 + jnp.exp(-x))


def _params(*sem):
    return pltpu.CompilerParams(dimension_semantics=sem, vmem_limit_bytes=VMEM_LIMIT)


def _adaln_kernel(c_ref, w_ref, b_ref, o_ref):
    c = c_ref[...]
    cond = c * _sigmoid(c)
    ch, cl = _hi_lo(cond)
    wh, wl = _hi_lo(w_ref[...])
    o_ref[...] = _dot(ch, wh) + _dot(ch, wl) + _dot(cl, wh) + b_ref[...]


def _adaln(c, ada_w, ada_b):
    depth, d, n6 = ada_w.shape
    b = c.shape[0]
    tn = 1024
    c_pad = jnp.zeros((SUBLANES, d), F32).at[:b].set(c)
    out = pl.pallas_call(
        _adaln_kernel,
        grid=(depth, n6 // tn),
        in_specs=[
            pl.BlockSpec((SUBLANES, d), lambda l, j: (0, 0)),
            pl.BlockSpec((None, d, tn), lambda l, j: (l, 0, j)),
            pl.BlockSpec((None, 1, tn), lambda l, j: (l, 0, j)),
        ],
        out_specs=pl.BlockSpec((None, SUBLANES, tn), lambda l, j: (l, 0, j)),
        out_shape=jax.ShapeDtypeStruct((depth, SUBLANES, n6), F32),
        compiler_params=_params("arbitrary", "arbitrary"),
        name="adaln",
    )(c_pad, ada_w, ada_b.reshape(depth, 1, n6))
    return out[:, :b]


def _block_diag_ones():
    r = lax.broadcasted_iota(I32, (LANES, LANES), 0) // HEAD_DIM
    c = lax.broadcasted_iota(I32, (LANES, LANES), 1) // HEAD_DIM
    return jnp.where(r == c, 1.0, 0.0).astype(BF16)


def _head_norm(z, bd, gain):
    cols = []
    for c0 in range(0, z.shape[1], LANES):
        zc = z[:, c0:c0 + LANES]
        hi, lo = _hi_lo(zc * zc)
        ss = _dot(hi, bd) + _dot(lo, bd)
        cols.append(zc * lax.rsqrt(ss * (1.0 / HEAD_DIM) + EPS))
    zn = cols[0] if len(cols) == 1 else jnp.concatenate(cols, axis=1)
    return zn * gain


def _modulated_norm(x, g, sc1p, sh):
    ms = jnp.mean(x * x, axis=-1, keepdims=True)
    y = x * lax.rsqrt(ms + EPS)
    return (y * g) * sc1p + sh


def _inproj_kernel(x_ref, g_ref, sc_ref, sh_ref, w_ref, gq_ref, gk_ref, gmq_ref, gmk_ref,
                   q_ref, kcv_ref, ks_ref, vs_ref, kw_ref, vw_ref, gate_ref,
                   mq_ref, mk_ref, mv_ref, kmean_ref):
    h = _modulated_norm(x_ref[...], g_ref[...], sc_ref[...], sh_ref[...])
    hb = h.astype(BF16)
    bd = _block_diag_ones()

    def proj(a, b):
        return _dot(hb, w_ref[:, a:b])

    q_ref[...] = _head_norm(proj(0, 512), bd, gq_ref[...]).astype(BF16)
    kv = proj(512, 1280)
    kcv_ref[...] = kv[:, 0:256]
    ks_ref[...] = _head_norm(kv[:, 256:384], bd, gk_ref[...]).astype(BF16)
    vs_ref[...] = kv[:, 384:512].astype(BF16)
    kw_ref[...] = _head_norm(kv[:, 512:640], bd, gk_ref[...]).astype(BF16)
    vw_ref[...] = kv[:, 640:768].astype(BF16)
    gate_ref[...] = _sigmoid(proj(1280, 1408))
    mq_ref[...] = _head_norm(proj(1408, 1920), bd, gmq_ref[...]).astype(BF16)
    mkn = _head_norm(proj(1920, 2432), bd, gmk_ref[...])
    mk_ref[...] = mkn.astype(BF16)
    mv_ref[...] = proj(2432, 2944).astype(BF16)
    means = [jnp.sum(mkn[r:r + MOBA_BLOCK], axis=0, keepdims=True) * (1.0 / MOBA_BLOCK)
             for r in range(0, mkn.shape[0], MOBA_BLOCK)]
    kmean_ref[...] = jnp.concatenate(means, axis=0)


def _pack_w_in(w_in):
    d = w_in.shape[0]
    wg = jnp.zeros((d, LANES), w_in.dtype).at[:, :3 * NSA_HEADS].set(w_in[:, 1280:1304])
    return jnp.concatenate([w_in[:, :1280], wg, w_in[:, 1304:]], axis=1).astype(BF16)


def _inproj(x2d, ln_g, sc1p, sh, w_pack, gq, gk, gmq, gmk, seq):
    t, d = x2d.shape
    tm = TM_PROJ
    nb = seq // tm
    row = lambda i: (i, 0)
    bat = lambda i: (i // nb, 0, 0)
    const = lambda i: (0, 0)
    outs = [
        (512, BF16), (256, F32), (128, BF16), (128, BF16), (128, BF16), (128, BF16),
        (128, F32), (512, BF16), (512, BF16), (512, BF16),
    ]
    out_shape = [jax.ShapeDtypeStruct((t, w), dt) for w, dt in outs]
    out_specs = [pl.BlockSpec((tm, w), row) for w, _ in outs]
    nmean = tm // MOBA_BLOCK
    out_shape.append(jax.ShapeDtypeStruct((t // tm, nmean, 512), F32))
    out_specs.append(pl.BlockSpec((None, nmean, 512), lambda i: (i, 0, 0)))
    return pl.pallas_call(
        _inproj_kernel,
        grid=(t // tm,),
        in_specs=[
            pl.BlockSpec((tm, d), row),
            pl.BlockSpec((1, d), const),
            pl.BlockSpec((None, 1, d), bat),
            pl.BlockSpec((None, 1, d), bat),
            pl.BlockSpec(w_pack.shape, const),
            pl.BlockSpec((1, 512), const),
            pl.BlockSpec((1, 128), const),
            pl.BlockSpec((1, 512), const),
            pl.BlockSpec((1, 512), const),
        ],
        out_specs=out_specs,
        out_shape=out_shape,
        compiler_params=_params("arbitrary"),
        name="inproj",
    )(x2d, ln_g, sc1p, sh, w_pack, gq, gk, gmq, gmk)


def _compress_kernel(x_ref, pos_ref, w1_ref, w2_ref, gk_ref, o_ref):
    kv = pl.program_id(1)
    half = CMP_STRIDE * HEAD_DIM
    x = x_ref[...]
    xl = (x + pos_ref[:, :half]).astype(BF16)
    xh = (x + pos_ref[:, half:]).astype(BF16)
    a = _dot(xl, w1_ref[:half, :])
    b = _dot(xh, w1_ref[half:, :])
    hid = a + pltpu.roll(b, b.shape[0] - 1, 0)
    out = _dot(jax.nn.gelu(hid, approximate=True).astype(BF16), w2_ref[...])
    ms = jnp.mean(out * out, axis=-1, keepdims=True)
    normed = (out * lax.rsqrt(ms + EPS)) * gk_ref[...]
    o_ref[...] = jnp.where(kv == 0, normed, out)


def _compress(xc, pos, w1, w2, gk):
    b, _, g, nr, w = xc.shape
    return pl.pallas_call(
        _compress_kernel,
        grid=(b, 2, g),
        in_specs=[
            pl.BlockSpec((None, None, None, nr, w), lambda i, k, j: (i, k, j, 0, 0)),
            pl.BlockSpec((None, 1, w * 2), lambda i, k, j: (k, 0, 0)),
            pl.BlockSpec((None, w * 2, CMP_HID), lambda i, k, j: (k, 0, 0)),
            pl.BlockSpec((None, CMP_HID, HEAD_DIM), lambda i, k, j: (k, 0, 0)),
            pl.BlockSpec((1, HEAD_DIM), lambda i, k, j: (0, 0)),
        ],
        out_specs=pl.BlockSpec((None, None, None, nr, HEAD_DIM), lambda i, k, j: (i, k, j, 0, 0)),
        out_shape=jax.ShapeDtypeStruct((b, 2, g, nr, HEAD_DIM), F32),
        compiler_params=_params("arbitrary", "arbitrary", "arbitrary"),
        name="nsa_compress",
    )(xc, pos, w1, w2, gk)


def _nsa_slope(g, h):
    return 2.0 ** (-(g * NSA_HPG + h + 1))


def _rank_select(imp, n_top):
    nb = imp.shape[0]
    pieces = [imp[r:r + SUBLANES] for r in range(0, nb, SUBLANES)]
    counts = [jnp.zeros(p.shape, I32) for p in pieces]
    sub = lax.broadcasted_iota(I32, pieces[0].shape, 0)
    for jp in range(nb):
        row = imp[jp:jp + 1]
        rj = jp // SUBLANES
        for r, p in enumerate(pieces):
            if r < rj:
                inc = jnp.where(row > p, 1, 0)
            elif r > rj:
                inc = jnp.where(row >= p, 1, 0)
            else:
                inc = jnp.where(sub > jp % SUBLANES, jnp.where(row >= p, 1, 0), jnp.where(row > p, 1, 0))
            counts[r] = counts[r] + inc
    cnt = jnp.concatenate(counts, axis=0)
    return cnt < n_top


def _nsa_cmp_kernel(q_ref, kc_ref, vct_ref, inct_ref, gate_ref, acc_ref, sel_ref, *, n_top):
    g = pl.program_id(1)
    q0 = pl.program_id(2) * TQ
    nc = kc_ref.shape[0]
    nsel = inct_ref.shape[0]
    t = q0 + lax.broadcasted_iota(I32, (1, TQ), 1)
    n = lax.broadcasted_iota(I32, (nc, 1), 0)
    center = (CMP_STRIDE * n).astype(F32) + 0.5 * (CMP_LEN - 1)
    dist = t.astype(F32) - center
    mask = (CMP_STRIDE * n + (CMP_LEN - 1)) <= t
    kc = kc_ref[...]
    vct = vct_ref[...]
    psum = jnp.zeros((nc, TQ), F32)
    for h in range(NSA_HPG):
        slope = jnp.where(g == 0, _nsa_slope(0, h), _nsa_slope(1, h))
        s = _dot(kc, q_ref[h]) - slope * dist
        s = jnp.where(mask, s, NEG)
        m = jnp.max(s, axis=0, keepdims=True)
        e = jnp.exp(s - m)
        l = jnp.sum(e, axis=0, keepdims=True)
        p = jnp.where(mask, e, 0.0) * (1.0 / l)
        o = _dot(vct, p.astype(BF16))
        gate = gate_ref[pl.ds(g * (3 * NSA_HPG) + 3 * h, 1), :]
        acc_ref[h] = gate * o
        psum = psum + p
    hi, lo = _hi_lo(psum)
    inct = inct_ref[...]
    imp = _dot(inct, hi) + _dot(inct, lo)
    blk = lax.broadcasted_iota(I32, (nsel, 1), 0)
    cur = t // SEL_BLOCK
    forced = (blk == 0) | (blk == cur) | (blk == cur - 1)
    imp = jnp.where(forced, FORCE_SCORE, imp)
    imp = jnp.where(blk <= cur, imp, NEG)
    chosen = _rank_select(imp, n_top) & (blk <= cur)
    sel_ref[...] = jnp.where(chosen, 1.0, 0.0)


def _incidence_t(nc, nsel):
    cmp_start = CMP_STRIDE * np.arange(nc)
    sel_start = SEL_BLOCK * np.arange(nsel)
    inc = ((cmp_start[:, None] <= sel_start[None, :] + SEL_BLOCK - 1)
           & (cmp_start[:, None] + CMP_LEN - 1 >= sel_start[None, :]))
    inc[nc - 1] = False
    return jnp.asarray(inc.T, dtype=BF16)


def _nsa_cmp(qt, kc, vct, gates_t):
    b, _, _, s = qt.shape
    nc = kc.shape[2]
    nsel = s // SEL_BLOCK
    n_top = min(SEL_TOP, nsel)
    inct = _incidence_t(nc, nsel)
    return pl.pallas_call(
        functools.partial(_nsa_cmp_kernel, n_top=n_top),
        grid=(b, NSA_GROUPS, s // TQ),
        in_specs=[
            pl.BlockSpec((None, NSA_HPG, HEAD_DIM, TQ), lambda i, g, j: (i, g, 0, j)),
            pl.BlockSpec((None, None, nc, HEAD_DIM), lambda i, g, j: (i, g, 0, 0)),
            pl.BlockSpec((None, None, HEAD_DIM, nc), lambda i, g, j: (i, g, 0, 0)),
            pl.BlockSpec((nsel, nc), lambda i, g, j: (0, 0)),
            pl.BlockSpec((None, LANES, TQ), lambda i, g, j: (i, 0, j)),
        ],
        out_specs=[
            pl.BlockSpec((None, NSA_HPG, HEAD_DIM, TQ), lambda i, g, j: (i, g, 0, j)),
            pl.BlockSpec((None, None, nsel, TQ), lambda i, g, j: (i, g, 0, j)),
        ],
        out_shape=[
            jax.ShapeDtypeStruct((b, NSA_HEADS, HEAD_DIM, s), F32),
            jax.ShapeDtypeStruct((b, NSA_GROUPS, nsel, s), F32),
        ],
        compiler_params=_params("arbitrary", "arbitrary", "arbitrary"),
        name="nsa_cmp_select",
    )(qt, kc, vct, inct, gates_t)


def _alibi_rows(slope, q0, rows):
    r = lax.broadcasted_iota(I32, (rows, TQ), 0)
    return jnp.where(r == 0, slope * POS_SPLIT, jnp.where(r == 1, slope, jnp.where(r == 2, -slope * q0, 0.0)))


def _wide_dist(width):
    lane = lax.broadcasted_iota(I32, (TK, width), 1) & (TQ - 1)
    return lane - lax.broadcasted_iota(I32, (TK, width), 0)


def _flash_init(s, pv, m_ref, l_ref, acc_ref):
    m = jnp.max(s, axis=0, keepdims=True)
    p = jnp.exp(s - m)
    m_ref[...] = m
    l_ref[...] = jnp.sum(p, axis=0, keepdims=True)
    acc_ref[...] = pv(p.astype(BF16))


def _flash_update(s, pv, m_ref, l_ref, acc_ref):
    m_old = m_ref[...]
    m_new = jnp.maximum(m_old, jnp.max(s, axis=0, keepdims=True))
    alpha = jnp.exp(m_old - m_new)
    p = jnp.exp(s - m_new)
    m_ref[...] = m_new
    l_ref[...] = alpha * l_ref[...] + jnp.sum(p, axis=0, keepdims=True)
    acc_ref[...] = alpha * acc_ref[...] + pv(p.astype(BF16))


def _pos_cols(s, width):
    pos = np.arange(s)
    cols = np.zeros((s, width), np.float32)
    cols[:, 0] = pos // POS_SPLIT
    cols[:, 1] = pos % POS_SPLIT
    cols[:, 2] = 1.0
    return cols


def _onehot_cols(s, block, width):
    cols = np.zeros((s, width), np.float32)
    cols[np.arange(s), np.arange(s) // block] = 1.0
    return cols


def _aug_key_tiles(z, extra, b, s, nh):
    e = extra.shape[1]
    zk = z.reshape(b, s, nh, HEAD_DIM)
    ex = jnp.broadcast_to(jnp.asarray(extra, BF16)[None, :, None, :], (b, s, nh, e))
    za = jnp.concatenate([zk, ex], axis=-1)
    return za.reshape(b, s // TK, TK, nh, HEAD_DIM + e).transpose(0, 3, 1, 2, 4)


def _nsa_selwin_kernel(q_ref, ks_ref, vst_ref, kw_ref, vwt_ref, sel_ref, gate_ref, accin_ref, o_ref,
                       qa_ref, ms_ref, ls_ref, as_ref, mw_ref, lw_ref, aw_ref):
    g = pl.program_id(1)
    i = pl.program_id(2)
    q0 = (i * TQ).astype(F32)
    width = NSA_HPG * TQ
    selbias = ((sel_ref[...] - 1.0) * BIG).astype(BF16)
    for h in range(NSA_HPG):
        slope = jnp.where(g == 0, _nsa_slope(0, h), _nsa_slope(1, h))
        lanes = slice(h * TQ, (h + 1) * TQ)
        qa_ref[0:HEAD_DIM, lanes] = q_ref[h]
        qa_ref[HEAD_DIM:2 * HEAD_DIM, lanes] = _alibi_rows(slope, q0, HEAD_DIM).astype(BF16)
        qa_ref[2 * HEAD_DIM:, lanes] = selbias
    qa = qa_ref[...]
    qw = qa_ref[0:2 * HEAD_DIM, :]
    dist = _wide_dist(width)

    def pv_sel(j):
        return lambda p: _dot(vst_ref[j], p)

    def pv_win(j):
        return lambda p: _dot(vwt_ref[j], p)

    causal = dist >= 0
    _flash_init(jnp.where(causal, _dot(ks_ref[i], qa), NEG), pv_sel(i), ms_ref, ls_ref, as_ref)
    _flash_init(jnp.where(causal, _dot(kw_ref[i], qw), NEG), pv_win(i), mw_ref, lw_ref, aw_ref)

    def past(j, carry):
        _flash_update(_dot(ks_ref[j], qa), pv_sel(j), ms_ref, ls_ref, as_ref)
        return carry

    lax.fori_loop(0, i, past, 0)

    for d in range(1, WINDOW // TK + 1):
        @pl.when(i >= d)
        def _():
            s = jnp.where(dist + d * TK < WINDOW, _dot(kw_ref[i - d], qw), NEG)
            _flash_update(s, pv_win(i - d), mw_ref, lw_ref, aw_ref)

    o_sel = as_ref[...] / ls_ref[...]
    o_win = aw_ref[...] / lw_ref[...]
    for h in range(NSA_HPG):
        lanes = slice(h * TQ, (h + 1) * TQ)
        row = g * (3 * NSA_HPG) + 3 * h
        g_sel = gate_ref[pl.ds(row + 1, 1), :]
        g_win = gate_ref[pl.ds(row + 2, 1), :]
        o_ref[h] = (accin_ref[h] + g_sel * o_sel[:, lanes] + g_win * o_win[:, lanes]).astype(o_ref.dtype)


def _nsa_selwin(qt, ks_t, vs_t, kw_t, vw_t, sel_t, gates_t, acc_t):
    b, _, _, s = qt.shape
    nt = s // TK
    width = NSA_HPG * TQ
    ksw = ks_t.shape[-1]
    kww = kw_t.shape[-1]
    vspec = pl.BlockSpec((None, None, nt, HEAD_DIM, TK), lambda i, g, j: (i, g, 0, 0, 0))
    qspec = pl.BlockSpec((None, NSA_HPG, HEAD_DIM, TQ), lambda i, g, j: (i, g, 0, j))
    stat = pltpu.VMEM((1, width), F32)
    accs = pltpu.VMEM((HEAD_DIM, width), F32)
    return pl.pallas_call(
        _nsa_selwin_kernel,
        grid=(b, NSA_GROUPS, s // TQ),
        in_specs=[
            qspec,
            pl.BlockSpec((None, None, nt, TK, ksw), lambda i, g, j: (i, g, 0, 0, 0)), vspec,
            pl.BlockSpec((None, None, nt, TK, kww), lambda i, g, j: (i, g, 0, 0, 0)), vspec,
            pl.BlockSpec((None, None, MAX_SEL_BLOCKS, TQ), lambda i, g, j: (i, g, 0, j)),
            pl.BlockSpec((None, LANES, TQ), lambda i, g, j: (i, 0, j)),
            qspec,
        ],
        out_specs=qspec,
        out_shape=jax.ShapeDtypeStruct((b, NSA_HEADS, HEAD_DIM, s), BF16),
        scratch_shapes=[pltpu.VMEM((ksw, width), BF16), stat, stat, accs, stat, stat, accs],
        compiler_params=_params("arbitrary", "arbitrary", "arbitrary"),
        name="nsa_select_window",
    )(qt, ks_t, vs_t, kw_t, vw_t, sel_t, gates_t, acc_t)


def _moba_kernel(q_ref, k_ref, vt_ref, kmean_ref, o_ref, qa_ref, m_ref, l_ref, acc_ref, *, n_top):
    hg = pl.program_id(1)
    i = pl.program_id(2)
    nblk = kmean_ref.shape[1]
    q0 = (i * TQ).astype(F32)
    blk = lax.broadcasted_iota(I32, (nblk, 1), 0)
    for h in range(MOBA_HB):
        head = hg * MOBA_HB + h
        slope = jnp.float32(2.0 ** -MOBA_HEADS)
        for hh in range(MOBA_HEADS - 1):
            slope = jnp.where(head == hh, 2.0 ** -(hh + 1), slope)
        q = q_ref[h]
        gate = jnp.where(blk < i, _dot(kmean_ref[h].astype(BF16), q), NEG)
        bias = jnp.full(gate.shape, -BIG, F32)
        for _ in range(n_top):
            mx = jnp.max(gate, axis=0, keepdims=True)
            first = jnp.min(jnp.where(gate == mx, blk, nblk), axis=0, keepdims=True)
            pick = blk == first
            bias = jnp.where(pick, 0.0, bias)
            gate = jnp.where(pick, -jnp.inf, gate)
        bias = jnp.where(blk < i, bias, 0.0)
        if nblk < MAX_MOBA_BLOCKS:
            bias = jnp.concatenate([bias, jnp.zeros((MAX_MOBA_BLOCKS - nblk, TQ), F32)], axis=0)
        qa_ref[h, 0:HEAD_DIM, :] = q
        qa_ref[h, HEAD_DIM:HEAD_DIM + MAX_MOBA_BLOCKS, :] = _alibi_rows(slope, q0, MAX_MOBA_BLOCKS).astype(BF16)
        qa_ref[h, HEAD_DIM + MAX_MOBA_BLOCKS:, :] = bias.astype(BF16)

    def scores(j):
        return jnp.concatenate([_dot(k_ref[h, j], qa_ref[h]) for h in range(MOBA_HB)], axis=1)

    def pv(j):
        return lambda p: jnp.concatenate(
            [_dot(vt_ref[h, j], p[:, h * TQ:(h + 1) * TQ]) for h in range(MOBA_HB)], axis=1)

    s = jnp.where(_wide_dist(MOBA_HB * TQ) >= 0, scores(i), NEG)
    _flash_init(s, pv(i), m_ref, l_ref, acc_ref)

    def past(j, carry):
        _flash_update(scores(j), pv(j), m_ref, l_ref, acc_ref)
        return carry

    lax.fori_loop(0, i, past, 0)
    o = acc_ref[...] / l_ref[...]
    for h in range(MOBA_HB):
        o_ref[h] = o[:, h * TQ:(h + 1) * TQ].astype(o_ref.dtype)


def _moba(qt, k_t, v_t, kmean):
    b, nh, _, s = qt.shape
    nt = s // TK
    n_top = min(MOBA_TOP, nt - 1)
    hb = MOBA_HB
    kw = k_t.shape[-1]
    width = hb * TQ
    return pl.pallas_call(
        functools.partial(_moba_kernel, n_top=n_top),
        grid=(b, nh // hb, s // TQ),
        in_specs=[
            pl.BlockSpec((None, hb, HEAD_DIM, TQ), lambda i, h, j: (i, h, 0, j)),
            pl.BlockSpec((None, hb, nt, TK, kw), lambda i, h, j: (i, h, 0, 0, 0)),
            pl.BlockSpec((None, hb, nt, HEAD_DIM, TK), lambda i, h, j: (i, h, 0, 0, 0)),
            pl.BlockSpec((None, hb, nt, HEAD_DIM), lambda i, h, j: (i, h, 0, 0)),
        ],
        out_specs=pl.BlockSpec((None, hb, HEAD_DIM, TQ), lambda i, h, j: (i, h, 0, j)),
        out_shape=jax.ShapeDtypeStruct((b, nh, HEAD_DIM, s), BF16),
        scratch_shapes=[
            pltpu.VMEM((hb, kw, TQ), BF16), pltpu.VMEM((1, width), F32), pltpu.VMEM((1, width), F32),
            pltpu.VMEM((HEAD_DIM, width), F32),
        ],
        compiler_params=_params("arbitrary", "arbitrary", "arbitrary"),
        name="moba",
    )(qt, k_t, v_t, kmean)


def _outproj_kernel(on_ref, om_ref, w_ref, x_ref, g1_ref, ln_ref, sc_ref, sh_ref,
                    wrh_ref, wrl_ref, rb_ref, x1_ref, h2_ref, lg_ref):
    half = on_ref.shape[1]
    y = _dot(on_ref[...], w_ref[:half, :]) + _dot(om_ref[...], w_ref[half:, :])
    x1 = x_ref[...] + g1_ref[...] * y
    x1_ref[...] = x1
    h2 = _modulated_norm(x1, ln_ref[...], sc_ref[...], sh_ref[...])
    h2_ref[...] = h2
    hh, hl = _hi_lo(h2)
    lg_ref[...] = _dot(hh, wrh_ref[...]) + _dot(hh, wrl_ref[...]) + _dot(hl, wrh_ref[...]) + rb_ref[...]


def _outproj(o_nsa, o_moba, w_out, x2d, g1, ln_g, sc2p, sh2, wr_hi, wr_lo, rb, seq):
    t, d = x2d.shape
    tm = TM_PROJ
    nb = seq // tm
    row = lambda i: (i, 0)
    bat = lambda i: (i // nb, 0, 0)
    const = lambda i: (0, 0)
    half = o_nsa.shape[1]
    return pl.pallas_call(
        _outproj_kernel,
        grid=(t // tm,),
        in_specs=[
            pl.BlockSpec((tm, half), row), pl.BlockSpec((tm, half), row),
            pl.BlockSpec(w_out.shape, const),
            pl.BlockSpec((tm, d), row),
            pl.BlockSpec((None, 1, d), bat),
            pl.BlockSpec((1, d), const),
            pl.BlockSpec((None, 1, d), bat), pl.BlockSpec((None, 1, d), bat),
            pl.BlockSpec(wr_hi.shape, const), pl.BlockSpec(wr_lo.shape, const),
            pl.BlockSpec((1, LANES), const),
        ],
        out_specs=[pl.BlockSpec((tm, d), row), pl.BlockSpec((tm, d), row), pl.BlockSpec((tm, LANES), row)],
        out_shape=[jax.ShapeDtypeStruct((t, d), F32), jax.ShapeDtypeStruct((t, d), F32),
                   jax.ShapeDtypeStruct((t, LANES), F32)],
        compiler_params=_params("arbitrary"),
        name="outproj_router",
    )(o_nsa, o_moba, w_out, x2d, g1, ln_g, sc2p, sh2, wr_hi, wr_lo, rb)


def _route_kernel(lg_ref, idx_ref, gate_ref, rank_ref, cnt_ref, carry_ref):
    @pl.when(pl.program_id(0) == 0)
    def _():
        carry_ref[...] = jnp.zeros(carry_ref.shape, carry_ref.dtype)

    lg = lg_ref[...]
    ne, tt = lg.shape
    eidx = lax.broadcasted_iota(I32, lg.shape, 0)
    vals, idxs, hots = [], [], []
    for _ in range(TOP_K):
        mx = jnp.max(lg, axis=0, keepdims=True)
        first = jnp.min(jnp.where(lg == mx, eidx, ne), axis=0, keepdims=True)
        hot = eidx == first
        vals.append(mx)
        idxs.append(first)
        hots.append(hot)
        lg = jnp.where(hot, -jnp.inf, lg)
    e = [jnp.exp(v - vals[0]) for v in vals]
    tot = e[0]
    for x in e[1:]:
        tot = tot + x
    cnt = jnp.where(hots[0], 1.0, 0.0)
    for hot in hots[1:]:
        cnt = cnt + jnp.where(hot, 1.0, 0.0)
    cntb = cnt.astype(BF16)
    r = lax.broadcasted_iota(I32, (tt, tt), 0)
    c = lax.broadcasted_iota(I32, (tt, tt), 1)
    before = jnp.where(r < c, 1.0, 0.0).astype(BF16)
    carry = carry_ref[...]
    prefix = _dot(cntb, before) + carry
    ranks = [jnp.sum(jnp.where(hot, prefix, 0.0), axis=0, keepdims=True) for hot in hots]
    carry = carry + _dot(cntb, jnp.ones((tt, tt), BF16))
    carry_ref[...] = carry
    cnt_ref[...] = carry[:, :LANES]
    pad_i = jnp.zeros((SUBLANES - TOP_K, tt), I32)
    pad_f = jnp.zeros((SUBLANES - TOP_K, tt), F32)
    idx_ref[...] = jnp.concatenate(idxs + [pad_i], axis=0)
    gate_ref[...] = jnp.concatenate([x / tot for x in e] + [pad_f], axis=0)
    rank_ref[...] = jnp.concatenate([x.astype(I32) for x in ranks] + [pad_i], axis=0)


def _route(logits_t):
    ne, t = logits_t.shape
    tt = min(TT_ROUTE, t)
    tok = pl.BlockSpec((SUBLANES, tt), lambda i: (0, i))
    return pl.pallas_call(
        _route_kernel,
        grid=(t // tt,),
        in_specs=[pl.BlockSpec((ne, tt), lambda i: (0, i))],
        out_specs=[tok, tok, tok, pl.BlockSpec((ne, LANES), lambda i: (0, 0))],
        out_shape=[jax.ShapeDtypeStruct((SUBLANES, t), I32), jax.ShapeDtypeStruct((SUBLANES, t), F32),
                   jax.ShapeDtypeStruct((SUBLANES, t), I32), jax.ShapeDtypeStruct((ne, LANES), F32)],
        scratch_shapes=[pltpu.VMEM((ne, tt), F32)],
        compiler_params=_params("arbitrary"),
        name="route_topk_rank",
    )(logits_t)


def _dispatch_plan(idx, rank, counts):
    tm = TM_MOE
    t = idx.shape[1]
    nb = (t * TOP_K + N_EXPERTS * tm) // tm
    experts = jnp.arange(N_EXPERTS, dtype=I32)
    counts = counts.astype(I32)
    padded = (counts + tm - 1) // tm * tm
    pend = jnp.sum(jnp.where(experts[None, :] <= experts[:, None], padded[None, :], 0), axis=1)
    pstart = pend - padded
    base = jnp.sum(jnp.where(idx[:, :, None] == experts, pstart, 0), axis=-1)
    dest = (rank + base).T.reshape(t * TOP_K).astype(I32)
    blk_e = jnp.sum(jnp.where(pend[None, :] <= (jnp.arange(nb, dtype=I32) * tm)[:, None], 1, 0), axis=1)
    blk_e = jnp.minimum(blk_e, N_EXPERTS - 1).astype(I32)
    nused = (pend[-1] // tm).astype(I32).reshape(1)
    fill_start = jnp.maximum(pend - tm, 0).astype(I32)
    fill_flag = (padded > counts).astype(I32)
    return dest, blk_e, nused, fill_start, fill_flag


def _dispatch_kernel(fill_start_ref, fill_flag_ref, nused_ref, dest_ref, h_hbm, xs_hbm, zbuf, zsem, sem):
    i = pl.program_id(0)
    nrow = dest_ref.shape[1]
    ntok = nrow // TOP_K

    @pl.when(i == 0)
    def _():
        zbuf[...] = jnp.zeros(zbuf.shape, zbuf.dtype)

        def fill(start):
            return pltpu.make_async_copy(zbuf, xs_hbm.at[pl.ds(pl.multiple_of(start, TM_MOE), TM_MOE)], zsem)

        for e in range(N_EXPERTS):
            @pl.when(fill_flag_ref[e] != 0)
            def _():
                fill(fill_start_ref[e]).start()
        for e in range(N_EXPERTS):
            @pl.when(fill_flag_ref[e] != 0)
            def _():
                fill(fill_start_ref[e]).wait()

        nblk = xs_hbm.shape[0] // TM_MOE

        def tail_start(b, c):
            fill(b * TM_MOE).start()
            return c

        def tail_wait(b, c):
            fill(b * TM_MOE).wait()
            return c

        lax.fori_loop(nused_ref[0], nblk, tail_start, 0)
        lax.fori_loop(nused_ref[0], nblk, tail_wait, 0)

    def issue(tl, c):
        src = h_hbm.at[pl.ds(i * ntok + tl, 1)]
        for k in range(TOP_K):
            pltpu.make_async_copy(src, xs_hbm.at[pl.ds(dest_ref[0, tl * TOP_K + k], 1)], sem).start()
        return c

    lax.fori_loop(0, ntok, issue, 0)

    def drain():
        pltpu.make_async_copy(h_hbm.at[pl.ds(0, nrow)], xs_hbm.at[pl.ds(0, nrow)], sem).wait()

    @pl.when(i > 0)
    def _():
        drain()

    @pl.when(i == pl.num_programs(0) - 1)
    def _():
        drain()


def _dispatch(fill_start, fill_flag, nused, dest, h2):
    t, d = h2.shape
    td = min(TD_DISPATCH, t)
    r = t * TOP_K + N_EXPERTS * TM_MOE
    grid_spec = pltpu.PrefetchScalarGridSpec(
        num_scalar_prefetch=3,
        grid=(t // td,),
        in_specs=[
            pl.BlockSpec((None, 1, td * TOP_K), lambda i, fs, ff, nu: (i, 0, 0), memory_space=pltpu.SMEM),
            pl.BlockSpec(memory_space=pl.ANY),
        ],
        out_specs=pl.BlockSpec(memory_space=pl.ANY),
        scratch_shapes=[pltpu.VMEM((TM_MOE, d), F32), pltpu.SemaphoreType.DMA(()), pltpu.SemaphoreType.DMA(())],
    )
    return pl.pallas_call(
        _dispatch_kernel,
        grid_spec=grid_spec,
        out_shape=jax.ShapeDtypeStruct((r, d), F32),
        compiler_params=_params("arbitrary"),
        name="moe_dispatch",
    )(fill_start, fill_flag, nused, dest.reshape(t // td, 1, td * TOP_K), h2)


def _moe_ffn_kernel(blk_e_ref, nused_ref, x_ref, w1_ref, b1_ref, w2_ref, b2_ref, y_ref, w1b, w2b):
    i = pl.program_id(0)
    dff = w2_ref.shape[0]

    @pl.when(i < nused_ref[0])
    def _():
        prev = blk_e_ref[jnp.maximum(i - 1, 0)]

        @pl.when((i == 0) | (blk_e_ref[i] != prev))
        def _():
            w1b[...] = w1_ref[...].astype(BF16)
            w2b[...] = w2_ref[...].astype(BF16)

        u = _dot(x_ref[...].astype(BF16), w1b[...]) + b1_ref[...]
        gl = jnp.minimum(u[:, :dff], SWIGLU_LIMIT)
        lin = jnp.clip(u[:, dff:], -SWIGLU_LIMIT, SWIGLU_LIMIT)
        act = (lin + 1.0) * gl * _sigmoid(SWIGLU_ALPHA * gl)
        y_ref[...] = _dot(act.astype(BF16), w2b[...]) + b2_ref[...]

    @pl.when(i >= nused_ref[0])
    def _():
        y_ref[...] = jnp.zeros(y_ref.shape, y_ref.dtype)


def _moe_ffn(blk_e, nused, xs, w1, b1, w2, b2):
    r, d = xs.shape
    ne, _, f2 = w1.shape
    dff = f2 // 2
    tm = TM_MOE
    rows = lambda i, be, nu: (jnp.minimum(i, nu[0] - 1), 0)
    grid_spec = pltpu.PrefetchScalarGridSpec(
        num_scalar_prefetch=2,
        grid=(r // tm,),
        in_specs=[
            pl.BlockSpec((tm, d), rows),
            pl.BlockSpec((None, d, f2), lambda i, be, nu: (be[i], 0, 0)),
            pl.BlockSpec((None, 1, f2), lambda i, be, nu: (be[i], 0, 0)),
            pl.BlockSpec((None, dff, d), lambda i, be, nu: (be[i], 0, 0)),
            pl.BlockSpec((None, 1, d), lambda i, be, nu: (be[i], 0, 0)),
        ],
        out_specs=pl.BlockSpec((tm, d), lambda i, be, nu: (i, 0)),
        scratch_shapes=[pltpu.VMEM((d, f2), BF16), pltpu.VMEM((dff, d), BF16)],
    )
    return pl.pallas_call(
        _moe_ffn_kernel,
        grid_spec=grid_spec,
        out_shape=jax.ShapeDtypeStruct((r, d), F32),
        compiler_params=_params("arbitrary"),
        name="moe_ffn",
    )(blk_e, nused, xs, w1, b1.reshape(ne, 1, f2), w2, b2.reshape(ne, 1, d))


def _moe_combine_kernel(pos_ref, nxt_ref, y_hbm, gate_ref, x1_ref, g2_ref, o_ref, buf, sems):
    i = pl.program_id(0)
    n = pl.num_programs(0)
    tc = x1_ref.shape[0]
    slot = i % 2

    def fetch(p_ref, s):
        def issue(tl, c):
            for k in range(TOP_K):
                pltpu.make_async_copy(y_hbm.at[pl.ds(p_ref[0, tl * TOP_K + k], 1)],
                                      buf.at[s, k, pl.ds(tl, 1)], sems.at[s]).start()
            return c
        lax.fori_loop(0, tc, issue, 0)

    @pl.when(i == 0)
    def _():
        fetch(pos_ref, 0)

    @pl.when(i + 1 < n)
    def _():
        fetch(nxt_ref, 1 - slot)

    for k in range(TOP_K):
        pltpu.make_async_copy(y_hbm.at[pl.ds(0, tc)], buf.at[slot, k], sems.at[slot]).wait()
    gate = gate_ref[...]
    moe = gate[:, 0:1] * buf[slot, 0]
    for k in range(1, TOP_K):
        moe = moe + gate[:, k:k + 1] * buf[slot, k]
    o_ref[...] = x1_ref[...] + g2_ref[...] * moe


def _moe_combine(pos, y, gate, x1, g2, seq):
    t, d = x1.shape
    tc = TC_COMB
    nb = seq // tc
    n = t // tc
    pos3 = pos.reshape(n, 1, tc * TOP_K)
    return pl.pallas_call(
        _moe_combine_kernel,
        grid=(n,),
        in_specs=[
            pl.BlockSpec((None, 1, tc * TOP_K), lambda i: (i, 0, 0), memory_space=pltpu.SMEM),
            pl.BlockSpec((None, 1, tc * TOP_K), lambda i: (jnp.minimum(i + 1, n - 1), 0, 0),
                         memory_space=pltpu.SMEM),
            pl.BlockSpec(memory_space=pl.ANY),
            pl.BlockSpec((tc, TOP_K), lambda i: (i, 0)),
            pl.BlockSpec((tc, d), lambda i: (i, 0)),
            pl.BlockSpec((None, 1, d), lambda i: (i // nb, 0, 0)),
        ],
        out_specs=pl.BlockSpec((tc, d), lambda i: (i, 0)),
        out_shape=jax.ShapeDtypeStruct((t, d), F32),
        scratch_shapes=[pltpu.VMEM((2, TOP_K, tc, d), F32), pltpu.SemaphoreType.DMA((2,))],
        compiler_params=_params("arbitrary"),
        name="moe_combine",
    )(pos3, pos3, y, gate, x1, g2)


def _tile_gain(gain, width, scale=1.0):
    return (jnp.tile(gain, width // HEAD_DIM) * scale).reshape(1, width).astype(F32)


def _heads_t(z, b, s, nh):
    return z.reshape(b, s, nh, HEAD_DIM).transpose(0, 2, 3, 1)


def _key_tiles(z, b, s, nh):
    return z.reshape(b, s // TK, TK, nh, HEAD_DIM).transpose(0, 3, 1, 2, 4)


def _value_tiles_t(z, b, s, nh):
    return z.reshape(b, s // TK, TK, nh, HEAD_DIM).transpose(0, 3, 1, 4, 2)


def _layer(x2d, mod, b, s, ln1_g, ln2_g, w_in, nsa_q_gain, nsa_k_gain, cmp_pos, cmp_w1, cmp_w2,
           moba_q_gain, moba_k_gain, w_out, router_w, router_b, exp_w1, exp_b1, exp_w2, exp_b2):
    t, d = x2d.shape
    sh1, sc1, g1, sh2, sc2, g2 = [m.reshape(b, 1, d) for m in jnp.split(mod, 6, axis=-1)]
    scale = HEAD_DIM ** -0.5

    (q, kcv, ks, vs, kw, vw, gates, mq, mk, mv, kmean) = _inproj(
        x2d, ln1_g.reshape(1, d), 1.0 + sc1, sh1, _pack_w_in(w_in),
        _tile_gain(nsa_q_gain, 512, scale), _tile_gain(nsa_k_gain, 128),
        _tile_gain(moba_q_gain, 512, scale), _tile_gain(moba_k_gain, 512), s)

    g = NSA_GROUPS
    xc = kcv.reshape(b, s, 2, g, HEAD_DIM).transpose(0, 2, 3, 1, 4).reshape(
        b, 2, g, s // CMP_STRIDE, CMP_STRIDE * HEAD_DIM)
    cmp = _compress(xc, cmp_pos.reshape(2, 1, CMP_LEN * HEAD_DIM), cmp_w1.astype(BF16),
                    cmp_w2.astype(BF16), nsa_k_gain.reshape(1, HEAD_DIM))
    kc = cmp[:, 0].astype(BF16)
    vct = cmp[:, 1].transpose(0, 1, 3, 2).astype(BF16)
    qt = _heads_t(q, b, s, NSA_HEADS)
    gates_t = gates.reshape(b, s, LANES).transpose(0, 2, 1)
    acc_t, sel_t = _nsa_cmp(qt, kc, vct, gates_t)
    nsel = sel_t.shape[2]
    if nsel < MAX_SEL_BLOCKS:
        sel_t = jnp.concatenate([sel_t, jnp.ones((b, g, MAX_SEL_BLOCKS - nsel, s), F32)], axis=2)
    pos_cols = _pos_cols(s, HEAD_DIM)
    sel_cols = np.concatenate([pos_cols, _onehot_cols(s, SEL_BLOCK, MAX_SEL_BLOCKS)], axis=1)
    o_nsa_t = _nsa_selwin(qt, _aug_key_tiles(ks, sel_cols, b, s, g), _value_tiles_t(vs, b, s, g),
                          _aug_key_tiles(kw, pos_cols, b, s, g), _value_tiles_t(vw, b, s, g),
                          sel_t, gates_t, acc_t)
    o_nsa = o_nsa_t.transpose(0, 3, 1, 2).reshape(t, NSA_HEADS * HEAD_DIM)

    km = kmean.reshape(b, s // MOBA_BLOCK, MOBA_HEADS, HEAD_DIM).transpose(0, 2, 1, 3)
    moba_cols = np.concatenate([_pos_cols(s, MAX_MOBA_BLOCKS),
                                _onehot_cols(s, MOBA_BLOCK, MAX_MOBA_BLOCKS)], axis=1)
    o_moba_t = _moba(_heads_t(mq, b, s, MOBA_HEADS), _aug_key_tiles(mk, moba_cols, b, s, MOBA_HEADS),
                     _value_tiles_t(mv, b, s, MOBA_HEADS), km)
    o_moba = o_moba_t.transpose(0, 3, 1, 2).reshape(t, MOBA_HEADS * HEAD_DIM)

    wr = jnp.zeros((d, LANES), F32).at[:, :N_EXPERTS].set(router_w)
    wr_hi = wr.astype(BF16)
    wr_lo = (wr - wr_hi.astype(F32)).astype(BF16)
    rb = jnp.zeros((1, LANES), F32).at[0, :N_EXPERTS].set(router_b)
    x1, h2, logits = _outproj(o_nsa, o_moba, w_out.astype(BF16), x2d, g1, ln2_g.reshape(1, d),
                              1.0 + sc2, sh2, wr_hi, wr_lo, rb, s)

    idx8, gate8, rank8, counts = _route(logits[:, :N_EXPERTS].T)
    dest, blk_e, nused, fill_start, fill_flag = _dispatch_plan(idx8[:TOP_K], rank8[:TOP_K], counts[:, 0])
    xs = _dispatch(fill_start, fill_flag, nused, dest, h2)
    y = _moe_ffn(blk_e, nused, xs, exp_w1, exp_b1, exp_w2, exp_b2)
    return _moe_combine(dest, y, gate8[:TOP_K].T, x1, g2, s)


def kernel(x, c, ada_w, ada_b, ln1_g, ln2_g, w_in, nsa_q_gain, nsa_k_gain, nsa_cmp_pos, nsa_cmp_w1,
           nsa_cmp_w2, moba_q_gain, moba_k_gain, w_out, router_w, router_b, exp_w1, exp_b1, exp_w2, exp_b2):
    b, s, d = x.shape
    depth = ada_w.shape[0]
    mods = _adaln(c, ada_w, ada_b)
    x2d = x.reshape(b * s, d)
    for l in range(depth):
        x2d = _layer(x2d, mods[l], b, s, ln1_g[l], ln2_g[l], w_in[l], nsa_q_gain[l], nsa_k_gain[l],
                     nsa_cmp_pos[l], nsa_cmp_w1[l], nsa_cmp_w2[l], moba_q_gain[l], moba_k_gain[l],
                     w_out[l], router_w[l], router_b[l], exp_w1[l], exp_b1[l], exp_w2[l], exp_b2[l])
    return x2d.reshape(b, s, d)
```

```python
import functools

import numpy as np
import jax
import jax.numpy as jnp
from jax import lax
from jax.experimental import pallas as pl
from jax.experimental.pallas import tpu as pltpu

F32 = jnp.float32
BF16 = jnp.bfloat16
I32 = jnp.int32

HEAD_DIM = 64
NSA_HEADS = 8
NSA_GROUPS = 2
NSA_HPG = NSA_HEADS // NSA_GROUPS
CMP_LEN = 32
CMP_STRIDE = 16
CMP_HID = 128
SEL_BLOCK = 64
SEL_TOP = 16
WINDOW = 512
MOBA_HEADS = 8
MOBA_BLOCK = 256
MOBA_TOP = 3
N_EXPERTS = 32
TOP_K = 4
SWIGLU_LIMIT = 7.0
SWIGLU_ALPHA = 1.702
EPS = 1e-6
NEG = -1e30
FORCE_SCORE = 1e4

LANES = 128
SUBLANES = 8
TQ = 256
TK = 256
POS_SPLIT = 256
BIG = 1e30
MAX_SEL_BLOCKS = 128
MAX_MOBA_BLOCKS = 32
MOBA_HB = 4
TM_PROJ = 512
TM_MOE = 256
TC_COMB = 128
TT_ROUTE = 512
TD_DISPATCH = 512
VMEM_LIMIT = 56 * 1024 * 1024


def _dot(a, b):
    return jnp.dot(a, b, preferred_element_type=F32)


def _hi_lo(x):
    hi = x.astype(BF16)
    lo = (x - hi.astype(F32)).astype(BF16)
    return hi, lo


def _sigmoid(x):
    return 1.0 / (1.0 + jnp.exp(-x))


def _params(*sem):
    return pltpu.CompilerParams(dimension_semantics=sem, vmem_limit_bytes=VMEM_LIMIT)


def _adaln_kernel(c_ref, w_ref, b_ref, o_ref):
    c = c_ref[...]
    cond = c * _sigmoid(c)
    ch, cl = _hi_lo(cond)
    wh, wl = _hi_lo(w_ref[...])
    o_ref[...] = _dot(ch, wh) + _dot(ch, wl) + _dot(cl, wh) + b_ref[...]


def _adaln(c, ada_w, ada_b):
    depth, d, n6 = ada_w.shape
    b = c.shape[0]
    tn = 1024
    c_pad = jnp.zeros((SUBLANES, d), F32).at[:b].set(c)
    out = pl.pallas_call(
        _adaln_kernel,
        grid=(depth, n6 // tn),
        in_specs=[
            pl.BlockSpec((SUBLANES, d), lambda l, j: (0, 0)),
            pl.BlockSpec((None, d, tn), lambda l, j: (l, 0, j)),
            pl.BlockSpec((None, 1, tn), lambda l, j: (l, 0, j)),
        ],
        out_specs=pl.BlockSpec((None, SUBLANES, tn), lambda l, j: (l, 0, j)),
        out_shape=jax.ShapeDtypeStruct((depth, SUBLANES, n6), F32),
        compiler_params=_params("arbitrary", "arbitrary"),
        name="adaln",
    )(c_pad, ada_w, ada_b.reshape(depth, 1, n6))
    return out[:, :b]


def _block_diag_ones():
    r = lax.broadcasted_iota(I32, (LANES, LANES), 0) // HEAD_DIM
    c = lax.broadcasted_iota(I32, (LANES, LANES), 1) // HEAD_DIM
    return jnp.where(r == c, 1.0, 0.0).astype(BF16)


def _head_norm(z, bd, gain):
    cols = []
    for c0 in range(0, z.shape[1], LANES):
        zc = z[:, c0:c0 + LANES]
        hi, lo = _hi_lo(zc * zc)
        ss = _dot(hi, bd) + _dot(lo, bd)
        cols.append(zc * lax.rsqrt(ss * (1.0 / HEAD_DIM) + EPS))
    zn = cols[0] if len(cols) == 1 else jnp.concatenate(cols, axis=1)
    return zn * gain


def _modulated_norm(x, g, sc1p, sh):
    ms = jnp.mean(x * x, axis=-1, keepdims=True)
    y = x * lax.rsqrt(ms + EPS)
    return (y * g) * sc1p + sh


def _inproj_kernel(x_ref, g_ref, sc_ref, sh_ref, w_ref, gq_ref, gk_ref, gmq_ref, gmk_ref,
                   q_ref, kcv_ref, ks_ref, vs_ref, kw_ref, vw_ref, gate_ref,
                   mq_ref, mk_ref, mv_ref, kmean_ref):
    h = _modulated_norm(x_ref[...], g_ref[...], sc_ref[...], sh_ref[...])
    hb = h.astype(BF16)
    bd = _block_diag_ones()

    def proj(a, b):
        return _dot(hb, w_ref[:, a:b])

    q_ref[...] = _head_norm(proj(0, 512), bd, gq_ref[...]).astype(BF16)
    kv = proj(512, 1280)
    kcv_ref[...] = kv[:, 0:256]
    ks_ref[...] = _head_norm(kv[:, 256:384], bd, gk_ref[...]).astype(BF16)
    vs_ref[...] = kv[:, 384:512].astype(BF16)
    kw_ref[...] = _head_norm(kv[:, 512:640], bd, gk_ref[...]).astype(BF16)
    vw_ref[...] = kv[:, 640:768].astype(BF16)
    gate_ref[...] = _sigmoid(proj(1280, 1408))
    mq_ref[...] = _head_norm(proj(1408, 1920), bd, gmq_ref[...]).astype(BF16)
    mkn = _head_norm(proj(1920, 2432), bd, gmk_ref[...])
    mk_ref[...] = mkn.astype(BF16)
    mv_ref[...] = proj(2432, 2944).astype(BF16)
    means = [jnp.sum(mkn[r:r + MOBA_BLOCK], axis=0, keepdims=True) * (1.0 / MOBA_BLOCK)
             for r in range(0, mkn.shape[0], MOBA_BLOCK)]
    kmean_ref[...] = jnp.concatenate(means, axis=0)


def _pack_w_in(w_in):
    d = w_in.shape[0]
    wg = jnp.zeros((d, LANES), w_in.dtype).at[:, :3 * NSA_HEADS].set(w_in[:, 1280:1304])
    return jnp.concatenate([w_in[:, :1280], wg, w_in[:, 1304:]], axis=1).astype(BF16)


def _inproj(x2d, ln_g, sc1p, sh, w_pack, gq, gk, gmq, gmk, seq):
    t, d = x2d.shape
    tm = TM_PROJ
    nb = seq // tm
    row = lambda i: (i, 0)
    bat = lambda i: (i // nb, 0, 0)
    const = lambda i: (0, 0)
    outs = [
        (512, BF16), (256, F32), (128, BF16), (128, BF16), (128, BF16), (128, BF16),
        (128, F32), (512, BF16), (512, BF16), (512, BF16),
    ]
    out_shape = [jax.ShapeDtypeStruct((t, w), dt) for w, dt in outs]
    out_specs = [pl.BlockSpec((tm, w), row) for w, _ in outs]
    nmean = tm // MOBA_BLOCK
    out_shape.append(jax.ShapeDtypeStruct((t // tm, nmean, 512), F32))
    out_specs.append(pl.BlockSpec((None, nmean, 512), lambda i: (i, 0, 0)))
    return pl.pallas_call(
        _inproj_kernel,
        grid=(t // tm,),
        in_specs=[
            pl.BlockSpec((tm, d), row),
            pl.BlockSpec((1, d), const),
            pl.BlockSpec((None, 1, d), bat),
            pl.BlockSpec((None, 1, d), bat),
            pl.BlockSpec(w_pack.shape, const),
            pl.BlockSpec((1, 512), const),
            pl.BlockSpec((1, 128), const),
            pl.BlockSpec((1, 512), const),
            pl.BlockSpec((1, 512), const),
        ],
        out_specs=out_specs,
        out_shape=out_shape,
        compiler_params=_params("arbitrary"),
        name="inproj",
    )(x2d, ln_g, sc1p, sh, w_pack, gq, gk, gmq, gmk)


def _compress_kernel(x_ref, pos_ref, w1_ref, w2_ref, gk_ref, o_ref):
    kv = pl.program_id(1)
    half = CMP_STRIDE * HEAD_DIM
    x = x_ref[...]
    xl = (x + pos_ref[:, :half]).astype(BF16)
    xh = (x + pos_ref[:, half:]).astype(BF16)
    a = _dot(xl, w1_ref[:half, :])
    b = _dot(xh, w1_ref[half:, :])
    hid = a + pltpu.roll(b, b.shape[0] - 1, 0)
    out = _dot(jax.nn.gelu(hid, approximate=True).astype(BF16), w2_ref[...])
    ms = jnp.mean(out * out, axis=-1, keepdims=True)
    normed = (out * lax.rsqrt(ms + EPS)) * gk_ref[...]
    o_ref[...] = jnp.where(kv == 0, normed, out)


def _compress(xc, pos, w1, w2, gk):
    b, _, g, nr, w = xc.shape
    return pl.pallas_call(
        _compress_kernel,
        grid=(b, 2, g),
        in_specs=[
            pl.BlockSpec((None, None, None, nr, w), lambda i, k, j: (i, k, j, 0, 0)),
            pl.BlockSpec((None, 1, w * 2), lambda i, k, j: (k, 0, 0)),
            pl.BlockSpec((None, w * 2, CMP_HID), lambda i, k, j: (k, 0, 0)),
            pl.BlockSpec((None, CMP_HID, HEAD_DIM), lambda i, k, j: (k, 0, 0)),
            pl.BlockSpec((1, HEAD_DIM), lambda i, k, j: (0, 0)),
        ],
        out_specs=pl.BlockSpec((None, None, None, nr, HEAD_DIM), lambda i, k, j: (i, k, j, 0, 0)),
        out_shape=jax.ShapeDtypeStruct((b, 2, g, nr, HEAD_DIM), F32),
        compiler_params=_params("arbitrary", "arbitrary", "arbitrary"),
        name="nsa_compress",
    )(xc, pos, w1, w2, gk)


def _nsa_slope(g, h):
    return 2.0 ** (-(g * NSA_HPG + h + 1))


def _rank_select(imp, n_top):
    nb = imp.shape[0]
    pieces = [imp[r:r + SUBLANES] for r in range(0, nb, SUBLANES)]
    counts = [jnp.zeros(p.shape, I32) for p in pieces]
    sub = lax.broadcasted_iota(I32, pieces[0].shape, 0)
    for jp in range(nb):
        row = imp[jp:jp + 1]
        rj = jp // SUBLANES
        for r, p in enumerate(pieces):
            if r < rj:
                inc = jnp.where(row > p, 1, 0)
            elif r > rj:
                inc = jnp.where(row >= p, 1, 0)
            else:
                inc = jnp.where(sub > jp % SUBLANES, jnp.where(row >= p, 1, 0), jnp.where(row > p, 1, 0))
            counts[r] = counts[r] + inc
    cnt = jnp.concatenate(counts, axis=0)
    return cnt < n_top


def _nsa_cmp_kernel(q_ref, kc_ref, vct_ref, inct_ref, gate_ref, acc_ref, sel_ref, *, n_top):
    g = pl.program_id(1)
    q0 = pl.program_id(2) * TQ
    nc = kc_ref.shape[0]
    nsel = inct_ref.shape[0]
    t = q0 + lax.broadcasted_iota(I32, (1, TQ), 1)
    n = lax.broadcasted_iota(I32, (nc, 1), 0)
    center = (CMP_STRIDE * n).astype(F32) + 0.5 * (CMP_LEN - 1)
    dist = t.astype(F32) - center
    mask = (CMP_STRIDE * n + (CMP_LEN - 1)) <= t
    kc = kc_ref[...]
    vct = vct_ref[...]
    psum = jnp.zeros((nc, TQ), F32)
    for h in range(NSA_HPG):
        slope = jnp.where(g == 0, _nsa_slope(0, h), _nsa_slope(1, h))
        s = _dot(kc, q_ref[h]) - slope * dist
        s = jnp.where(mask, s, NEG)
        m = jnp.max(s, axis=0, keepdims=True)
        e = jnp.exp(s - m)
        l = jnp.sum(e, axis=0, keepdims=True)
        p = jnp.where(mask, e, 0.0) * (1.0 / l)
        o = _dot(vct, p.astype(BF16))
        gate = gate_ref[pl.ds(g * (3 * NSA_HPG) + 3 * h, 1), :]
        acc_ref[h] = gate * o
        psum = psum + p
    hi, lo = _hi_lo(psum)
    inct = inct_ref[...]
    imp = _dot(inct, hi) + _dot(inct, lo)
    blk = lax.broadcasted_iota(I32, (nsel, 1), 0)
    cur = t // SEL_BLOCK
    forced = (blk == 0) | (blk == cur) | (blk == cur - 1)
    imp = jnp.where(forced, FORCE_SCORE, imp)
    imp = jnp.where(blk <= cur, imp, NEG)
    chosen = _rank_select(imp, n_top) & (blk <= cur)
    sel_ref[...] = jnp.where(chosen, 1.0, 0.0)


def _incidence_t(nc, nsel):
    cmp_start = CMP_STRIDE * np.arange(nc)
    sel_start = SEL_BLOCK * np.arange(nsel)
    inc = ((cmp_start[:, None] <= sel_start[None, :] + SEL_BLOCK - 1)
           & (cmp_start[:, None] + CMP_LEN - 1 >= sel_start[None, :]))
    inc[nc - 1] = False
    return jnp.asarray(inc.T, dtype=BF16)


def _nsa_cmp(qt, kc, vct, gates_t):
    b, _, _, s = qt.shape
    nc = kc.shape[2]
    nsel = s // SEL_BLOCK
    n_top = min(SEL_TOP, nsel)
    inct = _incidence_t(nc, nsel)
    return pl.pallas_call(
        functools.partial(_nsa_cmp_kernel, n_top=n_top),
        grid=(b, NSA_GROUPS, s // TQ),
        in_specs=[
            pl.BlockSpec((None, NSA_HPG, HEAD_DIM, TQ), lambda i, g, j: (i, g, 0, j)),
            pl.BlockSpec((None, None, nc, HEAD_DIM), lambda i, g, j: (i, g, 0, 0)),
            pl.BlockSpec((None, None, HEAD_DIM, nc), lambda i, g, j: (i, g, 0, 0)),
            pl.BlockSpec((nsel, nc), lambda i, g, j: (0, 0)),
            pl.BlockSpec((None, LANES, TQ), lambda i, g, j: (i, 0, j)),
        ],
        out_specs=[
            pl.BlockSpec((None, NSA_HPG, HEAD_DIM, TQ), lambda i, g, j: (i, g, 0, j)),
            pl.BlockSpec((None, None, nsel, TQ), lambda i, g, j: (i, g, 0, j)),
        ],
        out_shape=[
            jax.ShapeDtypeStruct((b, NSA_HEADS, HEAD_DIM, s), F32),
            jax.ShapeDtypeStruct((b, NSA_GROUPS, nsel, s), F32),
        ],
        compiler_params=_params("arbitrary", "arbitrary", "arbitrary"),
        name="nsa_cmp_select",
    )(qt, kc, vct, inct, gates_t)


def _alibi_rows(slope, q0, rows):
    r = lax.broadcasted_iota(I32, (rows, TQ), 0)
    return jnp.where(r == 0, slope * POS_SPLIT, jnp.where(r == 1, slope, jnp.where(r == 2, -slope * q0, 0.0)))


def _wide_dist(width):
    lane = lax.broadcasted_iota(I32, (TK, width), 1) & (TQ - 1)
    return lane - lax.broadcasted_iota(I32, (TK, width), 0)


def _flash_init(s, pv, m_ref, l_ref, acc_ref):
    m = jnp.max(s, axis=0, keepdims=True)
    p = jnp.exp(s - m)
    m_ref[...] = m
    l_ref[...] = jnp.sum(p, axis=0, keepdims=True)
    acc_ref[...] = pv(p.astype(BF16))


def _flash_update(s, pv, m_ref, l_ref, acc_ref):
    m_old = m_ref[...]
    m_new = jnp.maximum(m_old, jnp.max(s, axis=0, keepdims=True))
    alpha = jnp.exp(m_old - m_new)
    p = jnp.exp(s - m_new)
    m_ref[...] = m_new
    l_ref[...] = alpha * l_ref[...] + jnp.sum(p, axis=0, keepdims=True)
    acc_ref[...] = alpha * acc_ref[...] + pv(p.astype(BF16))


def _pos_cols(s, width):
    pos = np.arange(s)
    cols = np.zeros((s, width), np.float32)
    cols[:, 0] = pos // POS_SPLIT
    cols[:, 1] = pos % POS_SPLIT
    cols[:, 2] = 1.0
    return cols


def _onehot_cols(s, block, width):
    cols = np.zeros((s, width), np.float32)
    cols[np.arange(s), np.arange(s) // block] = 1.0
    return cols


def _aug_key_tiles(z, extra, b, s, nh):
    e = extra.shape[1]
    zk = z.reshape(b, s, nh, HEAD_DIM)
    ex = jnp.broadcast_to(jnp.asarray(extra, BF16)[None, :, None, :], (b, s, nh, e))
    za = jnp.concatenate([zk, ex], axis=-1)
    return za.reshape(b, s // TK, TK, nh, HEAD_DIM + e).transpose(0, 3, 1, 2, 4)


def _nsa_selwin_kernel(q_ref, ks_ref, vst_ref, kw_ref, vwt_ref, sel_ref, gate_ref, accin_ref, o_ref,
                       qa_ref, ms_ref, ls_ref, as_ref, mw_ref, lw_ref, aw_ref):
    g = pl.program_id(1)
    i = pl.program_id(2)
    q0 = (i * TQ).astype(F32)
    width = NSA_HPG * TQ
    selbias = ((sel_ref[...] - 1.0) * BIG).astype(BF16)
    for h in range(NSA_HPG):
        slope = jnp.where(g == 0, _nsa_slope(0, h), _nsa_slope(1, h))
        lanes = slice(h * TQ, (h + 1) * TQ)
        qa_ref[0:HEAD_DIM, lanes] = q_ref[h]
        qa_ref[HEAD_DIM:2 * HEAD_DIM, lanes] = _alibi_rows(slope, q0, HEAD_DIM).astype(BF16)
        qa_ref[2 * HEAD_DIM:, lanes] = selbias
    qa = qa_ref[...]
    qw = qa_ref[0:2 * HEAD_DIM, :]
    dist = _wide_dist(width)

    def pv_sel(j):
        return lambda p: _dot(vst_ref[j], p)

    def pv_win(j):
        return lambda p: _dot(vwt_ref[j], p)

    causal = dist >= 0
    _flash_init(jnp.where(causal, _dot(ks_ref[i], qa), NEG), pv_sel(i), ms_ref, ls_ref, as_ref)
    _flash_init(jnp.where(causal, _dot(kw_ref[i], qw), NEG), pv_win(i), mw_ref, lw_ref, aw_ref)

    def past(j, carry):
        _flash_update(_dot(ks_ref[j], qa), pv_sel(j), ms_ref, ls_ref, as_ref)
        return carry

    lax.fori_loop(0, i, past, 0)

    for d in range(1, WINDOW // TK + 1):
        @pl.when(i >= d)
        def _():
            s = jnp.where(dist + d * TK < WINDOW, _dot(kw_ref[i - d], qw), NEG)
            _flash_update(s, pv_win(i - d), mw_ref, lw_ref, aw_ref)

    o_sel = as_ref[...] / ls_ref[...]
    o_win = aw_ref[...] / lw_ref[...]
    for h in range(NSA_HPG):
        lanes = slice(h * TQ, (h + 1) * TQ)
        row = g * (3 * NSA_HPG) + 3 * h
        g_sel = gate_ref[pl.ds(row + 1, 1), :]
        g_win = gate_ref[pl.ds(row + 2, 1), :]
        o_ref[h] = (accin_ref[h] + g_sel * o_sel[:, lanes] + g_win * o_win[:, lanes]).astype(o_ref.dtype)


def _nsa_selwin(qt, ks_t, vs_t, kw_t, vw_t, sel_t, gates_t, acc_t):
    b, _, _, s = qt.shape
    nt = s // TK
    width = NSA_HPG * TQ
    ksw = ks_t.shape[-1]
    kww = kw_t.shape[-1]
    vspec = pl.BlockSpec((None, None, nt, HEAD_DIM, TK), lambda i, g, j: (i, g, 0, 0, 0))
    qspec = pl.BlockSpec((None, NSA_HPG, HEAD_DIM, TQ), lambda i, g, j: (i, g, 0, j))
    stat = pltpu.VMEM((1, width), F32)
    accs = pltpu.VMEM((HEAD_DIM, width), F32)
    return pl.pallas_call(
        _nsa_selwin_kernel,
        grid=(b, NSA_GROUPS, s // TQ),
        in_specs=[
            qspec,
            pl.BlockSpec((None, None, nt, TK, ksw), lambda i, g, j: (i, g, 0, 0, 0)), vspec,
            pl.BlockSpec((None, None, nt, TK, kww), lambda i, g, j: (i, g, 0, 0, 0)), vspec,
            pl.BlockSpec((None, None, MAX_SEL_BLOCKS, TQ), lambda i, g, j: (i, g, 0, j)),
            pl.BlockSpec((None, LANES, TQ), lambda i, g, j: (i, 0, j)),
            qspec,
        ],
        out_specs=qspec,
        out_shape=jax.ShapeDtypeStruct((b, NSA_HEADS, HEAD_DIM, s), BF16),
        scratch_shapes=[pltpu.VMEM((ksw, width), BF16), stat, stat, accs, stat, stat, accs],
        compiler_params=_params("arbitrary", "arbitrary", "arbitrary"),
        name="nsa_select_window",
    )(qt, ks_t, vs_t, kw_t, vw_t, sel_t, gates_t, acc_t)


def _moba_kernel(q_ref, k_ref, vt_ref, kmean_ref, o_ref, qa_ref, m_ref, l_ref, acc_ref, *, n_top):
    hg = pl.program_id(1)
    i = pl.program_id(2)
    nblk = kmean_ref.shape[1]
    q0 = (i * TQ).astype(F32)
    blk = lax.broadcasted_iota(I32, (nblk, 1), 0)
    for h in range(MOBA_HB):
        head = hg * MOBA_HB + h
        slope = jnp.float32(2.0 ** -MOBA_HEADS)
        for hh in range(MOBA_HEADS - 1):
            slope = jnp.where(head == hh, 2.0 ** -(hh + 1), slope)
        q = q_ref[h]
        gate = jnp.where(blk < i, _dot(kmean_ref[h].astype(BF16), q), NEG)
        bias = jnp.full(gate.shape, -BIG, F32)
        for _ in range(n_top):
            mx = jnp.max(gate, axis=0, keepdims=True)
            first = jnp.min(jnp.where(gate == mx, blk, nblk), axis=0, keepdims=True)
            pick = blk == first
            bias = jnp.where(pick, 0.0, bias)
            gate = jnp.where(pick, -jnp.inf, gate)
        bias = jnp.where(blk < i, bias, 0.0)
        if nblk < MAX_MOBA_BLOCKS:
            bias = jnp.concatenate([bias, jnp.zeros((MAX_MOBA_BLOCKS - nblk, TQ), F32)], axis=0)
        qa_ref[h, 0:HEAD_DIM, :] = q
        qa_ref[h, HEAD_DIM:HEAD_DIM + MAX_MOBA_BLOCKS, :] = _alibi_rows(slope, q0, MAX_MOBA_BLOCKS).astype(BF16)
        qa_ref[h, HEAD_DIM + MAX_MOBA_BLOCKS:, :] = bias.astype(BF16)

    def scores(j):
        return jnp.concatenate([_dot(k_ref[h, j], qa_ref[h]) for h in range(MOBA_HB)], axis=1)

    def pv(j):
        return lambda p: jnp.concatenate(
            [_dot(vt_ref[h, j], p[:, h * TQ:(h + 1) * TQ]) for h in range(MOBA_HB)], axis=1)

    s = jnp.where(_wide_dist(MOBA_HB * TQ) >= 0, scores(i), NEG)
    _flash_init(s, pv(i), m_ref, l_ref, acc_ref)

    def past(j, carry):
        _flash_update(scores(j), pv(j), m_ref, l_ref, acc_ref)
        return carry

    lax.fori_loop(0, i, past, 0)
    o = acc_ref[...] / l_ref[...]
    for h in range(MOBA_HB):
        o_ref[h] = o[:, h * TQ:(h + 1) * TQ].astype(o_ref.dtype)


def _moba(qt, k_t, v_t, kmean):
    b, nh, _, s = qt.shape
    nt = s // TK
    n_top = min(MOBA_TOP, nt - 1)
    hb = MOBA_HB
    kw = k_t.shape[-1]
    width = hb * TQ
    return pl.pallas_call(
        functools.partial(_moba_kernel, n_top=n_top),
        grid=(b, nh // hb, s // TQ),
        in_specs=[
            pl.BlockSpec((None, hb, HEAD_DIM, TQ), lambda i, h, j: (i, h, 0, j)),
            pl.BlockSpec((None, hb, nt, TK, kw), lambda i, h, j: (i, h, 0, 0, 0)),
            pl.BlockSpec((None, hb, nt, HEAD_DIM, TK), lambda i, h, j: (i, h, 0, 0, 0)),
            pl.BlockSpec((None, hb, nt, HEAD_DIM), lambda i, h, j: (i, h, 0, 0)),
        ],
        out_specs=pl.BlockSpec((None, hb, HEAD_DIM, TQ), lambda i, h, j: (i, h, 0, j)),
        out_shape=jax.ShapeDtypeStruct((b, nh, HEAD_DIM, s), BF16),
        scratch_shapes=[
            pltpu.VMEM((hb, kw, TQ), BF16), pltpu.VMEM((1, width), F32), pltpu.VMEM((1, width), F32),
            pltpu.VMEM((HEAD_DIM, width), F32),
        ],
        compiler_params=_params("arbitrary", "arbitrary", "arbitrary"),
        name="moba",
    )(qt, k_t, v_t, kmean)


def _outproj_kernel(on_ref, om_ref, w_ref, x_ref, g1_ref, ln_ref, sc_ref, sh_ref,
                    wrh_ref, wrl_ref, rb_ref, x1_ref, h2_ref, lg_ref):
    half = on_ref.shape[1]
    y = _dot(on_ref[...], w_ref[:half, :]) + _dot(om_ref[...], w_ref[half:, :])
    x1 = x_ref[...] + g1_ref[...] * y
    x1_ref[...] = x1
    h2 = _modulated_norm(x1, ln_ref[...], sc_ref[...], sh_ref[...])
    h2_ref[...] = h2
    hh, hl = _hi_lo(h2)
    lg_ref[...] = _dot(hh, wrh_ref[...]) + _dot(hh, wrl_ref[...]) + _dot(hl, wrh_ref[...]) + rb_ref[...]


def _outproj(o_nsa, o_moba, w_out, x2d, g1, ln_g, sc2p, sh2, wr_hi, wr_lo, rb, seq):
    t, d = x2d.shape
    tm = TM_PROJ
    nb = seq // tm
    row = lambda i: (i, 0)
    bat = lambda i: (i // nb, 0, 0)
    const = lambda i: (0, 0)
    half = o_nsa.shape[1]
    return pl.pallas_call(
        _outproj_kernel,
        grid=(t // tm,),
        in_specs=[
            pl.BlockSpec((tm, half), row), pl.BlockSpec((tm, half), row),
            pl.BlockSpec(w_out.shape, const),
            pl.BlockSpec((tm, d), row),
            pl.BlockSpec((None, 1, d), bat),
            pl.BlockSpec((1, d), const),
            pl.BlockSpec((None, 1, d), bat), pl.BlockSpec((None, 1, d), bat),
            pl.BlockSpec(wr_hi.shape, const), pl.BlockSpec(wr_lo.shape, const),
            pl.BlockSpec((1, LANES), const),
        ],
        out_specs=[pl.BlockSpec((tm, d), row), pl.BlockSpec((tm, d), row), pl.BlockSpec((tm, LANES), row)],
        out_shape=[jax.ShapeDtypeStruct((t, d), F32), jax.ShapeDtypeStruct((t, d), F32),
                   jax.ShapeDtypeStruct((t, LANES), F32)],
        compiler_params=_params("arbitrary"),
        name="outproj_router",
    )(o_nsa, o_moba, w_out, x2d, g1, ln_g, sc2p, sh2, wr_hi, wr_lo, rb)


def _route_kernel(lg_ref, idx_ref, gate_ref, rank_ref, cnt_ref, carry_ref):
    @pl.when(pl.program_id(0) == 0)
    def _():
        carry_ref[...] = jnp.zeros(carry_ref.shape, carry_ref.dtype)

    lg = lg_ref[...]
    ne, tt = lg.shape
    eidx = lax.broadcasted_iota(I32, lg.shape, 0)
    vals, idxs, hots = [], [], []
    for _ in range(TOP_K):
        mx = jnp.max(lg, axis=0, keepdims=True)
        first = jnp.min(jnp.where(lg == mx, eidx, ne), axis=0, keepdims=True)
        hot = eidx == first
        vals.append(mx)
        idxs.append(first)
        hots.append(hot)
        lg = jnp.where(hot, -jnp.inf, lg)
    e = [jnp.exp(v - vals[0]) for v in vals]
    tot = e[0]
    for x in e[1:]:
        tot = tot + x
    cnt = jnp.where(hots[0], 1.0, 0.0)
    for hot in hots[1:]:
        cnt = cnt + jnp.where(hot, 1.0, 0.0)
    cntb = cnt.astype(BF16)
    r = lax.broadcasted_iota(I32, (tt, tt), 0)
    c = lax.broadcasted_iota(I32, (tt, tt), 1)
    before = jnp.where(r < c, 1.0, 0.0).astype(BF16)
    carry = carry_ref[...]
    prefix = _dot(cntb, before) + carry
    ranks = [jnp.sum(jnp.where(hot, prefix, 0.0), axis=0, keepdims=True) for hot in hots]
    carry = carry + _dot(cntb, jnp.ones((tt, tt), BF16))
    carry_ref[...] = carry
    cnt_ref[...] = carry[:, :LANES]
    pad_i = jnp.zeros((SUBLANES - TOP_K, tt), I32)
    pad_f = jnp.zeros((SUBLANES - TOP_K, tt), F32)
    idx_ref[...] = jnp.concatenate(idxs + [pad_i], axis=0)
    gate_ref[...] = jnp.concatenate([x / tot for x in e] + [pad_f], axis=0)
    rank_ref[...] = jnp.concatenate([x.astype(I32) for x in ranks] + [pad_i], axis=0)


def _route(logits_t):
    ne, t = logits_t.shape
    tt = min(TT_ROUTE, t)
    tok = pl.BlockSpec((SUBLANES, tt), lambda i: (0, i))
    return pl.pallas_call(
        _route_kernel,
        grid=(t // tt,),
        in_specs=[pl.BlockSpec((ne, tt), lambda i: (0, i))],
        out_specs=[tok, tok, tok, pl.BlockSpec((ne, LANES), lambda i: (0, 0))],
        out_shape=[jax.ShapeDtypeStruct((SUBLANES, t), I32), jax.ShapeDtypeStruct((SUBLANES, t), F32),
                   jax.ShapeDtypeStruct((SUBLANES, t), I32), jax.ShapeDtypeStruct((ne, LANES), F32)],
        scratch_shapes=[pltpu.VMEM((ne, tt), F32)],
        compiler_params=_params("arbitrary"),
        name="route_topk_rank",
    )(logits_t)


def _dispatch_plan(idx, rank, counts):
    tm = TM_MOE
    t = idx.shape[1]
    nb = (t * TOP_K + N_EXPERTS * tm) // tm
    experts = jnp.arange(N_EXPERTS, dtype=I32)
    counts = counts.astype(I32)
    padded = (counts + tm - 1) // tm * tm
    pend = jnp.sum(jnp.where(experts[None, :] <= experts[:, None], padded[None, :], 0), axis=1)
    pstart = pend - padded
    base = jnp.sum(jnp.where(idx[:, :, None] == experts, pstart, 0), axis=-1)
    dest = (rank + base).T.reshape(t * TOP_K).astype(I32)
    blk_e = jnp.sum(jnp.where(pend[None, :] <= (jnp.arange(nb, dtype=I32) * tm)[:, None], 1, 0), axis=1)
    blk_e = jnp.minimum(blk_e, N_EXPERTS - 1).astype(I32)
    nused = (pend[-1] // tm).astype(I32).reshape(1)
    fill_start = jnp.maximum(pend - tm, 0).astype(I32)
    fill_flag = (padded > counts).astype(I32)
    return dest, blk_e, nused, fill_start, fill_flag


def _dispatch_kernel(fill_start_ref, fill_flag_ref, nused_ref, dest_ref, h_ref, xs_hbm, zbuf, zsem, sem):
    i = pl.program_id(0)
    nrow = dest_ref.shape[1]
    ntok = nrow // TOP_K

    @pl.when(i == 0)
    def _():
        zbuf[...] = jnp.zeros(zbuf.shape, zbuf.dtype)

        def fill(start):
            return pltpu.make_async_copy(zbuf, xs_hbm.at[pl.ds(pl.multiple_of(start, TM_MOE), TM_MOE)], zsem)

        for e in range(N_EXPERTS):
            @pl.when(fill_flag_ref[e] != 0)
            def _():
                fill(fill_start_ref[e]).start()
        for e in range(N_EXPERTS):
            @pl.when(fill_flag_ref[e] != 0)
            def _():
                fill(fill_start_ref[e]).wait()

        nblk = xs_hbm.shape[0] // TM_MOE

        def tail_start(b, c):
            fill(b * TM_MOE).start()
            return c

        def tail_wait(b, c):
            fill(b * TM_MOE).wait()
            return c

        lax.fori_loop(nused_ref[0], nblk, tail_start, 0)
        lax.fori_loop(nused_ref[0], nblk, tail_wait, 0)

    def issue(tl, c):
        src = h_ref.at[pl.ds(tl, 1)]
        for k in range(TOP_K):
            pltpu.make_async_copy(src, xs_hbm.at[pl.ds(dest_ref[0, tl * TOP_K + k], 1)], sem).start()
        return c

    lax.fori_loop(0, ntok, issue, 0)
    for _ in range(TOP_K):
        pltpu.make_async_copy(h_ref, xs_hbm.at[pl.ds(0, ntok)], sem).wait()


def _dispatch(fill_start, fill_flag, nused, dest, h2):
    t, d = h2.shape
    td = min(TD_DISPATCH, t)
    r = t * TOP_K + N_EXPERTS * TM_MOE
    grid_spec = pltpu.PrefetchScalarGridSpec(
        num_scalar_prefetch=3,
        grid=(t // td,),
        in_specs=[
            pl.BlockSpec((None, 1, td * TOP_K), lambda i, fs, ff, nu: (i, 0, 0), memory_space=pltpu.SMEM),
            pl.BlockSpec((td, d), lambda i, fs, ff, nu: (i, 0)),
        ],
        out_specs=pl.BlockSpec(memory_space=pl.ANY),
        scratch_shapes=[pltpu.VMEM((TM_MOE, d), F32), pltpu.SemaphoreType.DMA(()), pltpu.SemaphoreType.DMA(())],
    )
    return pl.pallas_call(
        _dispatch_kernel,
        grid_spec=grid_spec,
        out_shape=jax.ShapeDtypeStruct((r, d), F32),
        compiler_params=_params("arbitrary"),
        name="moe_dispatch",
    )(fill_start, fill_flag, nused, dest.reshape(t // td, 1, td * TOP_K), h2)


def _moe_ffn_kernel(blk_e_ref, nused_ref, x_ref, w1_ref, b1_ref, w2_ref, b2_ref, y_ref, w1b, w2b):
    i = pl.program_id(0)
    dff = w2_ref.shape[0]

    @pl.when(i < nused_ref[0])
    def _():
        prev = blk_e_ref[jnp.maximum(i - 1, 0)]

        @pl.when((i == 0) | (blk_e_ref[i] != prev))
        def _():
            w1b[...] = w1_ref[...].astype(BF16)
            w2b[...] = w2_ref[...].astype(BF16)

        u = _dot(x_ref[...].astype(BF16), w1b[...]) + b1_ref[...]
        gl = jnp.minimum(u[:, :dff], SWIGLU_LIMIT)
        lin = jnp.clip(u[:, dff:], -SWIGLU_LIMIT, SWIGLU_LIMIT)
        act = (lin + 1.0) * gl * _sigmoid(SWIGLU_ALPHA * gl)
        y_ref[...] = _dot(act.astype(BF16), w2b[...]) + b2_ref[...]

    @pl.when(i >= nused_ref[0])
    def _():
        y_ref[...] = jnp.zeros(y_ref.shape, y_ref.dtype)


def _moe_ffn(blk_e, nused, xs, w1, b1, w2, b2):
    r, d = xs.shape
    ne, _, f2 = w1.shape
    dff = f2 // 2
    tm = TM_MOE
    rows = lambda i, be, nu: (jnp.minimum(i, nu[0] - 1), 0)
    grid_spec = pltpu.PrefetchScalarGridSpec(
        num_scalar_prefetch=2,
        grid=(r // tm,),
        in_specs=[
            pl.BlockSpec((tm, d), rows),
            pl.BlockSpec((None, d, f2), lambda i, be, nu: (be[i], 0, 0)),
            pl.BlockSpec((None, 1, f2), lambda i, be, nu: (be[i], 0, 0)),
            pl.BlockSpec((None, dff, d), lambda i, be, nu: (be[i], 0, 0)),
            pl.BlockSpec((None, 1, d), lambda i, be, nu: (be[i], 0, 0)),
        ],
        out_specs=pl.BlockSpec((tm, d), lambda i, be, nu: (i, 0)),
        scratch_shapes=[pltpu.VMEM((d, f2), BF16), pltpu.VMEM((dff, d), BF16)],
    )
    return pl.pallas_call(
        _moe_ffn_kernel,
        grid_spec=grid_spec,
        out_shape=jax.ShapeDtypeStruct((r, d), F32),
        compiler_params=_params("arbitrary"),
        name="moe_ffn",
    )(blk_e, nused, xs, w1, b1.reshape(ne, 1, f2), w2, b2.reshape(ne, 1, d))


def _moe_combine_kernel(pos_ref, nxt_ref, y_hbm, gate_ref, x1_ref, g2_ref, o_ref, buf, sems):
    i = pl.program_id(0)
    n = pl.num_programs(0)
    tc = x1_ref.shape[0]
    slot = i % 2

    def fetch(p_ref, s):
        def issue(tl, c):
            for k in range(TOP_K):
                pltpu.make_async_copy(y_hbm.at[pl.ds(p_ref[0, tl * TOP_K + k], 1)],
                                      buf.at[s, k, pl.ds(tl, 1)], sems.at[s]).start()
            return c
        lax.fori_loop(0, tc, issue, 0)

    @pl.when(i == 0)
    def _():
        fetch(pos_ref, 0)

    @pl.when(i + 1 < n)
    def _():
        fetch(nxt_ref, 1 - slot)

    for k in range(TOP_K):
        pltpu.make_async_copy(y_hbm.at[pl.ds(0, tc)], buf.at[slot, k], sems.at[slot]).wait()
    gate = gate_ref[...]
    moe = gate[:, 0:1] * buf[slot, 0]
    for k in range(1, TOP_K):
        moe = moe + gate[:, k:k + 1] * buf[slot, k]
    o_ref[...] = x1_ref[...] + g2_ref[...] * moe


def _moe_combine(pos, y, gate, x1, g2, seq):
    t, d = x1.shape
    tc = TC_COMB
    nb = seq // tc
    n = t // tc
    pos3 = pos.reshape(n, 1, tc * TOP_K)
    return pl.pallas_call(
        _moe_combine_kernel,
        grid=(n,),
        in_specs=[
            pl.BlockSpec((None, 1, tc * TOP_K), lambda i: (i, 0, 0), memory_space=pltpu.SMEM),
            pl.BlockSpec((None, 1, tc * TOP_K), lambda i: (jnp.minimum(i + 1, n - 1), 0, 0),
                         memory_space=pltpu.SMEM),
            pl.BlockSpec(memory_space=pl.ANY),
            pl.BlockSpec((tc, TOP_K), lambda i: (i, 0)),
            pl.BlockSpec((tc, d), lambda i: (i, 0)),
            pl.BlockSpec((None, 1, d), lambda i: (i // nb, 0, 0)),
        ],
        out_specs=pl.BlockSpec((tc, d), lambda i: (i, 0)),
        out_shape=jax.ShapeDtypeStruct((t, d), F32),
        scratch_shapes=[pltpu.VMEM((2, TOP_K, tc, d), F32), pltpu.SemaphoreType.DMA((2,))],
        compiler_params=_params("arbitrary"),
        name="moe_combine",
    )(pos3, pos3, y, gate, x1, g2)


def _tile_gain(gain, width, scale=1.0):
    return (jnp.tile(gain, width // HEAD_DIM) * scale).reshape(1, width).astype(F32)


def _heads_t(z, b, s, nh):
    return z.reshape(b, s, nh, HEAD_DIM).transpose(0, 2, 3, 1)


def _key_tiles(z, b, s, nh):
    return z.reshape(b, s // TK, TK, nh, HEAD_DIM).transpose(0, 3, 1, 2, 4)


def _value_tiles_t(z, b, s, nh):
    return z.reshape(b, s // TK, TK, nh, HEAD_DIM).transpose(0, 3, 1, 4, 2)


def _layer(x2d, mod, b, s, ln1_g, ln2_g, w_in, nsa_q_gain, nsa_k_gain, cmp_pos, cmp_w1, cmp_w2,
           moba_q_gain, moba_k_gain, w_out, router_w, router_b, exp_w1, exp_b1, exp_w2, exp_b2):
    t, d = x2d.shape
    sh1, sc1, g1, sh2, sc2, g2 = [m.reshape(b, 1, d) for m in jnp.split(mod, 6, axis=-1)]
    scale = HEAD_DIM ** -0.5

    (q, kcv, ks, vs, kw, vw, gates, mq, mk, mv, kmean) = _inproj(
        x2d, ln1_g.reshape(1, d), 1.0 + sc1, sh1, _pack_w_in(w_in),
        _tile_gain(nsa_q_gain, 512, scale), _tile_gain(nsa_k_gain, 128),
        _tile_gain(moba_q_gain, 512, scale), _tile_gain(moba_k_gain, 512), s)

    g = NSA_GROUPS
    xc = kcv.reshape(b, s, 2, g, HEAD_DIM).transpose(0, 2, 3, 1, 4).reshape(
        b, 2, g, s // CMP_STRIDE, CMP_STRIDE * HEAD_DIM)
    cmp = _compress(xc, cmp_pos.reshape(2, 1, CMP_LEN * HEAD_DIM), cmp_w1.astype(BF16),
                    cmp_w2.astype(BF16), nsa_k_gain.reshape(1, HEAD_DIM))
    kc = cmp[:, 0].astype(BF16)
    vct = cmp[:, 1].transpose(0, 1, 3, 2).astype(BF16)
    qt = _heads_t(q, b, s, NSA_HEADS)
    gates_t = gates.reshape(b, s, LANES).transpose(0, 2, 1)
    acc_t, sel_t = _nsa_cmp(qt, kc, vct, gates_t)
    nsel = sel_t.shape[2]
    if nsel < MAX_SEL_BLOCKS:
        sel_t = jnp.concatenate([sel_t, jnp.ones((b, g, MAX_SEL_BLOCKS - nsel, s), F32)], axis=2)
    pos_cols = _pos_cols(s, HEAD_DIM)
    sel_cols = np.concatenate([pos_cols, _onehot_cols(s, SEL_BLOCK, MAX_SEL_BLOCKS)], axis=1)
    o_nsa_t = _nsa_selwin(qt, _aug_key_tiles(ks, sel_cols, b, s, g), _value_tiles_t(vs, b, s, g),
                          _aug_key_tiles(kw, pos_cols, b, s, g), _value_tiles_t(vw, b, s, g),
                          sel_t, gates_t, acc_t)
    o_nsa = o_nsa_t.transpose(0, 3, 1, 2).reshape(t, NSA_HEADS * HEAD_DIM)

    km = kmean.reshape(b, s // MOBA_BLOCK, MOBA_HEADS, HEAD_DIM).transpose(0, 2, 1, 3)
    moba_cols = np.concatenate([_pos_cols(s, MAX_MOBA_BLOCKS),
                                _onehot_cols(s, MOBA_BLOCK, MAX_MOBA_BLOCKS)], axis=1)
    o_moba_t = _moba(_heads_t(mq, b, s, MOBA_HEADS), _aug_key_tiles(mk, moba_cols, b, s, MOBA_HEADS),
                     _value_tiles_t(mv, b, s, MOBA_HEADS), km)
    o_moba = o_moba_t.transpose(0, 3, 1, 2).reshape(t, MOBA_HEADS * HEAD_DIM)

    wr = jnp.zeros((d, LANES), F32).at[:, :N_EXPERTS].set(router_w)
    wr_hi = wr.astype(BF16)
    wr_lo = (wr - wr_hi.astype(F32)).astype(BF16)
    rb = jnp.zeros((1, LANES), F32).at[0, :N_EXPERTS].set(router_b)
    x1, h2, logits = _outproj(o_nsa, o_moba, w_out.astype(BF16), x2d, g1, ln2_g.reshape(1, d),
                              1.0 + sc2, sh2, wr_hi, wr_lo, rb, s)

    idx8, gate8, rank8, counts = _route(logits[:, :N_EXPERTS].T)
    dest, blk_e, nused, fill_start, fill_flag = _dispatch_plan(idx8[:TOP_K], rank8[:TOP_K], counts[:, 0])
    xs = _dispatch(fill_start, fill_flag, nused, dest, h2)
    y = _moe_ffn(blk_e, nused, xs, exp_w1, exp_b1, exp_w2, exp_b2)
    return _moe_combine(dest, y, gate8[:TOP_K].T, x1, g2, s)


def kernel(x, c, ada_w, ada_b, ln1_g, ln2_g, w_in, nsa_q_gain, nsa_k_gain, nsa_cmp_pos, nsa_cmp_w1,
           nsa_cmp_w2, moba_q_gain, moba_k_gain, w_out, router_w, router_b, exp_w1, exp_b1, exp_w2, exp_b2):
    b, s, d = x.shape
    depth = ada_w.shape[0]
    mods = _adaln(c, ada_w, ada_b)
    x2d = x.reshape(b * s, d)
    for l in range(depth):
        x2d = _layer(x2d, mods[l], b, s, ln1_g[l], ln2_g[l], w_in[l], nsa_q_gain[l], nsa_k_gain[l],
                     nsa_cmp_pos[l], nsa_cmp_w1[l], nsa_cmp_w2[l], moba_q_gain[l], moba_k_gain[l],
                     w_out[l], router_w[l], router_b[l], exp_w1[l], exp_b1[l], exp_w2[l], exp_b2[l])
    return x2d.reshape(b, s, d)
```

```python
import functools

import numpy as np
import jax
import jax.numpy as jnp
from jax import lax
from jax.experimental import pallas as pl
from jax.experimental.pallas import tpu as pltpu

F32 = jnp.float32
BF16 = jnp.bfloat16
I32 = jnp.int32

HEAD_DIM = 64
NSA_HEADS = 8
NSA_GROUPS = 2
NSA_HPG = NSA_HEADS // NSA_GROUPS
CMP_LEN = 32
CMP_STRIDE = 16
CMP_HID = 128
SEL_BLOCK = 64
SEL_TOP = 16
WINDOW = 512
MOBA_HEADS = 8
MOBA_BLOCK = 256
MOBA_TOP = 3
N_EXPERTS = 32
TOP_K = 4
SWIGLU_LIMIT = 7.0
SWIGLU_ALPHA = 1.702
EPS = 1e-6
NEG = -1e30
FORCE_SCORE = 1e4

LANES = 128
SUBLANES = 8
TQ = 256
TK = 256
POS_SPLIT = 256
BIG = 1e30
MAX_SEL_BLOCKS = 128
MAX_MOBA_BLOCKS = 32
MOBA_HB = 4
RANK_CHUNK = 32
TM_PROJ = 512
TM_MOE = 256
TC_COMB = 128
TT_ROUTE = 512
TD_DISPATCH = 512
VMEM_LIMIT = 56 * 1024 * 1024


def _dot(a, b):
    return jnp.dot(a, b, preferred_element_type=F32)


def _hi_lo(x):
    hi = x.astype(BF16)
    lo = (x - hi.astype(F32)).astype(BF16)
    return hi, lo


def _sigmoid(x):
    return 1.0 / (1.0 + jnp.exp(-x))


def _params(*sem):
    return pltpu.CompilerParams(dimension_semantics=sem, vmem_limit_bytes=VMEM_LIMIT)


def _adaln_kernel(c_ref, w_ref, b_ref, o_ref):
    c = c_ref[...]
    cond = c * _sigmoid(c)
    ch, cl = _hi_lo(cond)
    wh, wl = _hi_lo(w_ref[...])
    o_ref[...] = _dot(ch, wh) + _dot(ch, wl) + _dot(cl, wh) + b_ref[...]


def _adaln(c, ada_w, ada_b):
    depth, d, n6 = ada_w.shape
    b = c.shape[0]
    tn = 1024
    c_pad = jnp.zeros((SUBLANES, d), F32).at[:b].set(c)
    out = pl.pallas_call(
        _adaln_kernel,
        grid=(depth, n6 // tn),
        in_specs=[
            pl.BlockSpec((SUBLANES, d), lambda l, j: (0, 0)),
            pl.BlockSpec((None, d, tn), lambda l, j: (l, 0, j)),
            pl.BlockSpec((None, 1, tn), lambda l, j: (l, 0, j)),
        ],
        out_specs=pl.BlockSpec((None, SUBLANES, tn), lambda l, j: (l, 0, j)),
        out_shape=jax.ShapeDtypeStruct((depth, SUBLANES, n6), F32),
        compiler_params=_params("arbitrary", "arbitrary"),
        name="adaln",
    )(c_pad, ada_w, ada_b.reshape(depth, 1, n6))
    return out[:, :b]


def _block_diag_ones():
    r = lax.broadcasted_iota(I32, (LANES, LANES), 0) // HEAD_DIM
    c = lax.broadcasted_iota(I32, (LANES, LANES), 1) // HEAD_DIM
    return jnp.where(r == c, 1.0, 0.0).astype(BF16)


def _head_norm(z, bd, gain):
    cols = []
    for c0 in range(0, z.shape[1], LANES):
        zc = z[:, c0:c0 + LANES]
        hi, lo = _hi_lo(zc * zc)
        ss = _dot(hi, bd) + _dot(lo, bd)
        cols.append(zc * lax.rsqrt(ss * (1.0 / HEAD_DIM) + EPS))
    zn = cols[0] if len(cols) == 1 else jnp.concatenate(cols, axis=1)
    return zn * gain


def _modulated_norm(x, g, sc1p, sh):
    ms = jnp.mean(x * x, axis=-1, keepdims=True)
    y = x * lax.rsqrt(ms + EPS)
    return (y * g) * sc1p + sh


def _inproj_kernel(x_ref, g_ref, sc_ref, sh_ref, w_ref, gq_ref, gk_ref, gmq_ref, gmk_ref,
                   q_ref, kcv_ref, ks_ref, vs_ref, kw_ref, vw_ref, gate_ref,
                   mq_ref, mk_ref, mv_ref, kmean_ref):
    h = _modulated_norm(x_ref[...], g_ref[...], sc_ref[...], sh_ref[...])
    hb = h.astype(BF16)
    bd = _block_diag_ones()

    def proj(a, b):
        return _dot(hb, w_ref[:, a:b])

    q_ref[...] = _head_norm(proj(0, 512), bd, gq_ref[...]).astype(BF16)
    kv = proj(512, 1280)
    kcv_ref[...] = kv[:, 0:256]
    ks_ref[...] = _head_norm(kv[:, 256:384], bd, gk_ref[...]).astype(BF16)
    vs_ref[...] = kv[:, 384:512].astype(BF16)
    kw_ref[...] = _head_norm(kv[:, 512:640], bd, gk_ref[...]).astype(BF16)
    vw_ref[...] = kv[:, 640:768].astype(BF16)
    gate_ref[...] = _sigmoid(proj(1280, 1408))
    mq_ref[...] = _head_norm(proj(1408, 1920), bd, gmq_ref[...]).astype(BF16)
    mkn = _head_norm(proj(1920, 2432), bd, gmk_ref[...])
    mk_ref[...] = mkn.astype(BF16)
    mv_ref[...] = proj(2432, 2944).astype(BF16)
    means = [jnp.sum(mkn[r:r + MOBA_BLOCK], axis=0, keepdims=True) * (1.0 / MOBA_BLOCK)
             for r in range(0, mkn.shape[0], MOBA_BLOCK)]
    kmean_ref[...] = jnp.concatenate(means, axis=0)


def _pack_w_in(w_in):
    d = w_in.shape[0]
    wg = jnp.zeros((d, LANES), w_in.dtype).at[:, :3 * NSA_HEADS].set(w_in[:, 1280:1304])
    return jnp.concatenate([w_in[:, :1280], wg, w_in[:, 1304:]], axis=1).astype(BF16)


def _inproj(x2d, ln_g, sc1p, sh, w_pack, gq, gk, gmq, gmk, seq):
    t, d = x2d.shape
    tm = TM_PROJ
    nb = seq // tm
    row = lambda i: (i, 0)
    bat = lambda i: (i // nb, 0, 0)
    const = lambda i: (0, 0)
    outs = [
        (512, BF16), (256, F32), (128, BF16), (128, BF16), (128, BF16), (128, BF16),
        (128, F32), (512, BF16), (512, BF16), (512, BF16),
    ]
    out_shape = [jax.ShapeDtypeStruct((t, w), dt) for w, dt in outs]
    out_specs = [pl.BlockSpec((tm, w), row) for w, _ in outs]
    nmean = tm // MOBA_BLOCK
    out_shape.append(jax.ShapeDtypeStruct((t // tm, nmean, 512), F32))
    out_specs.append(pl.BlockSpec((None, nmean, 512), lambda i: (i, 0, 0)))
    return pl.pallas_call(
        _inproj_kernel,
        grid=(t // tm,),
        in_specs=[
            pl.BlockSpec((tm, d), row),
            pl.BlockSpec((1, d), const),
            pl.BlockSpec((None, 1, d), bat),
            pl.BlockSpec((None, 1, d), bat),
            pl.BlockSpec(w_pack.shape, const),
            pl.BlockSpec((1, 512), const),
            pl.BlockSpec((1, 128), const),
            pl.BlockSpec((1, 512), const),
            pl.BlockSpec((1, 512), const),
        ],
        out_specs=out_specs,
        out_shape=out_shape,
        compiler_params=_params("arbitrary"),
        name="inproj",
    )(x2d, ln_g, sc1p, sh, w_pack, gq, gk, gmq, gmk)


def _compress_kernel(x_ref, pos_ref, w1_ref, w2_ref, gk_ref, o_ref):
    kv = pl.program_id(1)
    half = CMP_STRIDE * HEAD_DIM
    x = x_ref[...]
    xl = (x + pos_ref[:, :half]).astype(BF16)
    xh = (x + pos_ref[:, half:]).astype(BF16)
    a = _dot(xl, w1_ref[:half, :])
    b = _dot(xh, w1_ref[half:, :])
    hid = a + pltpu.roll(b, b.shape[0] - 1, 0)
    out = _dot(jax.nn.gelu(hid, approximate=True).astype(BF16), w2_ref[...])
    ms = jnp.mean(out * out, axis=-1, keepdims=True)
    normed = (out * lax.rsqrt(ms + EPS)) * gk_ref[...]
    o_ref[...] = jnp.where(kv == 0, normed, out)


def _compress(xc, pos, w1, w2, gk):
    b, _, g, nr, w = xc.shape
    return pl.pallas_call(
        _compress_kernel,
        grid=(b, 2, g),
        in_specs=[
            pl.BlockSpec((None, None, None, nr, w), lambda i, k, j: (i, k, j, 0, 0)),
            pl.BlockSpec((None, 1, w * 2), lambda i, k, j: (k, 0, 0)),
            pl.BlockSpec((None, w * 2, CMP_HID), lambda i, k, j: (k, 0, 0)),
            pl.BlockSpec((None, CMP_HID, HEAD_DIM), lambda i, k, j: (k, 0, 0)),
            pl.BlockSpec((1, HEAD_DIM), lambda i, k, j: (0, 0)),
        ],
        out_specs=pl.BlockSpec((None, None, None, nr, HEAD_DIM), lambda i, k, j: (i, k, j, 0, 0)),
        out_shape=jax.ShapeDtypeStruct((b, 2, g, nr, HEAD_DIM), F32),
        compiler_params=_params("arbitrary", "arbitrary", "arbitrary"),
        name="nsa_compress",
    )(xc, pos, w1, w2, gk)


def _nsa_slope(g, h):
    return 2.0 ** (-(g * NSA_HPG + h + 1))


def _rank_counts(imp_ref, cnt_ref, last_blk):
    nb = imp_ref.shape[0]
    chunk = min(RANK_CHUNK, nb)
    cnt_ref[...] = jnp.zeros(cnt_ref.shape, cnt_ref.dtype)
    sub = lax.broadcasted_iota(I32, (SUBLANES, imp_ref.shape[1]), 0)
    for c in range(nb // chunk):
        for d in range(nb // chunk):
            @pl.when(max(c, d) * chunk <= last_blk)
            def _():
                rows0 = range(d * chunk, (d + 1) * chunk, SUBLANES)
                pieces = [imp_ref[r:r + SUBLANES] for r in rows0]
                counts = [jnp.zeros(p.shape, I32) for p in pieces]
                for jp in range(c * chunk, (c + 1) * chunk):
                    row = imp_ref[jp:jp + 1]
                    for k, r0 in enumerate(rows0):
                        p = pieces[k]
                        if r0 + SUBLANES - 1 < jp:
                            inc = jnp.where(row > p, 1, 0)
                        elif r0 > jp:
                            inc = jnp.where(row >= p, 1, 0)
                        else:
                            inc = jnp.where(sub > jp - r0, jnp.where(row >= p, 1, 0), jnp.where(row > p, 1, 0))
                        counts[k] = counts[k] + inc
                rows = slice(d * chunk, (d + 1) * chunk)
                cnt_ref[rows] = cnt_ref[rows] + jnp.concatenate(counts, axis=0)


def _nsa_cmp_kernel(q_ref, kc_ref, vct_ref, inct_ref, gate_ref, acc_ref, sel_ref, imp_ref, cnt_ref, *, n_top):
    g = pl.program_id(1)
    q0 = pl.program_id(2) * TQ
    nc = kc_ref.shape[0]
    nsel = inct_ref.shape[0]
    t = q0 + lax.broadcasted_iota(I32, (1, TQ), 1)
    n = lax.broadcasted_iota(I32, (nc, 1), 0)
    center = (CMP_STRIDE * n).astype(F32) + 0.5 * (CMP_LEN - 1)
    dist = t.astype(F32) - center
    mask = (CMP_STRIDE * n + (CMP_LEN - 1)) <= t
    kc = kc_ref[...]
    vct = vct_ref[...]
    psum = jnp.zeros((nc, TQ), F32)
    for h in range(NSA_HPG):
        slope = jnp.where(g == 0, _nsa_slope(0, h), _nsa_slope(1, h))
        s = _dot(kc, q_ref[h]) - slope * dist
        s = jnp.where(mask, s, NEG)
        m = jnp.max(s, axis=0, keepdims=True)
        e = jnp.exp(s - m)
        l = jnp.sum(e, axis=0, keepdims=True)
        p = jnp.where(mask, e, 0.0) * (1.0 / l)
        o = _dot(vct, p.astype(BF16))
        gate = gate_ref[pl.ds(g * (3 * NSA_HPG) + 3 * h, 1), :]
        acc_ref[h] = gate * o
        psum = psum + p
    hi, lo = _hi_lo(psum)
    inct = inct_ref[...]
    imp = _dot(inct, hi) + _dot(inct, lo)
    blk = lax.broadcasted_iota(I32, (nsel, 1), 0)
    cur = t // SEL_BLOCK
    forced = (blk == 0) | (blk == cur) | (blk == cur - 1)
    imp = jnp.where(forced, FORCE_SCORE, imp)
    imp_ref[...] = jnp.where(blk <= cur, imp, NEG)
    _rank_counts(imp_ref, cnt_ref, (q0 + TQ - 1) // SEL_BLOCK)
    chosen = (cnt_ref[...] < n_top) & (blk <= cur)
    sel_ref[...] = jnp.where(chosen, 1.0, 0.0)


def _incidence_t(nc, nsel):
    cmp_start = CMP_STRIDE * np.arange(nc)
    sel_start = SEL_BLOCK * np.arange(nsel)
    inc = ((cmp_start[:, None] <= sel_start[None, :] + SEL_BLOCK - 1)
           & (cmp_start[:, None] + CMP_LEN - 1 >= sel_start[None, :]))
    inc[nc - 1] = False
    return jnp.asarray(inc.T, dtype=BF16)


def _nsa_cmp(qt, kc, vct, gates_t):
    b, _, _, s = qt.shape
    nc = kc.shape[2]
    nsel = s // SEL_BLOCK
    n_top = min(SEL_TOP, nsel)
    inct = _incidence_t(nc, nsel)
    return pl.pallas_call(
        functools.partial(_nsa_cmp_kernel, n_top=n_top),
        grid=(b, NSA_GROUPS, s // TQ),
        in_specs=[
            pl.BlockSpec((None, NSA_HPG, HEAD_DIM, TQ), lambda i, g, j: (i, g, 0, j)),
            pl.BlockSpec((None, None, nc, HEAD_DIM), lambda i, g, j: (i, g, 0, 0)),
            pl.BlockSpec((None, None, HEAD_DIM, nc), lambda i, g, j: (i, g, 0, 0)),
            pl.BlockSpec((nsel, nc), lambda i, g, j: (0, 0)),
            pl.BlockSpec((None, LANES, TQ), lambda i, g, j: (i, 0, j)),
        ],
        out_specs=[
            pl.BlockSpec((None, NSA_HPG, HEAD_DIM, TQ), lambda i, g, j: (i, g, 0, j)),
            pl.BlockSpec((None, None, nsel, TQ), lambda i, g, j: (i, g, 0, j)),
        ],
        out_shape=[
            jax.ShapeDtypeStruct((b, NSA_HEADS, HEAD_DIM, s), F32),
            jax.ShapeDtypeStruct((b, NSA_GROUPS, nsel, s), F32),
        ],
        scratch_shapes=[pltpu.VMEM((nsel, TQ), F32), pltpu.VMEM((nsel, TQ), I32)],
        compiler_params=_params("arbitrary", "arbitrary", "arbitrary"),
        name="nsa_cmp_select",
    )(qt, kc, vct, inct, gates_t)


def _alibi_rows(slope, q0, rows):
    r = lax.broadcasted_iota(I32, (rows, TQ), 0)
    return jnp.where(r == 0, slope * POS_SPLIT, jnp.where(r == 1, slope, jnp.where(r == 2, -slope * q0, 0.0)))


def _wide_dist(width):
    lane = lax.broadcasted_iota(I32, (TK, width), 1) & (TQ - 1)
    return lane - lax.broadcasted_iota(I32, (TK, width), 0)


def _flash_init(s, pv, m_ref, l_ref, acc_ref):
    m = jnp.max(s, axis=0, keepdims=True)
    p = jnp.exp(s - m)
    m_ref[...] = m
    l_ref[...] = jnp.sum(p, axis=0, keepdims=True)
    acc_ref[...] = pv(p.astype(BF16))


def _flash_update(s, pv, m_ref, l_ref, acc_ref):
    m_old = m_ref[...]
    m_new = jnp.maximum(m_old, jnp.max(s, axis=0, keepdims=True))
    alpha = jnp.exp(m_old - m_new)
    p = jnp.exp(s - m_new)
    m_ref[...] = m_new
    l_ref[...] = alpha * l_ref[...] + jnp.sum(p, axis=0, keepdims=True)
    acc_ref[...] = alpha * acc_ref[...] + pv(p.astype(BF16))


class _FlashPipe:
    def __init__(self, scores, values, m_ref, l_ref, acc_ref, s_ref, p_ref, alpha_ref):
        self.scores, self.values = scores, values
        self.m_ref, self.l_ref, self.acc_ref = m_ref, l_ref, acc_ref
        self.s_ref, self.p_ref, self.alpha_ref = s_ref, p_ref, alpha_ref

    def run(self, first_scores, first_tile, n_past):
        m = jnp.max(first_scores, axis=0, keepdims=True)
        p = jnp.exp(first_scores - m)
        self.m_ref[...] = m
        self.l_ref[...] = jnp.sum(p, axis=0, keepdims=True)
        self.p_ref[...] = p.astype(BF16)
        self.alpha_ref[...] = jnp.ones(self.alpha_ref.shape, F32)
        self.acc_ref[...] = jnp.zeros(self.acc_ref.shape, F32)

        @pl.when(n_past > 0)
        def _():
            self.s_ref[...] = self.scores(0)

        def step(j, carry):
            pending = jnp.where(j == 0, first_tile, j - 1)
            s = self.s_ref[...]
            self.s_ref[...] = self.scores(jnp.minimum(j + 1, n_past - 1))
            pv = self.values(pending, self.p_ref[...])
            self.acc_ref[...] = self.alpha_ref[...] * self.acc_ref[...] + pv
            m_old = self.m_ref[...]
            m_new = jnp.maximum(m_old, jnp.max(s, axis=0, keepdims=True))
            alpha = jnp.exp(m_old - m_new)
            p = jnp.exp(s - m_new)
            self.m_ref[...] = m_new
            self.l_ref[...] = alpha * self.l_ref[...] + jnp.sum(p, axis=0, keepdims=True)
            self.alpha_ref[...] = alpha
            self.p_ref[...] = p.astype(BF16)
            return carry

        lax.fori_loop(0, n_past, step, 0)
        pending = jnp.where(n_past == 0, first_tile, n_past - 1)
        acc = self.alpha_ref[...] * self.acc_ref[...] + self.values(pending, self.p_ref[...])
        return acc / self.l_ref[...]


def _pos_cols(s, width):
    pos = np.arange(s)
    cols = np.zeros((s, width), np.float32)
    cols[:, 0] = pos // POS_SPLIT
    cols[:, 1] = pos % POS_SPLIT
    cols[:, 2] = 1.0
    return cols


def _onehot_cols(s, block, width):
    cols = np.zeros((s, width), np.float32)
    cols[np.arange(s), np.arange(s) // block] = 1.0
    return cols


def _aug_key_tiles(z, extra, b, s, nh):
    e = extra.shape[1]
    zk = z.reshape(b, s, nh, HEAD_DIM)
    ex = jnp.broadcast_to(jnp.asarray(extra, BF16)[None, :, None, :], (b, s, nh, e))
    za = jnp.concatenate([zk, ex], axis=-1)
    return za.reshape(b, s // TK, TK, nh, HEAD_DIM + e).transpose(0, 3, 1, 2, 4)


def _nsa_selwin_kernel(q_ref, ks_ref, vst_ref, kw_ref, vwt_ref, sel_ref, gate_ref, accin_ref, o_ref,
                       qa_ref, ms_ref, ls_ref, as_ref, mw_ref, lw_ref, aw_ref, s_ref, p_ref, alpha_ref):
    g = pl.program_id(1)
    i = pl.program_id(2)
    q0 = (i * TQ).astype(F32)
    width = NSA_HPG * TQ
    selbias = ((sel_ref[...] - 1.0) * BIG).astype(BF16)
    for h in range(NSA_HPG):
        slope = jnp.where(g == 0, _nsa_slope(0, h), _nsa_slope(1, h))
        lanes = slice(h * TQ, (h + 1) * TQ)
        qa_ref[0:HEAD_DIM, lanes] = q_ref[h]
        qa_ref[HEAD_DIM:2 * HEAD_DIM, lanes] = _alibi_rows(slope, q0, HEAD_DIM).astype(BF16)
        qa_ref[2 * HEAD_DIM:, lanes] = selbias
    qa = qa_ref[...]
    qw = qa_ref[0:2 * HEAD_DIM, :]
    dist = _wide_dist(width)

    def pv_win(j):
        return lambda p: _dot(vwt_ref[j], p)

    causal = dist >= 0
    _flash_init(jnp.where(causal, _dot(kw_ref[i], qw), NEG), pv_win(i), mw_ref, lw_ref, aw_ref)
    for d in range(1, WINDOW // TK + 1):
        @pl.when(i >= d)
        def _():
            s = jnp.where(dist + d * TK < WINDOW, _dot(kw_ref[i - d], qw), NEG)
            _flash_update(s, pv_win(i - d), mw_ref, lw_ref, aw_ref)
    o_win = aw_ref[...] / lw_ref[...]

    pipe = _FlashPipe(lambda j: _dot(ks_ref[j], qa_ref[...]), lambda j, p: _dot(vst_ref[j], p),
                      ms_ref, ls_ref, as_ref, s_ref, p_ref, alpha_ref)
    o_sel = pipe.run(jnp.where(causal, _dot(ks_ref[i], qa), NEG), i, i)
    for h in range(NSA_HPG):
        lanes = slice(h * TQ, (h + 1) * TQ)
        row = g * (3 * NSA_HPG) + 3 * h
        g_sel = gate_ref[pl.ds(row + 1, 1), :]
        g_win = gate_ref[pl.ds(row + 2, 1), :]
        o_ref[h] = (accin_ref[h] + g_sel * o_sel[:, lanes] + g_win * o_win[:, lanes]).astype(o_ref.dtype)


def _nsa_selwin(qt, ks_t, vs_t, kw_t, vw_t, sel_t, gates_t, acc_t):
    b, _, _, s = qt.shape
    nt = s // TK
    width = NSA_HPG * TQ
    ksw = ks_t.shape[-1]
    kww = kw_t.shape[-1]
    vspec = pl.BlockSpec((None, None, nt, HEAD_DIM, TK), lambda i, g, j: (i, g, 0, 0, 0))
    qspec = pl.BlockSpec((None, NSA_HPG, HEAD_DIM, TQ), lambda i, g, j: (i, g, 0, j))
    stat = pltpu.VMEM((1, width), F32)
    accs = pltpu.VMEM((HEAD_DIM, width), F32)
    return pl.pallas_call(
        _nsa_selwin_kernel,
        grid=(b, NSA_GROUPS, s // TQ),
        in_specs=[
            qspec,
            pl.BlockSpec((None, None, nt, TK, ksw), lambda i, g, j: (i, g, 0, 0, 0)), vspec,
            pl.BlockSpec((None, None, nt, TK, kww), lambda i, g, j: (i, g, 0, 0, 0)), vspec,
            pl.BlockSpec((None, None, MAX_SEL_BLOCKS, TQ), lambda i, g, j: (i, g, 0, j)),
            pl.BlockSpec((None, LANES, TQ), lambda i, g, j: (i, 0, j)),
            qspec,
        ],
        out_specs=qspec,
        out_shape=jax.ShapeDtypeStruct((b, NSA_HEADS, HEAD_DIM, s), BF16),
        scratch_shapes=[pltpu.VMEM((ksw, width), BF16), stat, stat, accs, stat, stat, accs,
                        pltpu.VMEM((TK, width), F32), pltpu.VMEM((TK, width), BF16), stat],
        compiler_params=_params("arbitrary", "arbitrary", "arbitrary"),
        name="nsa_select_window",
    )(qt, ks_t, vs_t, kw_t, vw_t, sel_t, gates_t, acc_t)


def _moba_kernel(q_ref, k_ref, vt_ref, kmean_ref, o_ref, qa_ref, m_ref, l_ref, acc_ref,
                 s_ref, p_ref, alpha_ref, *, n_top):
    hg = pl.program_id(1)
    i = pl.program_id(2)
    nblk = kmean_ref.shape[1]
    q0 = (i * TQ).astype(F32)
    blk = lax.broadcasted_iota(I32, (nblk, 1), 0)
    for h in range(MOBA_HB):
        head = hg * MOBA_HB + h
        slope = jnp.float32(2.0 ** -MOBA_HEADS)
        for hh in range(MOBA_HEADS - 1):
            slope = jnp.where(head == hh, 2.0 ** -(hh + 1), slope)
        q = q_ref[h]
        gate = jnp.where(blk < i, _dot(kmean_ref[h].astype(BF16), q), NEG)
        bias = jnp.full(gate.shape, -BIG, F32)
        for _ in range(n_top):
            mx = jnp.max(gate, axis=0, keepdims=True)
            first = jnp.min(jnp.where(gate == mx, blk, nblk), axis=0, keepdims=True)
            pick = blk == first
            bias = jnp.where(pick, 0.0, bias)
            gate = jnp.where(pick, -jnp.inf, gate)
        bias = jnp.where(blk < i, bias, 0.0)
        if nblk < MAX_MOBA_BLOCKS:
            bias = jnp.concatenate([bias, jnp.zeros((MAX_MOBA_BLOCKS - nblk, TQ), F32)], axis=0)
        qa_ref[h, 0:HEAD_DIM, :] = q
        qa_ref[h, HEAD_DIM:HEAD_DIM + MAX_MOBA_BLOCKS, :] = _alibi_rows(slope, q0, MAX_MOBA_BLOCKS).astype(BF16)
        qa_ref[h, HEAD_DIM + MAX_MOBA_BLOCKS:, :] = bias.astype(BF16)

    def scores(j):
        return jnp.concatenate([_dot(k_ref[h, j], qa_ref[h]) for h in range(MOBA_HB)], axis=1)

    def values(j, p):
        return jnp.concatenate(
            [_dot(vt_ref[h, j], p[:, h * TQ:(h + 1) * TQ]) for h in range(MOBA_HB)], axis=1)

    pipe = _FlashPipe(scores, values, m_ref, l_ref, acc_ref, s_ref, p_ref, alpha_ref)
    o = pipe.run(jnp.where(_wide_dist(MOBA_HB * TQ) >= 0, scores(i), NEG), i, i)
    for h in range(MOBA_HB):
        o_ref[h] = o[:, h * TQ:(h + 1) * TQ].astype(o_ref.dtype)


def _moba(qt, k_t, v_t, kmean):
    b, nh, _, s = qt.shape
    nt = s // TK
    n_top = min(MOBA_TOP, nt - 1)
    hb = MOBA_HB
    kw = k_t.shape[-1]
    width = hb * TQ
    return pl.pallas_call(
        functools.partial(_moba_kernel, n_top=n_top),
        grid=(b, nh // hb, s // TQ),
        in_specs=[
            pl.BlockSpec((None, hb, HEAD_DIM, TQ), lambda i, h, j: (i, h, 0, j)),
            pl.BlockSpec((None, hb, nt, TK, kw), lambda i, h, j: (i, h, 0, 0, 0)),
            pl.BlockSpec((None, hb, nt, HEAD_DIM, TK), lambda i, h, j: (i, h, 0, 0, 0)),
            pl.BlockSpec((None, hb, nt, HEAD_DIM), lambda i, h, j: (i, h, 0, 0)),
        ],
        out_specs=pl.BlockSpec((None, hb, HEAD_DIM, TQ), lambda i, h, j: (i, h, 0, j)),
        out_shape=jax.ShapeDtypeStruct((b, nh, HEAD_DIM, s), BF16),
        scratch_shapes=[
            pltpu.VMEM((hb, kw, TQ), BF16), pltpu.VMEM((1, width), F32), pltpu.VMEM((1, width), F32),
            pltpu.VMEM((HEAD_DIM, width), F32),
            pltpu.VMEM((TK, width), F32), pltpu.VMEM((TK, width), BF16), pltpu.VMEM((1, width), F32),
        ],
        compiler_params=_params("arbitrary", "arbitrary", "arbitrary"),
        name="moba",
    )(qt, k_t, v_t, kmean)


def _outproj_kernel(on_ref, om_ref, w_ref, x_ref, g1_ref, ln_ref, sc_ref, sh_ref,
                    wrh_ref, wrl_ref, rb_ref, x1_ref, h2_ref, lg_ref):
    half = on_ref.shape[1]
    y = _dot(on_ref[...], w_ref[:half, :]) + _dot(om_ref[...], w_ref[half:, :])
    x1 = x_ref[...] + g1_ref[...] * y
    x1_ref[...] = x1
    h2 = _modulated_norm(x1, ln_ref[...], sc_ref[...], sh_ref[...])
    h2_ref[...] = h2
    hh, hl = _hi_lo(h2)
    lg_ref[...] = _dot(hh, wrh_ref[...]) + _dot(hh, wrl_ref[...]) + _dot(hl, wrh_ref[...]) + rb_ref[...]


def _outproj(o_nsa, o_moba, w_out, x2d, g1, ln_g, sc2p, sh2, wr_hi, wr_lo, rb, seq):
    t, d = x2d.shape
    tm = TM_PROJ
    nb = seq // tm
    row = lambda i: (i, 0)
    bat = lambda i: (i // nb, 0, 0)
    const = lambda i: (0, 0)
    half = o_nsa.shape[1]
    return pl.pallas_call(
        _outproj_kernel,
        grid=(t // tm,),
        in_specs=[
            pl.BlockSpec((tm, half), row), pl.BlockSpec((tm, half), row),
            pl.BlockSpec(w_out.shape, const),
            pl.BlockSpec((tm, d), row),
            pl.BlockSpec((None, 1, d), bat),
            pl.BlockSpec((1, d), const),
            pl.BlockSpec((None, 1, d), bat), pl.BlockSpec((None, 1, d), bat),
            pl.BlockSpec(wr_hi.shape, const), pl.BlockSpec(wr_lo.shape, const),
            pl.BlockSpec((1, LANES), const),
        ],
        out_specs=[pl.BlockSpec((tm, d), row), pl.BlockSpec((tm, d), row), pl.BlockSpec((tm, LANES), row)],
        out_shape=[jax.ShapeDtypeStruct((t, d), F32), jax.ShapeDtypeStruct((t, d), F32),
                   jax.ShapeDtypeStruct((t, LANES), F32)],
        compiler_params=_params("arbitrary"),
        name="outproj_router",
    )(o_nsa, o_moba, w_out, x2d, g1, ln_g, sc2p, sh2, wr_hi, wr_lo, rb)


def _route_kernel(lg_ref, idx_ref, gate_ref, rank_ref, cnt_ref, carry_ref):
    @pl.when(pl.program_id(0) == 0)
    def _():
        carry_ref[...] = jnp.zeros(carry_ref.shape, carry_ref.dtype)

    lg = lg_ref[...]
    ne, tt = lg.shape
    eidx = lax.broadcasted_iota(I32, lg.shape, 0)
    vals, idxs, hots = [], [], []
    for _ in range(TOP_K):
        mx = jnp.max(lg, axis=0, keepdims=True)
        first = jnp.min(jnp.where(lg == mx, eidx, ne), axis=0, keepdims=True)
        hot = eidx == first
        vals.append(mx)
        idxs.append(first)
        hots.append(hot)
        lg = jnp.where(hot, -jnp.inf, lg)
    e = [jnp.exp(v - vals[0]) for v in vals]
    tot = e[0]
    for x in e[1:]:
        tot = tot + x
    cnt = jnp.where(hots[0], 1.0, 0.0)
    for hot in hots[1:]:
        cnt = cnt + jnp.where(hot, 1.0, 0.0)
    cntb = cnt.astype(BF16)
    r = lax.broadcasted_iota(I32, (tt, tt), 0)
    c = lax.broadcasted_iota(I32, (tt, tt), 1)
    before = jnp.where(r < c, 1.0, 0.0).astype(BF16)
    carry = carry_ref[...]
    prefix = _dot(cntb, before) + carry
    ranks = [jnp.sum(jnp.where(hot, prefix, 0.0), axis=0, keepdims=True) for hot in hots]
    carry = carry + _dot(cntb, jnp.ones((tt, tt), BF16))
    carry_ref[...] = carry
    cnt_ref[...] = carry[:, :LANES]
    pad_i = jnp.zeros((SUBLANES - TOP_K, tt), I32)
    pad_f = jnp.zeros((SUBLANES - TOP_K, tt), F32)
    idx_ref[...] = jnp.concatenate(idxs + [pad_i], axis=0)
    gate_ref[...] = jnp.concatenate([x / tot for x in e] + [pad_f], axis=0)
    rank_ref[...] = jnp.concatenate([x.astype(I32) for x in ranks] + [pad_i], axis=0)


def _route(logits_t):
    ne, t = logits_t.shape
    tt = min(TT_ROUTE, t)
    tok = pl.BlockSpec((SUBLANES, tt), lambda i: (0, i))
    return pl.pallas_call(
        _route_kernel,
        grid=(t // tt,),
        in_specs=[pl.BlockSpec((ne, tt), lambda i: (0, i))],
        out_specs=[tok, tok, tok, pl.BlockSpec((ne, LANES), lambda i: (0, 0))],
        out_shape=[jax.ShapeDtypeStruct((SUBLANES, t), I32), jax.ShapeDtypeStruct((SUBLANES, t), F32),
                   jax.ShapeDtypeStruct((SUBLANES, t), I32), jax.ShapeDtypeStruct((ne, LANES), F32)],
        scratch_shapes=[pltpu.VMEM((ne, tt), F32)],
        compiler_params=_params("arbitrary"),
        name="route_topk_rank",
    )(logits_t)


def _dispatch_plan(idx, rank, counts):
    tm = TM_MOE
    t = idx.shape[1]
    nb = (t * TOP_K + N_EXPERTS * tm) // tm
    experts = jnp.arange(N_EXPERTS, dtype=I32)
    counts = counts.astype(I32)
    padded = (counts + tm - 1) // tm * tm
    pend = jnp.sum(jnp.where(experts[None, :] <= experts[:, None], padded[None, :], 0), axis=1)
    pstart = pend - padded
    base = jnp.sum(jnp.where(idx[:, :, None] == experts, pstart, 0), axis=-1)
    dest = (rank + base).T.reshape(t * TOP_K).astype(I32)
    blk_e = jnp.sum(jnp.where(pend[None, :] <= (jnp.arange(nb, dtype=I32) * tm)[:, None], 1, 0), axis=1)
    blk_e = jnp.minimum(blk_e, N_EXPERTS - 1).astype(I32)
    nused = (pend[-1] // tm).astype(I32).reshape(1)
    fill_start = jnp.maximum(pend - tm, 0).astype(I32)
    fill_flag = (padded > counts).astype(I32)
    return dest, blk_e, nused, fill_start, fill_flag


def _dispatch_kernel(fill_start_ref, fill_flag_ref, nused_ref, dest_ref, h_ref, xs_hbm, zbuf, zsem, sem):
    i = pl.program_id(0)
    nrow = dest_ref.shape[1]
    ntok = nrow // TOP_K

    @pl.when(i == 0)
    def _():
        zbuf[...] = jnp.zeros(zbuf.shape, zbuf.dtype)

        def fill(start):
            return pltpu.make_async_copy(zbuf, xs_hbm.at[pl.ds(pl.multiple_of(start, TM_MOE), TM_MOE)], zsem)

        for e in range(N_EXPERTS):
            @pl.when(fill_flag_ref[e] != 0)
            def _():
                fill(fill_start_ref[e]).start()
        for e in range(N_EXPERTS):
            @pl.when(fill_flag_ref[e] != 0)
            def _():
                fill(fill_start_ref[e]).wait()

        nblk = xs_hbm.shape[0] // TM_MOE

        def tail_start(b, c):
            fill(b * TM_MOE).start()
            return c

        def tail_wait(b, c):
            fill(b * TM_MOE).wait()
            return c

        lax.fori_loop(nused_ref[0], nblk, tail_start, 0)
        lax.fori_loop(nused_ref[0], nblk, tail_wait, 0)

    def issue(tl, c):
        src = h_ref.at[pl.ds(tl, 1)]
        for k in range(TOP_K):
            pltpu.make_async_copy(src, xs_hbm.at[pl.ds(dest_ref[0, tl * TOP_K + k], 1)], sem).start()
        return c

    lax.fori_loop(0, ntok, issue, 0)
    for _ in range(TOP_K):
        pltpu.make_async_copy(h_ref, xs_hbm.at[pl.ds(0, ntok)], sem).wait()


def _dispatch(fill_start, fill_flag, nused, dest, h2):
    t, d = h2.shape
    td = min(TD_DISPATCH, t)
    r = t * TOP_K + N_EXPERTS * TM_MOE
    grid_spec = pltpu.PrefetchScalarGridSpec(
        num_scalar_prefetch=3,
        grid=(t // td,),
        in_specs=[
            pl.BlockSpec((None, 1, td * TOP_K), lambda i, fs, ff, nu: (i, 0, 0), memory_space=pltpu.SMEM),
            pl.BlockSpec((td, d), lambda i, fs, ff, nu: (i, 0)),
        ],
        out_specs=pl.BlockSpec(memory_space=pl.ANY),
        scratch_shapes=[pltpu.VMEM((TM_MOE, d), F32), pltpu.SemaphoreType.DMA(()), pltpu.SemaphoreType.DMA(())],
    )
    return pl.pallas_call(
        _dispatch_kernel,
        grid_spec=grid_spec,
        out_shape=jax.ShapeDtypeStruct((r, d), F32),
        compiler_params=_params("arbitrary"),
        name="moe_dispatch",
    )(fill_start, fill_flag, nused, dest.reshape(t // td, 1, td * TOP_K), h2)


def _moe_ffn_kernel(blk_e_ref, nused_ref, x_ref, w1_ref, b1_ref, w2_ref, b2_ref, y_ref, w1b, w2b):
    i = pl.program_id(0)
    dff = w2_ref.shape[0]

    @pl.when(i < nused_ref[0])
    def _():
        prev = blk_e_ref[jnp.maximum(i - 1, 0)]

        @pl.when((i == 0) | (blk_e_ref[i] != prev))
        def _():
            w1b[...] = w1_ref[...].astype(BF16)
            w2b[...] = w2_ref[...].astype(BF16)

        u = _dot(x_ref[...].astype(BF16), w1b[...]) + b1_ref[...]
        gl = jnp.minimum(u[:, :dff], SWIGLU_LIMIT)
        lin = jnp.clip(u[:, dff:], -SWIGLU_LIMIT, SWIGLU_LIMIT)
        act = (lin + 1.0) * gl * _sigmoid(SWIGLU_ALPHA * gl)
        y_ref[...] = _dot(act.astype(BF16), w2b[...]) + b2_ref[...]

    @pl.when(i >= nused_ref[0])
    def _():
        y_ref[...] = jnp.zeros(y_ref.shape, y_ref.dtype)


def _moe_ffn(blk_e, nused, xs, w1, b1, w2, b2, layer):
    r, d = xs.shape
    depth, ne, _, f2 = w1.shape
    dff = f2 // 2
    tm = TM_MOE
    rows = lambda i, be, nu: (jnp.minimum(i, nu[0] - 1), 0)
    expert = lambda i, be, nu: (layer, be[i], 0, 0)
    grid_spec = pltpu.PrefetchScalarGridSpec(
        num_scalar_prefetch=2,
        grid=(r // tm,),
        in_specs=[
            pl.BlockSpec((tm, d), rows),
            pl.BlockSpec((None, None, d, f2), expert),
            pl.BlockSpec((None, None, 1, f2), expert),
            pl.BlockSpec((None, None, dff, d), expert),
            pl.BlockSpec((None, None, 1, d), expert),
        ],
        out_specs=pl.BlockSpec((tm, d), lambda i, be, nu: (i, 0)),
        scratch_shapes=[pltpu.VMEM((d, f2), BF16), pltpu.VMEM((dff, d), BF16)],
    )
    return pl.pallas_call(
        _moe_ffn_kernel,
        grid_spec=grid_spec,
        out_shape=jax.ShapeDtypeStruct((r, d), F32),
        compiler_params=_params("arbitrary"),
        name="moe_ffn",
    )(blk_e, nused, xs, w1, b1.reshape(depth, ne, 1, f2), w2, b2.reshape(depth, ne, 1, d))


def _moe_combine_kernel(pos_ref, nxt_ref, y_hbm, gate_ref, x1_ref, g2_ref, o_ref, buf, sems):
    i = pl.program_id(0)
    n = pl.num_programs(0)
    tc = x1_ref.shape[0]
    slot = i % 2

    def fetch(p_ref, s):
        def issue(tl, c):
            for k in range(TOP_K):
                pltpu.make_async_copy(y_hbm.at[pl.ds(p_ref[0, tl * TOP_K + k], 1)],
                                      buf.at[s, k, pl.ds(tl, 1)], sems.at[s]).start()
            return c
        lax.fori_loop(0, tc, issue, 0)

    @pl.when(i == 0)
    def _():
        fetch(pos_ref, 0)

    @pl.when(i + 1 < n)
    def _():
        fetch(nxt_ref, 1 - slot)

    for k in range(TOP_K):
        pltpu.make_async_copy(y_hbm.at[pl.ds(0, tc)], buf.at[slot, k], sems.at[slot]).wait()
    gate = gate_ref[...]
    moe = gate[:, 0:1] * buf[slot, 0]
    for k in range(1, TOP_K):
        moe = moe + gate[:, k:k + 1] * buf[slot, k]
    o_ref[...] = x1_ref[...] + g2_ref[...] * moe


def _moe_combine(pos, y, gate, x1, g2, seq):
    t, d = x1.shape
    tc = TC_COMB
    nb = seq // tc
    n = t // tc
    pos3 = pos.reshape(n, 1, tc * TOP_K)
    return pl.pallas_call(
        _moe_combine_kernel,
        grid=(n,),
        in_specs=[
            pl.BlockSpec((None, 1, tc * TOP_K), lambda i: (i, 0, 0), memory_space=pltpu.SMEM),
            pl.BlockSpec((None, 1, tc * TOP_K), lambda i: (jnp.minimum(i + 1, n - 1), 0, 0),
                         memory_space=pltpu.SMEM),
            pl.BlockSpec(memory_space=pl.ANY),
            pl.BlockSpec((tc, TOP_K), lambda i: (i, 0)),
            pl.BlockSpec((tc, d), lambda i: (i, 0)),
            pl.BlockSpec((None, 1, d), lambda i: (i // nb, 0, 0)),
        ],
        out_specs=pl.BlockSpec((tc, d), lambda i: (i, 0)),
        out_shape=jax.ShapeDtypeStruct((t, d), F32),
        scratch_shapes=[pltpu.VMEM((2, TOP_K, tc, d), F32), pltpu.SemaphoreType.DMA((2,))],
        compiler_params=_params("arbitrary"),
        name="moe_combine",
    )(pos3, pos3, y, gate, x1, g2)


def _tile_gain(gain, width, scale=1.0):
    return (jnp.tile(gain, width // HEAD_DIM) * scale).reshape(1, width).astype(F32)


def _heads_t(z, b, s, nh):
    return z.reshape(b, s, nh, HEAD_DIM).transpose(0, 2, 3, 1)


def _key_tiles(z, b, s, nh):
    return z.reshape(b, s // TK, TK, nh, HEAD_DIM).transpose(0, 3, 1, 2, 4)


def _value_tiles_t(z, b, s, nh):
    return z.reshape(b, s // TK, TK, nh, HEAD_DIM).transpose(0, 3, 1, 4, 2)


def _layer(x2d, mod, b, s, ln1_g, ln2_g, w_in, nsa_q_gain, nsa_k_gain, cmp_pos, cmp_w1, cmp_w2,
           moba_q_gain, moba_k_gain, w_out, router_w, router_b, exp_w1, exp_b1, exp_w2, exp_b2, layer):
    t, d = x2d.shape
    sh1, sc1, g1, sh2, sc2, g2 = [m.reshape(b, 1, d) for m in jnp.split(mod, 6, axis=-1)]
    scale = HEAD_DIM ** -0.5

    (q, kcv, ks, vs, kw, vw, gates, mq, mk, mv, kmean) = _inproj(
        x2d, ln1_g.reshape(1, d), 1.0 + sc1, sh1, _pack_w_in(w_in),
        _tile_gain(nsa_q_gain, 512, scale), _tile_gain(nsa_k_gain, 128),
        _tile_gain(moba_q_gain, 512, scale), _tile_gain(moba_k_gain, 512), s)

    g = NSA_GROUPS
    xc = kcv.reshape(b, s, 2, g, HEAD_DIM).transpose(0, 2, 3, 1, 4).reshape(
        b, 2, g, s // CMP_STRIDE, CMP_STRIDE * HEAD_DIM)
    cmp = _compress(xc, cmp_pos.reshape(2, 1, CMP_LEN * HEAD_DIM), cmp_w1.astype(BF16),
                    cmp_w2.astype(BF16), nsa_k_gain.reshape(1, HEAD_DIM))
    kc = cmp[:, 0].astype(BF16)
    vct = cmp[:, 1].transpose(0, 1, 3, 2).astype(BF16)
    qt = _heads_t(q, b, s, NSA_HEADS)
    gates_t = gates.reshape(b, s, LANES).transpose(0, 2, 1)
    acc_t, sel_t = _nsa_cmp(qt, kc, vct, gates_t)
    nsel = sel_t.shape[2]
    if nsel < MAX_SEL_BLOCKS:
        sel_t = jnp.concatenate([sel_t, jnp.ones((b, g, MAX_SEL_BLOCKS - nsel, s), F32)], axis=2)
    pos_cols = _pos_cols(s, HEAD_DIM)
    sel_cols = np.concatenate([pos_cols, _onehot_cols(s, SEL_BLOCK, MAX_SEL_BLOCKS)], axis=1)
    o_nsa_t = _nsa_selwin(qt, _aug_key_tiles(ks, sel_cols, b, s, g), _value_tiles_t(vs, b, s, g),
                          _aug_key_tiles(kw, pos_cols, b, s, g), _value_tiles_t(vw, b, s, g),
                          sel_t, gates_t, acc_t)
    o_nsa = o_nsa_t.transpose(0, 3, 1, 2).reshape(t, NSA_HEADS * HEAD_DIM)

    km = kmean.reshape(b, s // MOBA_BLOCK, MOBA_HEADS, HEAD_DIM).transpose(0, 2, 1, 3)
    moba_cols = np.concatenate([_pos_cols(s, MAX_MOBA_BLOCKS),
                                _onehot_cols(s, MOBA_BLOCK, MAX_MOBA_BLOCKS)], axis=1)
    o_moba_t = _moba(_heads_t(mq, b, s, MOBA_HEADS), _aug_key_tiles(mk, moba_cols, b, s, MOBA_HEADS),
                     _value_tiles_t(mv, b, s, MOBA_HEADS), km)
    o_moba = o_moba_t.transpose(0, 3, 1, 2).reshape(t, MOBA_HEADS * HEAD_DIM)

    wr = jnp.zeros((d, LANES), F32).at[:, :N_EXPERTS].set(router_w)
    wr_hi = wr.astype(BF16)
    wr_lo = (wr - wr_hi.astype(F32)).astype(BF16)
    rb = jnp.zeros((1, LANES), F32).at[0, :N_EXPERTS].set(router_b)
    x1, h2, logits = _outproj(o_nsa, o_moba, w_out.astype(BF16), x2d, g1, ln2_g.reshape(1, d),
                              1.0 + sc2, sh2, wr_hi, wr_lo, rb, s)

    idx8, gate8, rank8, counts = _route(logits[:, :N_EXPERTS].T)
    dest, blk_e, nused, fill_start, fill_flag = _dispatch_plan(idx8[:TOP_K], rank8[:TOP_K], counts[:, 0])
    xs = _dispatch(fill_start, fill_flag, nused, dest, h2)
    y = _moe_ffn(blk_e, nused, xs, exp_w1, exp_b1, exp_w2, exp_b2, layer)
    return _moe_combine(dest, y, gate8[:TOP_K].T, x1, g2, s)


def kernel(x, c, ada_w, ada_b, ln1_g, ln2_g, w_in, nsa_q_gain, nsa_k_gain, nsa_cmp_pos, nsa_cmp_w1,
           nsa_cmp_w2, moba_q_gain, moba_k_gain, w_out, router_w, router_b, exp_w1, exp_b1, exp_w2, exp_b2):
    b, s, d = x.shape
    depth = ada_w.shape[0]
    mods = _adaln(c, ada_w, ada_b)
    x2d = x.reshape(b * s, d)
    for l in range(depth):
        x2d = _layer(x2d, mods[l], b, s, ln1_g[l], ln2_g[l], w_in[l], nsa_q_gain[l], nsa_k_gain[l],
                     nsa_cmp_pos[l], nsa_cmp_w1[l], nsa_cmp_w2[l], moba_q_gain[l], moba_k_gain[l],
                     w_out[l], router_w[l], router_b[l], exp_w1, exp_b1, exp_w2, exp_b2, l)
    return x2d.reshape(b, s, d)
```

```python
import functools

import numpy as np
import jax
import jax.numpy as jnp
from jax import lax
from jax.experimental import pallas as pl
from jax.experimental.pallas import tpu as pltpu

F32 = jnp.float32
BF16 = jnp.bfloat16
I32 = jnp.int32

HEAD_DIM = 64
NSA_HEADS = 8
NSA_GROUPS = 2
NSA_HPG = NSA_HEADS // NSA_GROUPS
CMP_LEN = 32
CMP_STRIDE = 16
CMP_HID = 128
SEL_BLOCK = 64
SEL_TOP = 16
WINDOW = 512
MOBA_HEADS = 8
MOBA_BLOCK = 256
MOBA_TOP = 3
N_EXPERTS = 32
TOP_K = 4
SWIGLU_LIMIT = 7.0
SWIGLU_ALPHA = 1.702
EPS = 1e-6
NEG = -1e30
FORCE_SCORE = 1e4

LANES = 128
SUBLANES = 8
TQ = 256
TK = 256
POS_SPLIT = 256
BIG = 1e30
MAX_SEL_BLOCKS = 128
MAX_MOBA_BLOCKS = 32
MOBA_HB = 4
RANK_CHUNK = 32
TM_PROJ = 512
TM_MOE = 256
TC_COMB = 128
TT_ROUTE = 512
TD_DISPATCH = 512
VMEM_LIMIT = 56 * 1024 * 1024


def _dot(a, b):
    return jnp.dot(a, b, preferred_element_type=F32)


def _hi_lo(x):
    hi = x.astype(BF16)
    lo = (x - hi.astype(F32)).astype(BF16)
    return hi, lo


def _sigmoid(x):
    return 1.0 / (1.0 + jnp.exp(-x))


def _params(*sem):
    return pltpu.CompilerParams(dimension_semantics=sem, vmem_limit_bytes=VMEM_LIMIT)


def _adaln_kernel(c_ref, w_ref, b_ref, o_ref):
    c = c_ref[...]
    cond = c * _sigmoid(c)
    ch, cl = _hi_lo(cond)
    wh, wl = _hi_lo(w_ref[...])
    o_ref[...] = _dot(ch, wh) + _dot(ch, wl) + _dot(cl, wh) + b_ref[...]


def _adaln(c, ada_w, ada_b):
    depth, d, n6 = ada_w.shape
    b = c.shape[0]
    tn = 1024
    c_pad = jnp.zeros((SUBLANES, d), F32).at[:b].set(c)
    out = pl.pallas_call(
        _adaln_kernel,
        grid=(depth, n6 // tn),
        in_specs=[
            pl.BlockSpec((SUBLANES, d), lambda l, j: (0, 0)),
            pl.BlockSpec((None, d, tn), lambda l, j: (l, 0, j)),
            pl.BlockSpec((None, 1, tn), lambda l, j: (l, 0, j)),
        ],
        out_specs=pl.BlockSpec((None, SUBLANES, tn), lambda l, j: (l, 0, j)),
        out_shape=jax.ShapeDtypeStruct((depth, SUBLANES, n6), F32),
        compiler_params=_params("arbitrary", "arbitrary"),
        name="adaln",
    )(c_pad, ada_w, ada_b.reshape(depth, 1, n6))
    return out[:, :b]


def _block_diag_ones():
    r = lax.broadcasted_iota(I32, (LANES, LANES), 0) // HEAD_DIM
    c = lax.broadcasted_iota(I32, (LANES, LANES), 1) // HEAD_DIM
    return jnp.where(r == c, 1.0, 0.0).astype(BF16)


def _head_norm(z, bd, gain):
    cols = []
    for c0 in range(0, z.shape[1], LANES):
        zc = z[:, c0:c0 + LANES]
        hi, lo = _hi_lo(zc * zc)
        ss = _dot(hi, bd) + _dot(lo, bd)
        cols.append(zc * lax.rsqrt(ss * (1.0 / HEAD_DIM) + EPS))
    zn = cols[0] if len(cols) == 1 else jnp.concatenate(cols, axis=1)
    return zn * gain


def _modulated_norm(x, g, sc1p, sh):
    ms = jnp.mean(x * x, axis=-1, keepdims=True)
    y = x * lax.rsqrt(ms + EPS)
    return (y * g) * sc1p + sh


def _heads_transposed(z, nh):
    return z.T.astype(BF16).reshape(nh, HEAD_DIM, z.shape[0])


def _store_value_tiles(o_ref, z, nh):
    zt = z.T.astype(BF16)
    for h in range(nh):
        for t in range(z.shape[0] // TK):
            o_ref[h, t] = zt[h * HEAD_DIM:(h + 1) * HEAD_DIM, t * TK:(t + 1) * TK]


def _store_key_tiles(o_ref, z, extra, nh):
    rows = z.shape[0]
    for h in range(nh):
        ka = jnp.concatenate([z[:, h * HEAD_DIM:(h + 1) * HEAD_DIM], extra], axis=1).astype(BF16)
        o_ref[h] = ka.reshape(rows // TK, TK, ka.shape[1])


def _pos_cols(pos, width):
    col = lax.broadcasted_iota(I32, (pos.shape[0], width), 1)
    hi = lax.shift_right_logical(pos, POS_SPLIT.bit_length() - 1).astype(F32)
    lo = (pos & (POS_SPLIT - 1)).astype(F32)
    return jnp.where(col == 0, hi, jnp.where(col == 1, lo, jnp.where(col == 2, 1.0, 0.0)))


def _onehot_cols(pos, block, width):
    col = lax.broadcasted_iota(I32, (pos.shape[0], width), 1)
    return jnp.where(lax.shift_right_logical(pos, block.bit_length() - 1) == col, 1.0, 0.0)


def _inproj_kernel(x_ref, g_ref, sc_ref, sh_ref, w_ref, gq_ref, gk_ref, gmq_ref, gmk_ref,
                   q_ref, kcv_ref, ks_ref, vs_ref, kw_ref, vw_ref, gate_ref,
                   mq_ref, mk_ref, mv_ref, kmean_ref):
    h = _modulated_norm(x_ref[...], g_ref[...], sc_ref[...], sh_ref[...])
    hb = h.astype(BF16)
    bd = _block_diag_ones()
    rows = hb.shape[0]
    pos = pl.program_id(1) * rows + lax.broadcasted_iota(I32, (rows, 1), 0)

    def proj(a, b):
        return _dot(hb, w_ref[:, a:b])

    q_ref[...] = _heads_transposed(_head_norm(proj(0, 512), bd, gq_ref[...]), NSA_HEADS)
    kv = proj(512, 1280)
    kcv_ref[...] = kv[:, 0:256]
    pos_cols = _pos_cols(pos, HEAD_DIM)
    sel_cols = jnp.concatenate([pos_cols, _onehot_cols(pos, SEL_BLOCK, MAX_SEL_BLOCKS)], axis=1)
    _store_key_tiles(ks_ref, _head_norm(kv[:, 256:384], bd, gk_ref[...]), sel_cols, NSA_GROUPS)
    _store_value_tiles(vs_ref, kv[:, 384:512], NSA_GROUPS)
    _store_key_tiles(kw_ref, _head_norm(kv[:, 512:640], bd, gk_ref[...]), pos_cols, NSA_GROUPS)
    _store_value_tiles(vw_ref, kv[:, 640:768], NSA_GROUPS)
    gate_ref[...] = _sigmoid(proj(1280, 1408)).T
    mq_ref[...] = _heads_transposed(_head_norm(proj(1408, 1920), bd, gmq_ref[...]), MOBA_HEADS)
    mkn = _head_norm(proj(1920, 2432), bd, gmk_ref[...])
    moba_cols = jnp.concatenate([_pos_cols(pos, MAX_MOBA_BLOCKS),
                                 _onehot_cols(pos, MOBA_BLOCK, MAX_MOBA_BLOCKS)], axis=1)
    _store_key_tiles(mk_ref, mkn, moba_cols, MOBA_HEADS)
    _store_value_tiles(mv_ref, proj(2432, 2944), MOBA_HEADS)
    means = [jnp.sum(mkn[r:r + MOBA_BLOCK], axis=0, keepdims=True) * (1.0 / MOBA_BLOCK)
             for r in range(0, rows, MOBA_BLOCK)]
    kmean_ref[...] = jnp.concatenate(means, axis=0)


def _pack_w_in(w_in):
    d = w_in.shape[0]
    wg = jnp.zeros((d, LANES), w_in.dtype).at[:, :3 * NSA_HEADS].set(w_in[:, 1280:1304])
    return jnp.concatenate([w_in[:, :1280], wg, w_in[:, 1304:]], axis=1).astype(BF16)


def _inproj(x2d, ln_g, sc1p, sh, w_pack, gq, gk, gmq, gmk, b, seq):
    t, d = x2d.shape
    tm = TM_PROJ
    nb = seq // tm
    nt = seq // TK
    tpt = tm // TK
    g = NSA_GROUPS
    row = lambda i, j: (i * nb + j, 0)
    bat = lambda i, j: (i, 0, 0)
    const = lambda i, j: (0, 0)
    heads_t = lambda nh: pl.BlockSpec((None, nh, HEAD_DIM, tm), lambda i, j: (i, 0, 0, j))
    key_tiles = lambda nh, w: pl.BlockSpec((None, nh, tpt, TK, w), lambda i, j: (i, 0, j, 0, 0))
    val_tiles = lambda nh: pl.BlockSpec((None, nh, tpt, HEAD_DIM, TK), lambda i, j: (i, 0, j, 0, 0))
    sel_w = 2 * HEAD_DIM + MAX_SEL_BLOCKS
    win_w = 2 * HEAD_DIM
    moba_w = HEAD_DIM + 2 * MAX_MOBA_BLOCKS
    nmean = tm // MOBA_BLOCK
    sds = jax.ShapeDtypeStruct
    out_shape = [
        sds((b, NSA_HEADS, HEAD_DIM, seq), BF16), sds((t, 256), F32),
        sds((b, g, nt, TK, sel_w), BF16), sds((b, g, nt, HEAD_DIM, TK), BF16),
        sds((b, g, nt, TK, win_w), BF16), sds((b, g, nt, HEAD_DIM, TK), BF16),
        sds((b, LANES, seq), F32),
        sds((b, MOBA_HEADS, HEAD_DIM, seq), BF16),
        sds((b, MOBA_HEADS, nt, TK, moba_w), BF16), sds((b, MOBA_HEADS, nt, HEAD_DIM, TK), BF16),
        sds((t // tm, nmean, 512), F32),
    ]
    out_specs = [
        heads_t(NSA_HEADS), pl.BlockSpec((tm, 256), row),
        key_tiles(g, sel_w), val_tiles(g), key_tiles(g, win_w), val_tiles(g),
        pl.BlockSpec((None, LANES, tm), lambda i, j: (i, 0, j)),
        heads_t(MOBA_HEADS), key_tiles(MOBA_HEADS, moba_w), val_tiles(MOBA_HEADS),
        pl.BlockSpec((None, nmean, 512), lambda i, j: (i * nb + j, 0, 0)),
    ]
    return pl.pallas_call(
        _inproj_kernel,
        grid=(b, nb),
        in_specs=[
            pl.BlockSpec((tm, d), row),
            pl.BlockSpec((1, d), const),
            pl.BlockSpec((None, 1, d), bat),
            pl.BlockSpec((None, 1, d), bat),
            pl.BlockSpec(w_pack.shape, const),
            pl.BlockSpec((1, 512), const),
            pl.BlockSpec((1, 128), const),
            pl.BlockSpec((1, 512), const),
            pl.BlockSpec((1, 512), const),
        ],
        out_specs=out_specs,
        out_shape=out_shape,
        compiler_params=_params("arbitrary", "arbitrary"),
        name="inproj",
    )(x2d, ln_g, sc1p, sh, w_pack, gq, gk, gmq, gmk)


def _compress_kernel(x_ref, pos_ref, w1_ref, w2_ref, gk_ref, o_ref):
    kv = pl.program_id(1)
    half = CMP_STRIDE * HEAD_DIM
    x = x_ref[...]
    xl = (x + pos_ref[:, :half]).astype(BF16)
    xh = (x + pos_ref[:, half:]).astype(BF16)
    a = _dot(xl, w1_ref[:half, :])
    b = _dot(xh, w1_ref[half:, :])
    hid = a + pltpu.roll(b, b.shape[0] - 1, 0)
    out = _dot(jax.nn.gelu(hid, approximate=True).astype(BF16), w2_ref[...])
    ms = jnp.mean(out * out, axis=-1, keepdims=True)
    normed = (out * lax.rsqrt(ms + EPS)) * gk_ref[...]
    o_ref[...] = jnp.where(kv == 0, normed, out)


def _compress(xc, pos, w1, w2, gk):
    b, _, g, nr, w = xc.shape
    return pl.pallas_call(
        _compress_kernel,
        grid=(b, 2, g),
        in_specs=[
            pl.BlockSpec((None, None, None, nr, w), lambda i, k, j: (i, k, j, 0, 0)),
            pl.BlockSpec((None, 1, w * 2), lambda i, k, j: (k, 0, 0)),
            pl.BlockSpec((None, w * 2, CMP_HID), lambda i, k, j: (k, 0, 0)),
            pl.BlockSpec((None, CMP_HID, HEAD_DIM), lambda i, k, j: (k, 0, 0)),
            pl.BlockSpec((1, HEAD_DIM), lambda i, k, j: (0, 0)),
        ],
        out_specs=pl.BlockSpec((None, None, None, nr, HEAD_DIM), lambda i, k, j: (i, k, j, 0, 0)),
        out_shape=jax.ShapeDtypeStruct((b, 2, g, nr, HEAD_DIM), F32),
        compiler_params=_params("arbitrary", "arbitrary", "arbitrary"),
        name="nsa_compress",
    )(xc, pos, w1, w2, gk)


def _nsa_slope(g, h):
    return 2.0 ** (-(g * NSA_HPG + h + 1))


def _rank_counts(imp_ref, cnt_ref, last_blk):
    nb = imp_ref.shape[0]
    chunk = min(RANK_CHUNK, nb)
    cnt_ref[...] = jnp.zeros(cnt_ref.shape, cnt_ref.dtype)
    sub = lax.broadcasted_iota(I32, (SUBLANES, imp_ref.shape[1]), 0)
    for c in range(nb // chunk):
        for d in range(nb // chunk):
            @pl.when(max(c, d) * chunk <= last_blk)
            def _():
                rows0 = range(d * chunk, (d + 1) * chunk, SUBLANES)
                pieces = [imp_ref[r:r + SUBLANES] for r in rows0]
                counts = [jnp.zeros(p.shape, I32) for p in pieces]
                for jp in range(c * chunk, (c + 1) * chunk):
                    row = imp_ref[jp:jp + 1]
                    for k, r0 in enumerate(rows0):
                        p = pieces[k]
                        if r0 + SUBLANES - 1 < jp:
                            inc = jnp.where(row > p, 1, 0)
                        elif r0 > jp:
                            inc = jnp.where(row >= p, 1, 0)
                        else:
                            inc = jnp.where(sub > jp - r0, jnp.where(row >= p, 1, 0), jnp.where(row > p, 1, 0))
                        counts[k] = counts[k] + inc
                rows = slice(d * chunk, (d + 1) * chunk)
                cnt_ref[rows] = cnt_ref[rows] + jnp.concatenate(counts, axis=0)


def _nsa_cmp_kernel(q_ref, kc_ref, vct_ref, inct_ref, gate_ref, acc_ref, sel_ref, flag_ref,
                    imp_ref, cnt_ref, *, n_top):
    g = pl.program_id(1)
    q0 = pl.program_id(2) * TQ
    nc = kc_ref.shape[0]
    nsel = inct_ref.shape[0]
    t = q0 + lax.broadcasted_iota(I32, (1, TQ), 1)
    n = lax.broadcasted_iota(I32, (nc, 1), 0)
    center = (CMP_STRIDE * n).astype(F32) + 0.5 * (CMP_LEN - 1)
    dist = t.astype(F32) - center
    mask = (CMP_STRIDE * n + (CMP_LEN - 1)) <= t
    kc = kc_ref[...]
    vct = vct_ref[...]
    psum = jnp.zeros((nc, TQ), F32)
    for h in range(NSA_HPG):
        slope = jnp.where(g == 0, _nsa_slope(0, h), _nsa_slope(1, h))
        s = _dot(kc, q_ref[h]) - slope * dist
        s = jnp.where(mask, s, NEG)
        m = jnp.max(s, axis=0, keepdims=True)
        e = jnp.exp(s - m)
        l = jnp.sum(e, axis=0, keepdims=True)
        p = jnp.where(mask, e, 0.0) * (1.0 / l)
        o = _dot(vct, p.astype(BF16))
        gate = gate_ref[pl.ds(g * (3 * NSA_HPG) + 3 * h, 1), :]
        acc_ref[h] = gate * o
        psum = psum + p
    hi, lo = _hi_lo(psum)
    inct = inct_ref[...]
    imp = _dot(inct, hi) + _dot(inct, lo)
    blk = lax.broadcasted_iota(I32, (nsel, 1), 0)
    cur = t // SEL_BLOCK
    forced = (blk == 0) | (blk == cur) | (blk == cur - 1)
    imp = jnp.where(forced, FORCE_SCORE, imp)
    imp_ref[...] = jnp.where(blk <= cur, imp, NEG)
    _rank_counts(imp_ref, cnt_ref, (q0 + TQ - 1) // SEL_BLOCK)
    chosen = (cnt_ref[...] < n_top) & (blk <= cur)
    sel = jnp.where(chosen, 1.0, 0.0)
    sel_ref[...] = sel
    bpt = TK // SEL_BLOCK
    ntile = nsel // bpt
    trow = lax.broadcasted_iota(I32, (ntile, nsel), 0)
    tcol = lax.broadcasted_iota(I32, (ntile, nsel), 1)
    member = jnp.where(lax.shift_right_logical(tcol, bpt.bit_length() - 1) == trow, 1.0, 0.0).astype(BF16)
    hits = jnp.sum(_dot(member, sel.astype(BF16)), axis=1, keepdims=True)
    flag_ref[...] = jnp.broadcast_to(hits, (ntile, LANES))


def _incidence_t(nc, nsel):
    cmp_start = CMP_STRIDE * np.arange(nc)
    sel_start = SEL_BLOCK * np.arange(nsel)
    inc = ((cmp_start[:, None] <= sel_start[None, :] + SEL_BLOCK - 1)
           & (cmp_start[:, None] + CMP_LEN - 1 >= sel_start[None, :]))
    inc[nc - 1] = False
    return jnp.asarray(inc.T, dtype=BF16)


def _nsa_cmp(qt, kc, vct, gates_t):
    b, _, _, s = qt.shape
    nc = kc.shape[2]
    nsel = s // SEL_BLOCK
    n_top = min(SEL_TOP, nsel)
    ntile = s // TK
    inct = _incidence_t(nc, nsel)
    return pl.pallas_call(
        functools.partial(_nsa_cmp_kernel, n_top=n_top),
        grid=(b, NSA_GROUPS, s // TQ),
        in_specs=[
            pl.BlockSpec((None, NSA_HPG, HEAD_DIM, TQ), lambda i, g, j: (i, g, 0, j)),
            pl.BlockSpec((None, None, nc, HEAD_DIM), lambda i, g, j: (i, g, 0, 0)),
            pl.BlockSpec((None, None, HEAD_DIM, nc), lambda i, g, j: (i, g, 0, 0)),
            pl.BlockSpec((nsel, nc), lambda i, g, j: (0, 0)),
            pl.BlockSpec((None, LANES, TQ), lambda i, g, j: (i, 0, j)),
        ],
        out_specs=[
            pl.BlockSpec((None, NSA_HPG, HEAD_DIM, TQ), lambda i, g, j: (i, g, 0, j)),
            pl.BlockSpec((None, None, nsel, TQ), lambda i, g, j: (i, g, 0, j)),
            pl.BlockSpec((None, None, None, ntile, LANES), lambda i, g, j: (i, g, j, 0, 0)),
        ],
        out_shape=[
            jax.ShapeDtypeStruct((b, NSA_HEADS, HEAD_DIM, s), F32),
            jax.ShapeDtypeStruct((b, NSA_GROUPS, nsel, s), F32),
            jax.ShapeDtypeStruct((b, NSA_GROUPS, s // TQ, ntile, LANES), F32),
        ],
        scratch_shapes=[pltpu.VMEM((nsel, TQ), F32), pltpu.VMEM((nsel, TQ), I32)],
        compiler_params=_params("arbitrary", "arbitrary", "arbitrary"),
        name="nsa_cmp_select",
    )(qt, kc, vct, inct, gates_t)


def _alibi_rows(slope, q0, rows):
    r = lax.broadcasted_iota(I32, (rows, TQ), 0)
    return jnp.where(r == 0, slope * POS_SPLIT, jnp.where(r == 1, slope, jnp.where(r == 2, -slope * q0, 0.0)))


def _wide_dist(width):
    lane = lax.broadcasted_iota(I32, (TK, width), 1) & (TQ - 1)
    return lane - lax.broadcasted_iota(I32, (TK, width), 0)


def _flash_init(s, pv, m_ref, l_ref, acc_ref):
    m = jnp.max(s, axis=0, keepdims=True)
    p = jnp.exp(s - m)
    m_ref[...] = m
    l_ref[...] = jnp.sum(p, axis=0, keepdims=True)
    acc_ref[...] = pv(p.astype(BF16))


def _flash_update(s, pv, m_ref, l_ref, acc_ref):
    m_old = m_ref[...]
    m_new = jnp.maximum(m_old, jnp.max(s, axis=0, keepdims=True))
    alpha = jnp.exp(m_old - m_new)
    p = jnp.exp(s - m_new)
    m_ref[...] = m_new
    l_ref[...] = alpha * l_ref[...] + jnp.sum(p, axis=0, keepdims=True)
    acc_ref[...] = alpha * acc_ref[...] + pv(p.astype(BF16))


class _FlashPipe:
    def __init__(self, scores, values, m_ref, l_ref, acc_ref, s_ref, p_ref, alpha_ref, tile_of=lambda j: j):
        self.scores, self.values, self.tile_of = scores, values, tile_of
        self.m_ref, self.l_ref, self.acc_ref = m_ref, l_ref, acc_ref
        self.s_ref, self.p_ref, self.alpha_ref = s_ref, p_ref, alpha_ref

    def run(self, first_scores, first_tile, n_past):
        m = jnp.max(first_scores, axis=0, keepdims=True)
        p = jnp.exp(first_scores - m)
        self.m_ref[...] = m
        self.l_ref[...] = jnp.sum(p, axis=0, keepdims=True)
        self.p_ref[...] = p.astype(BF16)
        self.alpha_ref[...] = jnp.ones(self.alpha_ref.shape, F32)
        self.acc_ref[...] = jnp.zeros(self.acc_ref.shape, F32)

        @pl.when(n_past > 0)
        def _():
            self.s_ref[...] = self.scores(self.tile_of(0))

        def step(j, carry):
            pending = jnp.where(j == 0, first_tile, self.tile_of(jnp.maximum(j - 1, 0)))
            s = self.s_ref[...]
            self.s_ref[...] = self.scores(self.tile_of(jnp.minimum(j + 1, n_past - 1)))
            pv = self.values(pending, self.p_ref[...])
            self.acc_ref[...] = self.alpha_ref[...] * self.acc_ref[...] + pv
            m_old = self.m_ref[...]
            m_new = jnp.maximum(m_old, jnp.max(s, axis=0, keepdims=True))
            alpha = jnp.exp(m_old - m_new)
            p = jnp.exp(s - m_new)
            self.m_ref[...] = m_new
            self.l_ref[...] = alpha * self.l_ref[...] + jnp.sum(p, axis=0, keepdims=True)
            self.alpha_ref[...] = alpha
            self.p_ref[...] = p.astype(BF16)
            return carry

        lax.fori_loop(0, n_past, step, 0)
        pending = jnp.where(n_past == 0, first_tile, self.tile_of(jnp.maximum(n_past - 1, 0)))
        acc = self.alpha_ref[...] * self.acc_ref[...] + self.values(pending, self.p_ref[...])
        return acc / self.l_ref[...]


def _nsa_selwin_kernel(flags_ref, q_ref, ks_ref, vst_ref, kw_ref, vwt_ref, sel_ref, gate_ref, accin_ref,
                       o_ref, qa_ref, ms_ref, ls_ref, as_ref, mw_ref, lw_ref, aw_ref, s_ref, p_ref,
                       alpha_ref, visit_ref):
    g = pl.program_id(1)
    i = pl.program_id(2)
    ntile = ks_ref.shape[0]
    base = ((pl.program_id(0) * NSA_GROUPS + g) * pl.num_programs(2) + i) * ntile

    def note(jt, n):
        visit_ref[n] = jt
        return n + flags_ref[base + jt]

    n_visit = lax.fori_loop(0, i, note, 0)
    q0 = (i * TQ).astype(F32)
    width = NSA_HPG * TQ
    selbias = ((sel_ref[...] - 1.0) * BIG).astype(BF16)
    for h in range(NSA_HPG):
        slope = jnp.where(g == 0, _nsa_slope(0, h), _nsa_slope(1, h))
        lanes = slice(h * TQ, (h + 1) * TQ)
        qa_ref[0:HEAD_DIM, lanes] = q_ref[h]
        qa_ref[HEAD_DIM:2 * HEAD_DIM, lanes] = _alibi_rows(slope, q0, HEAD_DIM).astype(BF16)
        qa_ref[2 * HEAD_DIM:, lanes] = selbias
    qa = qa_ref[...]
    qw = qa_ref[0:2 * HEAD_DIM, :]
    dist = _wide_dist(width)

    def pv_win(j):
        return lambda p: _dot(vwt_ref[j], p)

    causal = dist >= 0
    _flash_init(jnp.where(causal, _dot(kw_ref[i], qw), NEG), pv_win(i), mw_ref, lw_ref, aw_ref)
    for d in range(1, WINDOW // TK + 1):
        @pl.when(i >= d)
        def _():
            s = jnp.where(dist + d * TK < WINDOW, _dot(kw_ref[i - d], qw), NEG)
            _flash_update(s, pv_win(i - d), mw_ref, lw_ref, aw_ref)
    o_win = aw_ref[...] / lw_ref[...]

    pipe = _FlashPipe(lambda j: _dot(ks_ref[j], qa_ref[...]), lambda j, p: _dot(vst_ref[j], p),
                      ms_ref, ls_ref, as_ref, s_ref, p_ref, alpha_ref, tile_of=lambda j: visit_ref[j])
    o_sel = pipe.run(jnp.where(causal, _dot(ks_ref[i], qa), NEG), i, n_visit)
    for h in range(NSA_HPG):
        lanes = slice(h * TQ, (h + 1) * TQ)
        row = g * (3 * NSA_HPG) + 3 * h
        g_sel = gate_ref[pl.ds(row + 1, 1), :]
        g_win = gate_ref[pl.ds(row + 2, 1), :]
        o_ref[h] = (accin_ref[h] + g_sel * o_sel[:, lanes] + g_win * o_win[:, lanes]).astype(o_ref.dtype)


def _nsa_selwin(tile_flags, qt, ks_t, vs_t, kw_t, vw_t, sel_t, gates_t, acc_t):
    b, _, _, s = qt.shape
    nt = s // TK
    width = NSA_HPG * TQ
    ksw = ks_t.shape[-1]
    kww = kw_t.shape[-1]
    vspec = pl.BlockSpec((None, None, nt, HEAD_DIM, TK), lambda i, g, j, fl: (i, g, 0, 0, 0))
    qspec = pl.BlockSpec((None, NSA_HPG, HEAD_DIM, TQ), lambda i, g, j, fl: (i, g, 0, j))
    stat = pltpu.VMEM((1, width), F32)
    accs = pltpu.VMEM((HEAD_DIM, width), F32)
    grid_spec = pltpu.PrefetchScalarGridSpec(
        num_scalar_prefetch=1,
        grid=(b, NSA_GROUPS, s // TQ),
        in_specs=[
            qspec,
            pl.BlockSpec((None, None, nt, TK, ksw), lambda i, g, j, fl: (i, g, 0, 0, 0)), vspec,
            pl.BlockSpec((None, None, nt, TK, kww), lambda i, g, j, fl: (i, g, 0, 0, 0)), vspec,
            pl.BlockSpec((None, None, MAX_SEL_BLOCKS, TQ), lambda i, g, j, fl: (i, g, 0, j)),
            pl.BlockSpec((None, LANES, TQ), lambda i, g, j, fl: (i, 0, j)),
            qspec,
        ],
        out_specs=qspec,
        scratch_shapes=[pltpu.VMEM((ksw, width), BF16), stat, stat, accs, stat, stat, accs,
                        pltpu.VMEM((TK, width), F32), pltpu.VMEM((TK, width), BF16), stat,
                        pltpu.SMEM((nt,), I32)],
    )
    return pl.pallas_call(
        _nsa_selwin_kernel,
        grid_spec=grid_spec,
        out_shape=jax.ShapeDtypeStruct((b, NSA_HEADS, HEAD_DIM, s), BF16),
        compiler_params=_params("arbitrary", "arbitrary", "arbitrary"),
        name="nsa_select_window",
    )(tile_flags, qt, ks_t, vs_t, kw_t, vw_t, sel_t, gates_t, acc_t)


def _moba_kernel(q_ref, k_ref, vt_ref, kmean_ref, o_ref, qa_ref, m_ref, l_ref, acc_ref,
                 s_ref, p_ref, alpha_ref, *, n_top):
    hg = pl.program_id(1)
    i = pl.program_id(2)
    nblk = kmean_ref.shape[1]
    q0 = (i * TQ).astype(F32)
    blk = lax.broadcasted_iota(I32, (nblk, 1), 0)
    for h in range(MOBA_HB):
        head = hg * MOBA_HB + h
        slope = jnp.float32(2.0 ** -MOBA_HEADS)
        for hh in range(MOBA_HEADS - 1):
            slope = jnp.where(head == hh, 2.0 ** -(hh + 1), slope)
        q = q_ref[h]
        gate = jnp.where(blk < i, _dot(kmean_ref[h].astype(BF16), q), NEG)
        bias = jnp.full(gate.shape, -BIG, F32)
        for _ in range(n_top):
            mx = jnp.max(gate, axis=0, keepdims=True)
            first = jnp.min(jnp.where(gate == mx, blk, nblk), axis=0, keepdims=True)
            pick = blk == first
            bias = jnp.where(pick, 0.0, bias)
            gate = jnp.where(pick, -jnp.inf, gate)
        bias = jnp.where(blk < i, bias, 0.0)
        if nblk < MAX_MOBA_BLOCKS:
            bias = jnp.concatenate([bias, jnp.zeros((MAX_MOBA_BLOCKS - nblk, TQ), F32)], axis=0)
        qa_ref[h, 0:HEAD_DIM, :] = q
        qa_ref[h, HEAD_DIM:HEAD_DIM + MAX_MOBA_BLOCKS, :] = _alibi_rows(slope, q0, MAX_MOBA_BLOCKS).astype(BF16)
        qa_ref[h, HEAD_DIM + MAX_MOBA_BLOCKS:, :] = bias.astype(BF16)

    def scores(j):
        return jnp.concatenate([_dot(k_ref[h, j], qa_ref[h]) for h in range(MOBA_HB)], axis=1)

    def values(j, p):
        return jnp.concatenate(
            [_dot(vt_ref[h, j], p[:, h * TQ:(h + 1) * TQ]) for h in range(MOBA_HB)], axis=1)

    pipe = _FlashPipe(scores, values, m_ref, l_ref, acc_ref, s_ref, p_ref, alpha_ref)
    o = pipe.run(jnp.where(_wide_dist(MOBA_HB * TQ) >= 0, scores(i), NEG), i, i)
    for h in range(MOBA_HB):
        o_ref[h] = o[:, h * TQ:(h + 1) * TQ].astype(o_ref.dtype)


def _moba(qt, k_t, v_t, kmean):
    b, nh, _, s = qt.shape
    nt = s // TK
    n_top = min(MOBA_TOP, nt - 1)
    hb = MOBA_HB
    kw = k_t.shape[-1]
    width = hb * TQ
    return pl.pallas_call(
        functools.partial(_moba_kernel, n_top=n_top),
        grid=(b, nh // hb, s // TQ),
        in_specs=[
            pl.BlockSpec((None, hb, HEAD_DIM, TQ), lambda i, h, j: (i, h, 0, j)),
            pl.BlockSpec((None, hb, nt, TK, kw), lambda i, h, j: (i, h, 0, 0, 0)),
            pl.BlockSpec((None, hb, nt, HEAD_DIM, TK), lambda i, h, j: (i, h, 0, 0, 0)),
            pl.BlockSpec((None, hb, nt, HEAD_DIM), lambda i, h, j: (i, h, 0, 0)),
        ],
        out_specs=pl.BlockSpec((None, hb, HEAD_DIM, TQ), lambda i, h, j: (i, h, 0, j)),
        out_shape=jax.ShapeDtypeStruct((b, nh, HEAD_DIM, s), BF16),
        scratch_shapes=[
            pltpu.VMEM((hb, kw, TQ), BF16), pltpu.VMEM((1, width), F32), pltpu.VMEM((1, width), F32),
            pltpu.VMEM((HEAD_DIM, width), F32),
            pltpu.VMEM((TK, width), F32), pltpu.VMEM((TK, width), BF16), pltpu.VMEM((1, width), F32),
        ],
        compiler_params=_params("arbitrary", "arbitrary", "arbitrary"),
        name="moba",
    )(qt, k_t, v_t, kmean)


def _outproj_kernel(on_ref, om_ref, w_ref, x_ref, g1_ref, ln_ref, sc_ref, sh_ref,
                    wrh_ref, wrl_ref, rb_ref, x1_ref, h2_ref, lg_ref):
    nh, dh, rows = on_ref.shape
    half = nh * dh

    def rows_major(o_ref):
        return o_ref[...].reshape(half, rows).astype(F32).T.astype(BF16)

    y = _dot(rows_major(on_ref), w_ref[:half, :]) + _dot(rows_major(om_ref), w_ref[half:, :])
    x1 = x_ref[...] + g1_ref[...] * y
    x1_ref[...] = x1
    h2 = _modulated_norm(x1, ln_ref[...], sc_ref[...], sh_ref[...])
    h2_ref[...] = h2
    hh, hl = _hi_lo(h2)
    logits = _dot(hh, wrh_ref[...]) + _dot(hh, wrl_ref[...]) + _dot(hl, wrh_ref[...]) + rb_ref[...]
    lg_ref[...] = logits.T


def _outproj(o_nsa_t, o_moba_t, w_out, x2d, g1, ln_g, sc2p, sh2, wr_hi, wr_lo, rb):
    t, d = x2d.shape
    b, nh, dh, seq = o_nsa_t.shape
    tm = TM_PROJ
    nb = seq // tm
    row = lambda i, j: (i * nb + j, 0)
    bat = lambda i, j: (i, 0, 0)
    const = lambda i, j: (0, 0)
    heads = pl.BlockSpec((None, nh, dh, tm), lambda i, j: (i, 0, 0, j))
    return pl.pallas_call(
        _outproj_kernel,
        grid=(b, nb),
        in_specs=[
            heads, heads,
            pl.BlockSpec(w_out.shape, const),
            pl.BlockSpec((tm, d), row),
            pl.BlockSpec((None, 1, d), bat),
            pl.BlockSpec((1, d), const),
            pl.BlockSpec((None, 1, d), bat), pl.BlockSpec((None, 1, d), bat),
            pl.BlockSpec(wr_hi.shape, const), pl.BlockSpec(wr_lo.shape, const),
            pl.BlockSpec((1, LANES), const),
        ],
        out_specs=[pl.BlockSpec((tm, d), row), pl.BlockSpec((tm, d), row),
                   pl.BlockSpec((LANES, tm), lambda i, j: (0, i * nb + j))],
        out_shape=[jax.ShapeDtypeStruct((t, d), F32), jax.ShapeDtypeStruct((t, d), F32),
                   jax.ShapeDtypeStruct((LANES, t), F32)],
        compiler_params=_params("arbitrary", "arbitrary"),
        name="outproj_router",
    )(o_nsa_t, o_moba_t, w_out, x2d, g1, ln_g, sc2p, sh2, wr_hi, wr_lo, rb)


def _route_kernel(lg_ref, idx_ref, gate_ref, rank_ref, cnt_ref, carry_ref):
    @pl.when(pl.program_id(0) == 0)
    def _():
        carry_ref[...] = jnp.zeros(carry_ref.shape, carry_ref.dtype)

    lg = lg_ref[...]
    ne, tt = lg.shape
    eidx = lax.broadcasted_iota(I32, lg.shape, 0)
    vals, idxs, hots = [], [], []
    for _ in range(TOP_K):
        mx = jnp.max(lg, axis=0, keepdims=True)
        first = jnp.min(jnp.where(lg == mx, eidx, ne), axis=0, keepdims=True)
        hot = eidx == first
        vals.append(mx)
        idxs.append(first)
        hots.append(hot)
        lg = jnp.where(hot, -jnp.inf, lg)
    e = [jnp.exp(v - vals[0]) for v in vals]
    tot = e[0]
    for x in e[1:]:
        tot = tot + x
    cnt = jnp.where(hots[0], 1.0, 0.0)
    for hot in hots[1:]:
        cnt = cnt + jnp.where(hot, 1.0, 0.0)
    cntb = cnt.astype(BF16)
    r = lax.broadcasted_iota(I32, (tt, tt), 0)
    c = lax.broadcasted_iota(I32, (tt, tt), 1)
    before = jnp.where(r < c, 1.0, 0.0).astype(BF16)
    carry = carry_ref[...]
    prefix = _dot(cntb, before) + carry
    ranks = [jnp.sum(jnp.where(hot, prefix, 0.0), axis=0, keepdims=True) for hot in hots]
    carry = carry + _dot(cntb, jnp.ones((tt, tt), BF16))
    carry_ref[...] = carry
    cnt_ref[...] = carry[:, :LANES]
    pad_i = jnp.zeros((SUBLANES - TOP_K, tt), I32)
    pad_f = jnp.zeros((SUBLANES - TOP_K, tt), F32)
    idx_ref[...] = jnp.concatenate(idxs + [pad_i], axis=0)
    gate_ref[...] = jnp.concatenate([x / tot for x in e] + [pad_f], axis=0)
    rank_ref[...] = jnp.concatenate([x.astype(I32) for x in ranks] + [pad_i], axis=0)


def _route(logits_t):
    ne, t = N_EXPERTS, logits_t.shape[1]
    tt = min(TT_ROUTE, t)
    tok = pl.BlockSpec((SUBLANES, tt), lambda i: (0, i))
    return pl.pallas_call(
        _route_kernel,
        grid=(t // tt,),
        in_specs=[pl.BlockSpec((ne, tt), lambda i: (0, i))],
        out_specs=[tok, tok, tok, pl.BlockSpec((ne, LANES), lambda i: (0, 0))],
        out_shape=[jax.ShapeDtypeStruct((SUBLANES, t), I32), jax.ShapeDtypeStruct((SUBLANES, t), F32),
                   jax.ShapeDtypeStruct((SUBLANES, t), I32), jax.ShapeDtypeStruct((ne, LANES), F32)],
        scratch_shapes=[pltpu.VMEM((ne, tt), F32)],
        compiler_params=_params("arbitrary"),
        name="route_topk_rank",
    )(logits_t)


def _dispatch_plan(idx, rank, counts):
    tm = TM_MOE
    t = idx.shape[1]
    nb = (t * TOP_K + N_EXPERTS * tm) // tm
    experts = jnp.arange(N_EXPERTS, dtype=I32)
    counts = counts.astype(I32)
    padded = (counts + tm - 1) // tm * tm
    pend = jnp.sum(jnp.where(experts[None, :] <= experts[:, None], padded[None, :], 0), axis=1)
    pstart = pend - padded
    base = jnp.sum(jnp.where(idx[:, :, None] == experts, pstart, 0), axis=-1)
    dest = (rank + base).T.reshape(t * TOP_K).astype(I32)
    blk_e = jnp.sum(jnp.where(pend[None, :] <= (jnp.arange(nb, dtype=I32) * tm)[:, None], 1, 0), axis=1)
    blk_e = jnp.minimum(blk_e, N_EXPERTS - 1).astype(I32)
    nused = (pend[-1] // tm).astype(I32).reshape(1)
    fill_start = jnp.maximum(pend - tm, 0).astype(I32)
    fill_flag = (padded > counts).astype(I32)
    return dest, blk_e, nused, fill_start, fill_flag


def _dispatch_kernel(fill_start_ref, fill_flag_ref, nused_ref, dest_ref, h_ref, xs_hbm, zbuf, zsem, sem):
    i = pl.program_id(0)
    nrow = dest_ref.shape[1]
    ntok = nrow // TOP_K

    @pl.when(i == 0)
    def _():
        zbuf[...] = jnp.zeros(zbuf.shape, zbuf.dtype)

        def fill(start):
            return pltpu.make_async_copy(zbuf, xs_hbm.at[pl.ds(pl.multiple_of(start, TM_MOE), TM_MOE)], zsem)

        for e in range(N_EXPERTS):
            @pl.when(fill_flag_ref[e] != 0)
            def _():
                fill(fill_start_ref[e]).start()
        for e in range(N_EXPERTS):
            @pl.when(fill_flag_ref[e] != 0)
            def _():
                fill(fill_start_ref[e]).wait()

        nblk = xs_hbm.shape[0] // TM_MOE

        def tail_start(b, c):
            fill(b * TM_MOE).start()
            return c

        def tail_wait(b, c):
            fill(b * TM_MOE).wait()
            return c

        lax.fori_loop(nused_ref[0], nblk, tail_start, 0)
        lax.fori_loop(nused_ref[0], nblk, tail_wait, 0)

    def issue(tl, c):
        src = h_ref.at[pl.ds(tl, 1)]
        for k in range(TOP_K):
            pltpu.make_async_copy(src, xs_hbm.at[pl.ds(dest_ref[0, tl * TOP_K + k], 1)], sem).start()
        return c

    lax.fori_loop(0, ntok, issue, 0)
    for _ in range(TOP_K):
        pltpu.make_async_copy(h_ref, xs_hbm.at[pl.ds(0, ntok)], sem).wait()


def _dispatch(fill_start, fill_flag, nused, dest, h2):
    t, d = h2.shape
    td = min(TD_DISPATCH, t)
    r = t * TOP_K + N_EXPERTS * TM_MOE
    grid_spec = pltpu.PrefetchScalarGridSpec(
        num_scalar_prefetch=3,
        grid=(t // td,),
        in_specs=[
            pl.BlockSpec((None, 1, td * TOP_K), lambda i, fs, ff, nu: (i, 0, 0), memory_space=pltpu.SMEM),
            pl.BlockSpec((td, d), lambda i, fs, ff, nu: (i, 0)),
        ],
        out_specs=pl.BlockSpec(memory_space=pl.ANY),
        scratch_shapes=[pltpu.VMEM((TM_MOE, d), F32), pltpu.SemaphoreType.DMA(()), pltpu.SemaphoreType.DMA(())],
    )
    return pl.pallas_call(
        _dispatch_kernel,
        grid_spec=grid_spec,
        out_shape=jax.ShapeDtypeStruct((r, d), F32),
        compiler_params=_params("arbitrary"),
        name="moe_dispatch",
    )(fill_start, fill_flag, nused, dest.reshape(t // td, 1, td * TOP_K), h2)


def _moe_ffn_kernel(blk_e_ref, nused_ref, x_ref, w1_ref, b1_ref, w2_ref, b2_ref, y_ref, w1b, w2b):
    i = pl.program_id(0)
    dff = w2_ref.shape[0]

    @pl.when(i < nused_ref[0])
    def _():
        prev = blk_e_ref[jnp.maximum(i - 1, 0)]

        @pl.when((i == 0) | (blk_e_ref[i] != prev))
        def _():
            w1b[...] = w1_ref[...].astype(BF16)
            w2b[...] = w2_ref[...].astype(BF16)

        u = _dot(x_ref[...].astype(BF16), w1b[...]) + b1_ref[...]
        gl = jnp.minimum(u[:, :dff], SWIGLU_LIMIT)
        lin = jnp.clip(u[:, dff:], -SWIGLU_LIMIT, SWIGLU_LIMIT)
        act = (lin + 1.0) * gl * _sigmoid(SWIGLU_ALPHA * gl)
        y_ref[...] = _dot(act.astype(BF16), w2b[...]) + b2_ref[...]

    @pl.when(i >= nused_ref[0])
    def _():
        y_ref[...] = jnp.zeros(y_ref.shape, y_ref.dtype)


def _moe_ffn(blk_e, nused, xs, w1, b1, w2, b2, layer):
    r, d = xs.shape
    depth, ne, _, f2 = w1.shape
    dff = f2 // 2
    tm = TM_MOE
    rows = lambda i, be, nu: (jnp.minimum(i, nu[0] - 1), 0)
    expert = lambda i, be, nu: (layer, be[i], 0, 0)
    grid_spec = pltpu.PrefetchScalarGridSpec(
        num_scalar_prefetch=2,
        grid=(r // tm,),
        in_specs=[
            pl.BlockSpec((tm, d), rows),
            pl.BlockSpec((None, None, d, f2), expert),
            pl.BlockSpec((None, None, 1, f2), expert),
            pl.BlockSpec((None, None, dff, d), expert),
            pl.BlockSpec((None, None, 1, d), expert),
        ],
        out_specs=pl.BlockSpec((tm, d), lambda i, be, nu: (i, 0)),
        scratch_shapes=[pltpu.VMEM((d, f2), BF16), pltpu.VMEM((dff, d), BF16)],
    )
    return pl.pallas_call(
        _moe_ffn_kernel,
        grid_spec=grid_spec,
        out_shape=jax.ShapeDtypeStruct((r, d), F32),
        compiler_params=_params("arbitrary"),
        name="moe_ffn",
    )(blk_e, nused, xs, w1, b1.reshape(depth, ne, 1, f2), w2, b2.reshape(depth, ne, 1, d))


def _moe_combine_kernel(pos_ref, nxt_ref, y_hbm, gate_ref, x1_ref, g2_ref, o_ref, buf, sems):
    i = pl.program_id(0)
    n = pl.num_programs(0)
    tc = x1_ref.shape[0]
    slot = i % 2

    def fetch(p_ref, s):
        def issue(tl, c):
            for k in range(TOP_K):
                pltpu.make_async_copy(y_hbm.at[pl.ds(p_ref[0, tl * TOP_K + k], 1)],
                                      buf.at[s, k, pl.ds(tl, 1)], sems.at[s]).start()
            return c
        lax.fori_loop(0, tc, issue, 0)

    @pl.when(i == 0)
    def _():
        fetch(pos_ref, 0)

    @pl.when(i + 1 < n)
    def _():
        fetch(nxt_ref, 1 - slot)

    for k in range(TOP_K):
        pltpu.make_async_copy(y_hbm.at[pl.ds(0, tc)], buf.at[slot, k], sems.at[slot]).wait()
    gate = gate_ref[...]
    moe = gate[:, 0:1] * buf[slot, 0]
    for k in range(1, TOP_K):
        moe = moe + gate[:, k:k + 1] * buf[slot, k]
    o_ref[...] = x1_ref[...] + g2_ref[...] * moe


def _moe_combine(pos, y, gate, x1, g2, seq):
    t, d = x1.shape
    tc = TC_COMB
    nb = seq // tc
    n = t // tc
    pos3 = pos.reshape(n, 1, tc * TOP_K)
    return pl.pallas_call(
        _moe_combine_kernel,
        grid=(n,),
        in_specs=[
            pl.BlockSpec((None, 1, tc * TOP_K), lambda i: (i, 0, 0), memory_space=pltpu.SMEM),
            pl.BlockSpec((None, 1, tc * TOP_K), lambda i: (jnp.minimum(i + 1, n - 1), 0, 0),
                         memory_space=pltpu.SMEM),
            pl.BlockSpec(memory_space=pl.ANY),
            pl.BlockSpec((tc, TOP_K), lambda i: (i, 0)),
            pl.BlockSpec((tc, d), lambda i: (i, 0)),
            pl.BlockSpec((None, 1, d), lambda i: (i // nb, 0, 0)),
        ],
        out_specs=pl.BlockSpec((tc, d), lambda i: (i, 0)),
        out_shape=jax.ShapeDtypeStruct((t, d), F32),
        scratch_shapes=[pltpu.VMEM((2, TOP_K, tc, d), F32), pltpu.SemaphoreType.DMA((2,))],
        compiler_params=_params("arbitrary"),
        name="moe_combine",
    )(pos3, pos3, y, gate, x1, g2)


def _tile_gain(gain, width, scale=1.0):
    return (jnp.tile(gain, width // HEAD_DIM) * scale).reshape(1, width).astype(F32)


def _layer(x2d, mod, b, s, ln1_g, ln2_g, w_in, nsa_q_gain, nsa_k_gain, cmp_pos, cmp_w1, cmp_w2,
           moba_q_gain, moba_k_gain, w_out, router_w, router_b, exp_w1, exp_b1, exp_w2, exp_b2, layer):
    t, d = x2d.shape
    sh1, sc1, g1, sh2, sc2, g2 = [m.reshape(b, 1, d) for m in jnp.split(mod, 6, axis=-1)]
    scale = HEAD_DIM ** -0.5

    (qt, kcv, ks_t, vs_t, kw_t, vw_t, gates_t, mq_t, mk_t, mv_t, kmean) = _inproj(
        x2d, ln1_g.reshape(1, d), 1.0 + sc1, sh1, _pack_w_in(w_in),
        _tile_gain(nsa_q_gain, 512, scale), _tile_gain(nsa_k_gain, 128),
        _tile_gain(moba_q_gain, 512, scale), _tile_gain(moba_k_gain, 512), b, s)

    g = NSA_GROUPS
    xc = kcv.reshape(b, s, 2, g, HEAD_DIM).transpose(0, 2, 3, 1, 4).reshape(
        b, 2, g, s // CMP_STRIDE, CMP_STRIDE * HEAD_DIM)
    cmp = _compress(xc, cmp_pos.reshape(2, 1, CMP_LEN * HEAD_DIM), cmp_w1.astype(BF16),
                    cmp_w2.astype(BF16), nsa_k_gain.reshape(1, HEAD_DIM))
    kc = cmp[:, 0].astype(BF16)
    vct = cmp[:, 1].transpose(0, 1, 3, 2).astype(BF16)
    acc_t, sel_t, tile_hits = _nsa_cmp(qt, kc, vct, gates_t)
    nsel = sel_t.shape[2]
    if nsel < MAX_SEL_BLOCKS:
        sel_t = jnp.concatenate([sel_t, jnp.ones((b, g, MAX_SEL_BLOCKS - nsel, s), F32)], axis=2)
    tile_flags = (tile_hits[..., 0] > 0).astype(I32).reshape(-1)
    o_nsa_t = _nsa_selwin(tile_flags, qt, ks_t, vs_t, kw_t, vw_t, sel_t, gates_t, acc_t)

    km = kmean.reshape(b, s // MOBA_BLOCK, MOBA_HEADS, HEAD_DIM).transpose(0, 2, 1, 3)
    o_moba_t = _moba(mq_t, mk_t, mv_t, km)

    wr = jnp.zeros((d, LANES), F32).at[:, :N_EXPERTS].set(router_w)
    wr_hi = wr.astype(BF16)
    wr_lo = (wr - wr_hi.astype(F32)).astype(BF16)
    rb = jnp.zeros((1, LANES), F32).at[0, :N_EXPERTS].set(router_b)
    x1, h2, logits_t = _outproj(o_nsa_t, o_moba_t, w_out.astype(BF16), x2d, g1, ln2_g.reshape(1, d),
                                1.0 + sc2, sh2, wr_hi, wr_lo, rb)

    idx8, gate8, rank8, counts = _route(logits_t)
    dest, blk_e, nused, fill_start, fill_flag = _dispatch_plan(idx8[:TOP_K], rank8[:TOP_K], counts[:, 0])
    xs = _dispatch(fill_start, fill_flag, nused, dest, h2)
    y = _moe_ffn(blk_e, nused, xs, exp_w1, exp_b1, exp_w2, exp_b2, layer)
    return _moe_combine(dest, y, gate8[:TOP_K].T, x1, g2, s)


def kernel(x, c, ada_w, ada_b, ln1_g, ln2_g, w_in, nsa_q_gain, nsa_k_gain, nsa_cmp_pos, nsa_cmp_w1,
           nsa_cmp_w2, moba_q_gain, moba_k_gain, w_out, router_w, router_b, exp_w1, exp_b1, exp_w2, exp_b2):
    b, s, d = x.shape
    depth = ada_w.shape[0]
    mods = _adaln(c, ada_w, ada_b)
    x2d = x.reshape(b * s, d)
    for l in range(depth):
        x2d = _layer(x2d, mods[l], b, s, ln1_g[l], ln2_g[l], w_in[l], nsa_q_gain[l], nsa_k_gain[l],
                     nsa_cmp_pos[l], nsa_cmp_w1[l], nsa_cmp_w2[l], moba_q_gain[l], moba_k_gain[l],
                     w_out[l], router_w[l], router_b[l], exp_w1, exp_b1, exp_w2, exp_b2, l)
    return x2d.reshape(b, s, d)
```

```python
import functools

import numpy as np
import jax
import jax.numpy as jnp
from jax import lax
from jax.experimental import pallas as pl
from jax.experimental.pallas import tpu as pltpu

F32 = jnp.float32
BF16 = jnp.bfloat16
I32 = jnp.int32

HEAD_DIM = 64
NSA_HEADS = 8
NSA_GROUPS = 2
NSA_HPG = NSA_HEADS // NSA_GROUPS
CMP_LEN = 32
CMP_STRIDE = 16
CMP_HID = 128
SEL_BLOCK = 64
SEL_TOP = 16
WINDOW = 512
MOBA_HEADS = 8
MOBA_BLOCK = 256
MOBA_TOP = 3
N_EXPERTS = 32
TOP_K = 4
SWIGLU_LIMIT = 7.0
SWIGLU_ALPHA = 1.702
EPS = 1e-6
NEG = -1e30
FORCE_SCORE = 1e4

LANES = 128
SUBLANES = 8
TQ = 256
TK = 256
POS_SPLIT = 256
BIG = 1e30
MAX_SEL_BLOCKS = 128
MAX_MOBA_BLOCKS = 32
MOBA_HB = 4
RANK_CHUNK = 32
TM_PROJ = 512
TM_MOE = 512
TC_COMB = 128
TT_ROUTE = 512
TD_DISPATCH = 512
VMEM_LIMIT = 56 * 1024 * 1024


def _dot(a, b):
    return jnp.dot(a, b, preferred_element_type=F32)


def _hi_lo(x):
    hi = x.astype(BF16)
    lo = (x - hi.astype(F32)).astype(BF16)
    return hi, lo


def _sigmoid(x):
    return 1.0 / (1.0 + jnp.exp(-x))


def _params(*sem):
    return pltpu.CompilerParams(dimension_semantics=sem, vmem_limit_bytes=VMEM_LIMIT)


def _adaln_kernel(c_ref, w_ref, b_ref, o_ref):
    c = c_ref[...]
    cond = c * _sigmoid(c)
    ch, cl = _hi_lo(cond)
    wh, wl = _hi_lo(w_ref[...])
    o_ref[...] = _dot(ch, wh) + _dot(ch, wl) + _dot(cl, wh) + b_ref[...]


def _adaln(c, ada_w, ada_b):
    depth, d, n6 = ada_w.shape
    b = c.shape[0]
    tn = 1024
    c_pad = jnp.zeros((SUBLANES, d), F32).at[:b].set(c)
    out = pl.pallas_call(
        _adaln_kernel,
        grid=(depth, n6 // tn),
        in_specs=[
            pl.BlockSpec((SUBLANES, d), lambda l, j: (0, 0)),
            pl.BlockSpec((None, d, tn), lambda l, j: (l, 0, j)),
            pl.BlockSpec((None, 1, tn), lambda l, j: (l, 0, j)),
        ],
        out_specs=pl.BlockSpec((None, SUBLANES, tn), lambda l, j: (l, 0, j)),
        out_shape=jax.ShapeDtypeStruct((depth, SUBLANES, n6), F32),
        compiler_params=_params("arbitrary", "arbitrary"),
        name="adaln",
    )(c_pad, ada_w, ada_b.reshape(depth, 1, n6))
    return out[:, :b]


def _block_diag_ones():
    r = lax.broadcasted_iota(I32, (LANES, LANES), 0) // HEAD_DIM
    c = lax.broadcasted_iota(I32, (LANES, LANES), 1) // HEAD_DIM
    return jnp.where(r == c, 1.0, 0.0).astype(BF16)


def _head_norm(z, bd, gain):
    cols = []
    for c0 in range(0, z.shape[1], LANES):
        zc = z[:, c0:c0 + LANES]
        hi, lo = _hi_lo(zc * zc)
        ss = _dot(hi, bd) + _dot(lo, bd)
        cols.append(zc * lax.rsqrt(ss * (1.0 / HEAD_DIM) + EPS))
    zn = cols[0] if len(cols) == 1 else jnp.concatenate(cols, axis=1)
    return zn * gain


def _modulated_norm(x, g, sc1p, sh):
    ms = jnp.mean(x * x, axis=-1, keepdims=True)
    y = x * lax.rsqrt(ms + EPS)
    return (y * g) * sc1p + sh


def _heads_transposed(z, nh):
    return z.T.astype(BF16).reshape(nh, HEAD_DIM, z.shape[0])


def _store_value_tiles(o_ref, z, nh):
    zt = z.T.astype(BF16)
    for h in range(nh):
        for t in range(z.shape[0] // TK):
            o_ref[h, t] = zt[h * HEAD_DIM:(h + 1) * HEAD_DIM, t * TK:(t + 1) * TK]


def _store_key_tiles(o_ref, z, extra, nh):
    rows = z.shape[0]
    for h in range(nh):
        ka = jnp.concatenate([z[:, h * HEAD_DIM:(h + 1) * HEAD_DIM], extra], axis=1).astype(BF16)
        o_ref[h] = ka.reshape(rows // TK, TK, ka.shape[1])


def _pos_cols(pos, width):
    col = lax.broadcasted_iota(I32, (pos.shape[0], width), 1)
    hi = lax.shift_right_logical(pos, POS_SPLIT.bit_length() - 1).astype(F32)
    lo = (pos & (POS_SPLIT - 1)).astype(F32)
    return jnp.where(col == 0, hi, jnp.where(col == 1, lo, jnp.where(col == 2, 1.0, 0.0)))


def _onehot_cols(pos, block, width):
    col = lax.broadcasted_iota(I32, (pos.shape[0], width), 1)
    return jnp.where(lax.shift_right_logical(pos, block.bit_length() - 1) == col, 1.0, 0.0)


def _inproj_kernel(x_ref, g_ref, sc_ref, sh_ref, w_ref, gq_ref, gk_ref, gmq_ref, gmk_ref,
                   q_ref, kcv_ref, ks_ref, vs_ref, kw_ref, vw_ref, gate_ref,
                   mq_ref, mk_ref, mv_ref, kmean_ref):
    h = _modulated_norm(x_ref[...], g_ref[...], sc_ref[...], sh_ref[...])
    hb = h.astype(BF16)
    bd = _block_diag_ones()
    rows = hb.shape[0]
    pos = pl.program_id(1) * rows + lax.broadcasted_iota(I32, (rows, 1), 0)

    def proj(a, b):
        return _dot(hb, w_ref[:, a:b])

    q_ref[...] = _heads_transposed(_head_norm(proj(0, 512), bd, gq_ref[...]), NSA_HEADS)
    kv = proj(512, 1280)
    kcv_ref[...] = kv[:, 0:256]
    pos_cols = _pos_cols(pos, HEAD_DIM)
    sel_cols = jnp.concatenate([pos_cols, _onehot_cols(pos, SEL_BLOCK, MAX_SEL_BLOCKS)], axis=1)
    _store_key_tiles(ks_ref, _head_norm(kv[:, 256:384], bd, gk_ref[...]), sel_cols, NSA_GROUPS)
    _store_value_tiles(vs_ref, kv[:, 384:512], NSA_GROUPS)
    _store_key_tiles(kw_ref, _head_norm(kv[:, 512:640], bd, gk_ref[...]), pos_cols, NSA_GROUPS)
    _store_value_tiles(vw_ref, kv[:, 640:768], NSA_GROUPS)
    gate_ref[...] = _sigmoid(proj(1280, 1408)).T
    mq_ref[...] = _heads_transposed(_head_norm(proj(1408, 1920), bd, gmq_ref[...]), MOBA_HEADS)
    mkn = _head_norm(proj(1920, 2432), bd, gmk_ref[...])
    moba_cols = jnp.concatenate([_pos_cols(pos, MAX_MOBA_BLOCKS),
                                 _onehot_cols(pos, MOBA_BLOCK, MAX_MOBA_BLOCKS)], axis=1)
    _store_key_tiles(mk_ref, mkn, moba_cols, MOBA_HEADS)
    _store_value_tiles(mv_ref, proj(2432, 2944), MOBA_HEADS)
    means = [jnp.sum(mkn[r:r + MOBA_BLOCK], axis=0, keepdims=True) * (1.0 / MOBA_BLOCK)
             for r in range(0, rows, MOBA_BLOCK)]
    kmean_ref[...] = jnp.concatenate(means, axis=0)


def _pack_w_in(w_in):
    d = w_in.shape[0]
    wg = jnp.zeros((d, LANES), w_in.dtype).at[:, :3 * NSA_HEADS].set(w_in[:, 1280:1304])
    return jnp.concatenate([w_in[:, :1280], wg, w_in[:, 1304:]], axis=1).astype(BF16)


def _inproj(x2d, ln_g, sc1p, sh, w_pack, gq, gk, gmq, gmk, b, seq):
    t, d = x2d.shape
    tm = TM_PROJ
    nb = seq // tm
    nt = seq // TK
    tpt = tm // TK
    g = NSA_GROUPS
    row = lambda i, j: (i * nb + j, 0)
    bat = lambda i, j: (i, 0, 0)
    const = lambda i, j: (0, 0)
    heads_t = lambda nh: pl.BlockSpec((None, nh, HEAD_DIM, tm), lambda i, j: (i, 0, 0, j))
    key_tiles = lambda nh, w: pl.BlockSpec((None, nh, tpt, TK, w), lambda i, j: (i, 0, j, 0, 0))
    val_tiles = lambda nh: pl.BlockSpec((None, nh, tpt, HEAD_DIM, TK), lambda i, j: (i, 0, j, 0, 0))
    sel_w = 2 * HEAD_DIM + MAX_SEL_BLOCKS
    win_w = 2 * HEAD_DIM
    moba_w = HEAD_DIM + 2 * MAX_MOBA_BLOCKS
    nmean = tm // MOBA_BLOCK
    sds = jax.ShapeDtypeStruct
    out_shape = [
        sds((b, NSA_HEADS, HEAD_DIM, seq), BF16), sds((t, 256), F32),
        sds((b, g, nt, TK, sel_w), BF16), sds((b, g, nt, HEAD_DIM, TK), BF16),
        sds((b, g, nt, TK, win_w), BF16), sds((b, g, nt, HEAD_DIM, TK), BF16),
        sds((b, LANES, seq), F32),
        sds((b, MOBA_HEADS, HEAD_DIM, seq), BF16),
        sds((b, MOBA_HEADS, nt, TK, moba_w), BF16), sds((b, MOBA_HEADS, nt, HEAD_DIM, TK), BF16),
        sds((t // tm, nmean, 512), F32),
    ]
    out_specs = [
        heads_t(NSA_HEADS), pl.BlockSpec((tm, 256), row),
        key_tiles(g, sel_w), val_tiles(g), key_tiles(g, win_w), val_tiles(g),
        pl.BlockSpec((None, LANES, tm), lambda i, j: (i, 0, j)),
        heads_t(MOBA_HEADS), key_tiles(MOBA_HEADS, moba_w), val_tiles(MOBA_HEADS),
        pl.BlockSpec((None, nmean, 512), lambda i, j: (i * nb + j, 0, 0)),
    ]
    return pl.pallas_call(
        _inproj_kernel,
        grid=(b, nb),
        in_specs=[
            pl.BlockSpec((tm, d), row),
            pl.BlockSpec((1, d), const),
            pl.BlockSpec((None, 1, d), bat),
            pl.BlockSpec((None, 1, d), bat),
            pl.BlockSpec(w_pack.shape, const),
            pl.BlockSpec((1, 512), const),
            pl.BlockSpec((1, 128), const),
            pl.BlockSpec((1, 512), const),
            pl.BlockSpec((1, 512), const),
        ],
        out_specs=out_specs,
        out_shape=out_shape,
        compiler_params=_params("arbitrary", "arbitrary"),
        name="inproj",
    )(x2d, ln_g, sc1p, sh, w_pack, gq, gk, gmq, gmk)


def _compress_kernel(x_ref, pos_ref, w1_ref, w2_ref, gk_ref, o_ref):
    kv = pl.program_id(1)
    half = CMP_STRIDE * HEAD_DIM
    x = x_ref[...]
    xl = (x + pos_ref[:, :half]).astype(BF16)
    xh = (x + pos_ref[:, half:]).astype(BF16)
    a = _dot(xl, w1_ref[:half, :])
    b = _dot(xh, w1_ref[half:, :])
    hid = a + pltpu.roll(b, b.shape[0] - 1, 0)
    out = _dot(jax.nn.gelu(hid, approximate=True).astype(BF16), w2_ref[...])
    ms = jnp.mean(out * out, axis=-1, keepdims=True)
    normed = (out * lax.rsqrt(ms + EPS)) * gk_ref[...]
    o_ref[...] = jnp.where(kv == 0, normed, out)


def _compress(xc, pos, w1, w2, gk):
    b, _, g, nr, w = xc.shape
    return pl.pallas_call(
        _compress_kernel,
        grid=(b, 2, g),
        in_specs=[
            pl.BlockSpec((None, None, None, nr, w), lambda i, k, j: (i, k, j, 0, 0)),
            pl.BlockSpec((None, 1, w * 2), lambda i, k, j: (k, 0, 0)),
            pl.BlockSpec((None, w * 2, CMP_HID), lambda i, k, j: (k, 0, 0)),
            pl.BlockSpec((None, CMP_HID, HEAD_DIM), lambda i, k, j: (k, 0, 0)),
            pl.BlockSpec((1, HEAD_DIM), lambda i, k, j: (0, 0)),
        ],
        out_specs=pl.BlockSpec((None, None, None, nr, HEAD_DIM), lambda i, k, j: (i, k, j, 0, 0)),
        out_shape=jax.ShapeDtypeStruct((b, 2, g, nr, HEAD_DIM), F32),
        compiler_params=_params("arbitrary", "arbitrary", "arbitrary"),
        name="nsa_compress",
    )(xc, pos, w1, w2, gk)


def _nsa_slope(g, h):
    return 2.0 ** (-(g * NSA_HPG + h + 1))


def _rank_counts(imp_ref, cnt_ref, last_blk):
    nb = imp_ref.shape[0]
    chunk = min(RANK_CHUNK, nb)
    cnt_ref[...] = jnp.zeros(cnt_ref.shape, cnt_ref.dtype)
    sub = lax.broadcasted_iota(I32, (SUBLANES, imp_ref.shape[1]), 0)
    for c in range(nb // chunk):
        for d in range(nb // chunk):
            @pl.when(max(c, d) * chunk <= last_blk)
            def _():
                rows0 = range(d * chunk, (d + 1) * chunk, SUBLANES)
                pieces = [imp_ref[r:r + SUBLANES] for r in rows0]
                counts = [jnp.zeros(p.shape, I32) for p in pieces]
                for jp in range(c * chunk, (c + 1) * chunk):
                    row = imp_ref[jp:jp + 1]
                    for k, r0 in enumerate(rows0):
                        p = pieces[k]
                        if r0 + SUBLANES - 1 < jp:
                            inc = jnp.where(row > p, 1, 0)
                        elif r0 > jp:
                            inc = jnp.where(row >= p, 1, 0)
                        else:
                            inc = jnp.where(sub > jp - r0, jnp.where(row >= p, 1, 0), jnp.where(row > p, 1, 0))
                        counts[k] = counts[k] + inc
                rows = slice(d * chunk, (d + 1) * chunk)
                cnt_ref[rows] = cnt_ref[rows] + jnp.concatenate(counts, axis=0)


def _nsa_cmp_kernel(q_ref, kc_ref, vct_ref, inct_ref, gate_ref, acc_ref, sel_ref, flag_ref,
                    imp_ref, cnt_ref, *, n_top):
    g = pl.program_id(1)
    q0 = pl.program_id(2) * TQ
    nc = kc_ref.shape[0]
    nsel = inct_ref.shape[0]
    t = q0 + lax.broadcasted_iota(I32, (1, TQ), 1)
    width = NSA_HPG * TQ
    tw = q0 + (lax.broadcasted_iota(I32, (1, width), 1) & (TQ - 1))
    n = lax.broadcasted_iota(I32, (nc, 1), 0)
    center = (CMP_STRIDE * n).astype(F32) + 0.5 * (CMP_LEN - 1)
    dist = tw.astype(F32) - center
    mask = (CMP_STRIDE * n + (CMP_LEN - 1)) <= tw
    slopes = jnp.concatenate(
        [jnp.full((1, TQ), 1.0, F32) * jnp.where(g == 0, _nsa_slope(0, h), _nsa_slope(1, h))
         for h in range(NSA_HPG)], axis=1)
    qw = jnp.concatenate([q_ref[h] for h in range(NSA_HPG)], axis=1)
    s = _dot(kc_ref[...], qw) - slopes * dist
    s = jnp.where(mask, s, NEG)
    m = jnp.max(s, axis=0, keepdims=True)
    e = jnp.exp(s - m)
    l = jnp.sum(e, axis=0, keepdims=True)
    p = jnp.where(mask, e, 0.0) * (1.0 / l)
    o = _dot(vct_ref[...], p.astype(BF16))
    psum = jnp.zeros((nc, TQ), F32)
    for h in range(NSA_HPG):
        lanes = slice(h * TQ, (h + 1) * TQ)
        gate = gate_ref[pl.ds(g * (3 * NSA_HPG) + 3 * h, 1), :]
        acc_ref[h] = gate * o[:, lanes]
        psum = psum + p[:, lanes]
    hi, lo = _hi_lo(psum)
    inct = inct_ref[...]
    imp = _dot(inct, hi) + _dot(inct, lo)
    blk = lax.broadcasted_iota(I32, (nsel, 1), 0)
    cur = t // SEL_BLOCK
    forced = (blk == 0) | (blk == cur) | (blk == cur - 1)
    imp = jnp.where(forced, FORCE_SCORE, imp)
    imp_ref[...] = jnp.where(blk <= cur, imp, NEG)
    _rank_counts(imp_ref, cnt_ref, (q0 + TQ - 1) // SEL_BLOCK)
    chosen = (cnt_ref[...] < n_top) & (blk <= cur)
    sel = jnp.where(chosen, 1.0, 0.0)
    sel_ref[...] = sel
    bpt = TK // SEL_BLOCK
    ntile = nsel // bpt
    trow = lax.broadcasted_iota(I32, (ntile, nsel), 0)
    tcol = lax.broadcasted_iota(I32, (ntile, nsel), 1)
    member = jnp.where(lax.shift_right_logical(tcol, bpt.bit_length() - 1) == trow, 1.0, 0.0).astype(BF16)
    hits = jnp.sum(_dot(member, sel.astype(BF16)), axis=1, keepdims=True)
    flag_ref[...] = jnp.broadcast_to(hits, (ntile, LANES))


def _incidence_t(nc, nsel):
    cmp_start = CMP_STRIDE * np.arange(nc)
    sel_start = SEL_BLOCK * np.arange(nsel)
    inc = ((cmp_start[:, None] <= sel_start[None, :] + SEL_BLOCK - 1)
           & (cmp_start[:, None] + CMP_LEN - 1 >= sel_start[None, :]))
    inc[nc - 1] = False
    return jnp.asarray(inc.T, dtype=BF16)


def _nsa_cmp(qt, kc, vct, gates_t):
    b, _, _, s = qt.shape
    nc = kc.shape[2]
    nsel = s // SEL_BLOCK
    n_top = min(SEL_TOP, nsel)
    ntile = s // TK
    inct = _incidence_t(nc, nsel)
    return pl.pallas_call(
        functools.partial(_nsa_cmp_kernel, n_top=n_top),
        grid=(b, NSA_GROUPS, s // TQ),
        in_specs=[
            pl.BlockSpec((None, NSA_HPG, HEAD_DIM, TQ), lambda i, g, j: (i, g, 0, j)),
            pl.BlockSpec((None, None, nc, HEAD_DIM), lambda i, g, j: (i, g, 0, 0)),
            pl.BlockSpec((None, None, HEAD_DIM, nc), lambda i, g, j: (i, g, 0, 0)),
            pl.BlockSpec((nsel, nc), lambda i, g, j: (0, 0)),
            pl.BlockSpec((None, LANES, TQ), lambda i, g, j: (i, 0, j)),
        ],
        out_specs=[
            pl.BlockSpec((None, NSA_HPG, HEAD_DIM, TQ), lambda i, g, j: (i, g, 0, j)),
            pl.BlockSpec((None, None, nsel, TQ), lambda i, g, j: (i, g, 0, j)),
            pl.BlockSpec((None, None, None, ntile, LANES), lambda i, g, j: (i, g, j, 0, 0)),
        ],
        out_shape=[
            jax.ShapeDtypeStruct((b, NSA_HEADS, HEAD_DIM, s), F32),
            jax.ShapeDtypeStruct((b, NSA_GROUPS, nsel, s), F32),
            jax.ShapeDtypeStruct((b, NSA_GROUPS, s // TQ, ntile, LANES), F32),
        ],
        scratch_shapes=[pltpu.VMEM((nsel, TQ), F32), pltpu.VMEM((nsel, TQ), I32)],
        compiler_params=_params("arbitrary", "arbitrary", "arbitrary"),
        name="nsa_cmp_select",
    )(qt, kc, vct, inct, gates_t)


def _alibi_rows(slope, q0, rows):
    r = lax.broadcasted_iota(I32, (rows, TQ), 0)
    return jnp.where(r == 0, slope * POS_SPLIT, jnp.where(r == 1, slope, jnp.where(r == 2, -slope * q0, 0.0)))


def _wide_dist(width):
    lane = lax.broadcasted_iota(I32, (TK, width), 1) & (TQ - 1)
    return lane - lax.broadcasted_iota(I32, (TK, width), 0)


class _FlashPipe:
    def __init__(self, scores, values, m_ref, l_ref, acc_ref, s_ref, p_ref, alpha_ref, tile_of=lambda j: j):
        self.scores, self.values, self.tile_of = scores, values, tile_of
        self.m_ref, self.l_ref, self.acc_ref = m_ref, l_ref, acc_ref
        self.s_ref, self.p_ref, self.alpha_ref = s_ref, p_ref, alpha_ref

    def run(self, first_scores, first_tile, n_past):
        m = jnp.max(first_scores, axis=0, keepdims=True)
        p = jnp.exp(first_scores - m)
        self.m_ref[...] = m
        self.l_ref[...] = jnp.sum(p, axis=0, keepdims=True)
        self.p_ref[...] = p.astype(BF16)
        self.alpha_ref[...] = jnp.ones(self.alpha_ref.shape, F32)
        self.acc_ref[...] = jnp.zeros(self.acc_ref.shape, F32)

        @pl.when(n_past > 0)
        def _():
            self.s_ref[...] = self.scores(self.tile_of(0))

        def step(j, carry):
            pending = jnp.where(j == 0, first_tile, self.tile_of(jnp.maximum(j - 1, 0)))
            s = self.s_ref[...]
            self.s_ref[...] = self.scores(self.tile_of(jnp.minimum(j + 1, n_past - 1)))
            pv = self.values(pending, self.p_ref[...])
            self.acc_ref[...] = self.alpha_ref[...] * self.acc_ref[...] + pv
            m_old = self.m_ref[...]
            m_new = jnp.maximum(m_old, jnp.max(s, axis=0, keepdims=True))
            alpha = jnp.exp(m_old - m_new)
            p = jnp.exp(s - m_new)
            self.m_ref[...] = m_new
            self.l_ref[...] = alpha * self.l_ref[...] + jnp.sum(p, axis=0, keepdims=True)
            self.alpha_ref[...] = alpha
            self.p_ref[...] = p.astype(BF16)
            return carry

        lax.fori_loop(0, n_past, step, 0)
        pending = jnp.where(n_past == 0, first_tile, self.tile_of(jnp.maximum(n_past - 1, 0)))
        acc = self.alpha_ref[...] * self.acc_ref[...] + self.values(pending, self.p_ref[...])
        return acc / self.l_ref[...]


def _nsa_selwin_kernel(flags_ref, q_ref, ks_ref, vst_ref, kw_ref, vwt_ref, sel_ref, gate_ref, accin_ref,
                       o_ref, qa_ref, ms_ref, ls_ref, as_ref, s_ref, p_ref, alpha_ref, visit_ref):
    g = pl.program_id(1)
    i = pl.program_id(2)
    ntile = ks_ref.shape[0]
    base = ((pl.program_id(0) * NSA_GROUPS + g) * pl.num_programs(2) + i) * ntile

    def note(jt, n):
        visit_ref[n] = jt
        return n + flags_ref[base + jt]

    n_visit = lax.fori_loop(0, i, note, 0)
    q0 = (i * TQ).astype(F32)
    width = NSA_HPG * TQ
    selbias = ((sel_ref[...] - 1.0) * BIG).astype(BF16)
    for h in range(NSA_HPG):
        slope = jnp.where(g == 0, _nsa_slope(0, h), _nsa_slope(1, h))
        lanes = slice(h * TQ, (h + 1) * TQ)
        qa_ref[0:HEAD_DIM, lanes] = q_ref[h]
        qa_ref[HEAD_DIM:2 * HEAD_DIM, lanes] = _alibi_rows(slope, q0, HEAD_DIM).astype(BF16)
        qa_ref[2 * HEAD_DIM:, lanes] = selbias
    qa = qa_ref[...]
    qw = qa_ref[0:2 * HEAD_DIM, :]
    dist = _wide_dist(width)

    causal = dist >= 0
    tiles = [i]
    scores = [jnp.where(causal, _dot(kw_ref[i], qw), NEG)]
    for d in range(1, WINDOW // TK + 1):
        tiles.append(jnp.maximum(i - d, 0))
        limit = jnp.where(i >= d, WINDOW, -(2 ** 30))
        scores.append(jnp.where(dist + d * TK < limit, _dot(kw_ref[tiles[d]], qw), NEG))
    m_win = jnp.max(scores[0], axis=0, keepdims=True)
    for s in scores[1:]:
        m_win = jnp.maximum(m_win, jnp.max(s, axis=0, keepdims=True))
    l_win = jnp.zeros((1, width), F32)
    a_win = jnp.zeros((HEAD_DIM, width), F32)
    for jt, s in zip(tiles, scores):
        p = jnp.exp(s - m_win)
        l_win = l_win + jnp.sum(p, axis=0, keepdims=True)
        a_win = a_win + _dot(vwt_ref[jt], p.astype(BF16))
    o_win = a_win / l_win

    pipe = _FlashPipe(lambda j: _dot(ks_ref[j], qa_ref[...]), lambda j, p: _dot(vst_ref[j], p),
                      ms_ref, ls_ref, as_ref, s_ref, p_ref, alpha_ref, tile_of=lambda j: visit_ref[j])
    o_sel = pipe.run(jnp.where(causal, _dot(ks_ref[i], qa), NEG), i, n_visit)
    for h in range(NSA_HPG):
        lanes = slice(h * TQ, (h + 1) * TQ)
        row = g * (3 * NSA_HPG) + 3 * h
        g_sel = gate_ref[pl.ds(row + 1, 1), :]
        g_win = gate_ref[pl.ds(row + 2, 1), :]
        o_ref[h] = (accin_ref[h] + g_sel * o_sel[:, lanes] + g_win * o_win[:, lanes]).astype(o_ref.dtype)


def _nsa_selwin(tile_flags, qt, ks_t, vs_t, kw_t, vw_t, sel_t, gates_t, acc_t):
    b, _, _, s = qt.shape
    nt = s // TK
    width = NSA_HPG * TQ
    ksw = ks_t.shape[-1]
    kww = kw_t.shape[-1]
    vspec = pl.BlockSpec((None, None, nt, HEAD_DIM, TK), lambda i, g, j, fl: (i, g, 0, 0, 0))
    qspec = pl.BlockSpec((None, NSA_HPG, HEAD_DIM, TQ), lambda i, g, j, fl: (i, g, 0, j))
    stat = pltpu.VMEM((1, width), F32)
    accs = pltpu.VMEM((HEAD_DIM, width), F32)
    grid_spec = pltpu.PrefetchScalarGridSpec(
        num_scalar_prefetch=1,
        grid=(b, NSA_GROUPS, s // TQ),
        in_specs=[
            qspec,
            pl.BlockSpec((None, None, nt, TK, ksw), lambda i, g, j, fl: (i, g, 0, 0, 0)), vspec,
            pl.BlockSpec((None, None, nt, TK, kww), lambda i, g, j, fl: (i, g, 0, 0, 0)), vspec,
            pl.BlockSpec((None, None, MAX_SEL_BLOCKS, TQ), lambda i, g, j, fl: (i, g, 0, j)),
            pl.BlockSpec((None, LANES, TQ), lambda i, g, j, fl: (i, 0, j)),
            qspec,
        ],
        out_specs=qspec,
        scratch_shapes=[pltpu.VMEM((ksw, width), BF16), stat, stat, accs,
                        pltpu.VMEM((TK, width), F32), pltpu.VMEM((TK, width), BF16), stat,
                        pltpu.SMEM((nt,), I32)],
    )
    return pl.pallas_call(
        _nsa_selwin_kernel,
        grid_spec=grid_spec,
        out_shape=jax.ShapeDtypeStruct((b, NSA_HEADS, HEAD_DIM, s), BF16),
        compiler_params=_params("arbitrary", "arbitrary", "arbitrary"),
        name="nsa_select_window",
    )(tile_flags, qt, ks_t, vs_t, kw_t, vw_t, sel_t, gates_t, acc_t)


def _moba_kernel(q_ref, k_ref, vt_ref, kmean_ref, o_ref, qa_ref, m_ref, l_ref, acc_ref,
                 s_ref, p_ref, alpha_ref, *, n_top):
    hg = pl.program_id(1)
    i = pl.program_id(2)
    nblk = kmean_ref.shape[1]
    q0 = (i * TQ).astype(F32)
    blk = lax.broadcasted_iota(I32, (nblk, 1), 0)
    for h in range(MOBA_HB):
        head = hg * MOBA_HB + h
        slope = jnp.float32(2.0 ** -MOBA_HEADS)
        for hh in range(MOBA_HEADS - 1):
            slope = jnp.where(head == hh, 2.0 ** -(hh + 1), slope)
        q = q_ref[h]
        gate = jnp.where(blk < i, _dot(kmean_ref[h].astype(BF16), q), NEG)
        bias = jnp.full(gate.shape, -BIG, F32)
        for _ in range(n_top):
            mx = jnp.max(gate, axis=0, keepdims=True)
            first = jnp.min(jnp.where(gate == mx, blk, nblk), axis=0, keepdims=True)
            pick = blk == first
            bias = jnp.where(pick, 0.0, bias)
            gate = jnp.where(pick, -jnp.inf, gate)
        bias = jnp.where(blk < i, bias, 0.0)
        if nblk < MAX_MOBA_BLOCKS:
            bias = jnp.concatenate([bias, jnp.zeros((MAX_MOBA_BLOCKS - nblk, TQ), F32)], axis=0)
        qa_ref[h, 0:HEAD_DIM, :] = q
        qa_ref[h, HEAD_DIM:HEAD_DIM + MAX_MOBA_BLOCKS, :] = _alibi_rows(slope, q0, MAX_MOBA_BLOCKS).astype(BF16)
        qa_ref[h, HEAD_DIM + MAX_MOBA_BLOCKS:, :] = bias.astype(BF16)

    def scores(j):
        return jnp.concatenate([_dot(k_ref[h, j], qa_ref[h]) for h in range(MOBA_HB)], axis=1)

    def values(j, p):
        return jnp.concatenate(
            [_dot(vt_ref[h, j], p[:, h * TQ:(h + 1) * TQ]) for h in range(MOBA_HB)], axis=1)

    pipe = _FlashPipe(scores, values, m_ref, l_ref, acc_ref, s_ref, p_ref, alpha_ref)
    o = pipe.run(jnp.where(_wide_dist(MOBA_HB * TQ) >= 0, scores(i), NEG), i, i)
    for h in range(MOBA_HB):
        o_ref[h] = o[:, h * TQ:(h + 1) * TQ].astype(o_ref.dtype)


def _moba(qt, k_t, v_t, kmean):
    b, nh, _, s = qt.shape
    nt = s // TK
    n_top = min(MOBA_TOP, nt - 1)
    hb = MOBA_HB
    kw = k_t.shape[-1]
    width = hb * TQ
    return pl.pallas_call(
        functools.partial(_moba_kernel, n_top=n_top),
        grid=(b, nh // hb, s // TQ),
        in_specs=[
            pl.BlockSpec((None, hb, HEAD_DIM, TQ), lambda i, h, j: (i, h, 0, j)),
            pl.BlockSpec((None, hb, nt, TK, kw), lambda i, h, j: (i, h, 0, 0, 0)),
            pl.BlockSpec((None, hb, nt, HEAD_DIM, TK), lambda i, h, j: (i, h, 0, 0, 0)),
            pl.BlockSpec((None, hb, nt, HEAD_DIM), lambda i, h, j: (i, h, 0, 0)),
        ],
        out_specs=pl.BlockSpec((None, hb, HEAD_DIM, TQ), lambda i, h, j: (i, h, 0, j)),
        out_shape=jax.ShapeDtypeStruct((b, nh, HEAD_DIM, s), BF16),
        scratch_shapes=[
            pltpu.VMEM((hb, kw, TQ), BF16), pltpu.VMEM((1, width), F32), pltpu.VMEM((1, width), F32),
            pltpu.VMEM((HEAD_DIM, width), F32),
            pltpu.VMEM((TK, width), F32), pltpu.VMEM((TK, width), BF16), pltpu.VMEM((1, width), F32),
        ],
        compiler_params=_params("arbitrary", "arbitrary", "arbitrary"),
        name="moba",
    )(qt, k_t, v_t, kmean)


def _outproj_kernel(on_ref, om_ref, w_ref, x_ref, g1_ref, ln_ref, sc_ref, sh_ref,
                    wrh_ref, wrl_ref, rb_ref, x1_ref, h2_ref, lg_ref):
    nh, dh, rows = on_ref.shape
    half = nh * dh

    def rows_major(o_ref):
        return o_ref[...].reshape(half, rows).astype(F32).T.astype(BF16)

    y = _dot(rows_major(on_ref), w_ref[:half, :]) + _dot(rows_major(om_ref), w_ref[half:, :])
    x1 = x_ref[...] + g1_ref[...] * y
    x1_ref[...] = x1
    h2 = _modulated_norm(x1, ln_ref[...], sc_ref[...], sh_ref[...])
    h2_ref[...] = h2
    hh, hl = _hi_lo(h2)
    logits = _dot(hh, wrh_ref[...]) + _dot(hh, wrl_ref[...]) + _dot(hl, wrh_ref[...]) + rb_ref[...]
    lg_ref[...] = logits.T


def _outproj(o_nsa_t, o_moba_t, w_out, x2d, g1, ln_g, sc2p, sh2, wr_hi, wr_lo, rb):
    t, d = x2d.shape
    b, nh, dh, seq = o_nsa_t.shape
    tm = TM_PROJ
    nb = seq // tm
    row = lambda i, j: (i * nb + j, 0)
    bat = lambda i, j: (i, 0, 0)
    const = lambda i, j: (0, 0)
    heads = pl.BlockSpec((None, nh, dh, tm), lambda i, j: (i, 0, 0, j))
    return pl.pallas_call(
        _outproj_kernel,
        grid=(b, nb),
        in_specs=[
            heads, heads,
            pl.BlockSpec(w_out.shape, const),
            pl.BlockSpec((tm, d), row),
            pl.BlockSpec((None, 1, d), bat),
            pl.BlockSpec((1, d), const),
            pl.BlockSpec((None, 1, d), bat), pl.BlockSpec((None, 1, d), bat),
            pl.BlockSpec(wr_hi.shape, const), pl.BlockSpec(wr_lo.shape, const),
            pl.BlockSpec((1, LANES), const),
        ],
        out_specs=[pl.BlockSpec((tm, d), row), pl.BlockSpec((tm, d), row),
                   pl.BlockSpec((LANES, tm), lambda i, j: (0, i * nb + j))],
        out_shape=[jax.ShapeDtypeStruct((t, d), F32), jax.ShapeDtypeStruct((t, d), F32),
                   jax.ShapeDtypeStruct((LANES, t), F32)],
        compiler_params=_params("arbitrary", "arbitrary"),
        name="outproj_router",
    )(o_nsa_t, o_moba_t, w_out, x2d, g1, ln_g, sc2p, sh2, wr_hi, wr_lo, rb)


def _route_kernel(lg_ref, idx_ref, gate_ref, rank_ref, cnt_ref, carry_ref):
    @pl.when(pl.program_id(0) == 0)
    def _():
        carry_ref[...] = jnp.zeros(carry_ref.shape, carry_ref.dtype)

    lg = lg_ref[...]
    ne, tt = lg.shape
    eidx = lax.broadcasted_iota(I32, lg.shape, 0)
    vals, idxs, hots = [], [], []
    for _ in range(TOP_K):
        mx = jnp.max(lg, axis=0, keepdims=True)
        first = jnp.min(jnp.where(lg == mx, eidx, ne), axis=0, keepdims=True)
        hot = eidx == first
        vals.append(mx)
        idxs.append(first)
        hots.append(hot)
        lg = jnp.where(hot, -jnp.inf, lg)
    e = [jnp.exp(v - vals[0]) for v in vals]
    tot = e[0]
    for x in e[1:]:
        tot = tot + x
    cnt = jnp.where(hots[0], 1.0, 0.0)
    for hot in hots[1:]:
        cnt = cnt + jnp.where(hot, 1.0, 0.0)
    cntb = cnt.astype(BF16)
    r = lax.broadcasted_iota(I32, (tt, tt), 0)
    c = lax.broadcasted_iota(I32, (tt, tt), 1)
    before = jnp.where(r < c, 1.0, 0.0).astype(BF16)
    carry = carry_ref[...]
    prefix = _dot(cntb, before) + carry
    ranks = [jnp.sum(jnp.where(hot, prefix, 0.0), axis=0, keepdims=True) for hot in hots]
    carry = carry + _dot(cntb, jnp.ones((tt, tt), BF16))
    carry_ref[...] = carry
    cnt_ref[...] = carry[:, :LANES]
    pad_i = jnp.zeros((SUBLANES - TOP_K, tt), I32)
    pad_f = jnp.zeros((SUBLANES - TOP_K, tt), F32)
    idx_ref[...] = jnp.concatenate(idxs + [pad_i], axis=0)
    gate_ref[...] = jnp.concatenate([x / tot for x in e] + [pad_f], axis=0)
    rank_ref[...] = jnp.concatenate([x.astype(I32) for x in ranks] + [pad_i], axis=0)


def _route(logits_t):
    ne, t = N_EXPERTS, logits_t.shape[1]
    tt = min(TT_ROUTE, t)
    tok = pl.BlockSpec((SUBLANES, tt), lambda i: (0, i))
    return pl.pallas_call(
        _route_kernel,
        grid=(t // tt,),
        in_specs=[pl.BlockSpec((ne, tt), lambda i: (0, i))],
        out_specs=[tok, tok, tok, pl.BlockSpec((ne, LANES), lambda i: (0, 0))],
        out_shape=[jax.ShapeDtypeStruct((SUBLANES, t), I32), jax.ShapeDtypeStruct((SUBLANES, t), F32),
                   jax.ShapeDtypeStruct((SUBLANES, t), I32), jax.ShapeDtypeStruct((ne, LANES), F32)],
        scratch_shapes=[pltpu.VMEM((ne, tt), F32)],
        compiler_params=_params("arbitrary"),
        name="route_topk_rank",
    )(logits_t)


def _dispatch_plan(idx, rank, counts):
    tm = TM_MOE
    t = idx.shape[1]
    nb = (t * TOP_K + N_EXPERTS * tm) // tm
    experts = jnp.arange(N_EXPERTS, dtype=I32)
    counts = counts.astype(I32)
    padded = (counts + tm - 1) // tm * tm
    pend = jnp.sum(jnp.where(experts[None, :] <= experts[:, None], padded[None, :], 0), axis=1)
    pstart = pend - padded
    base = jnp.sum(jnp.where(idx[:, :, None] == experts, pstart, 0), axis=-1)
    dest = (rank + base).T.reshape(t * TOP_K).astype(I32)
    blk_e = jnp.sum(jnp.where(pend[None, :] <= (jnp.arange(nb, dtype=I32) * tm)[:, None], 1, 0), axis=1)
    blk_e = jnp.minimum(blk_e, N_EXPERTS - 1).astype(I32)
    nused = (pend[-1] // tm).astype(I32).reshape(1)
    fill_start = jnp.maximum(pend - tm, 0).astype(I32)
    fill_flag = (padded > counts).astype(I32)
    return dest, blk_e, nused, fill_start, fill_flag


def _dispatch_kernel(fill_start_ref, fill_flag_ref, nused_ref, dest_ref, h_ref, xs_hbm, zbuf, zsem, sem):
    i = pl.program_id(0)
    nrow = dest_ref.shape[1]
    ntok = nrow // TOP_K

    @pl.when(i == 0)
    def _():
        zbuf[...] = jnp.zeros(zbuf.shape, zbuf.dtype)

        def fill(start):
            return pltpu.make_async_copy(zbuf, xs_hbm.at[pl.ds(pl.multiple_of(start, TM_MOE), TM_MOE)], zsem)

        for e in range(N_EXPERTS):
            @pl.when(fill_flag_ref[e] != 0)
            def _():
                fill(fill_start_ref[e]).start()
        for e in range(N_EXPERTS):
            @pl.when(fill_flag_ref[e] != 0)
            def _():
                fill(fill_start_ref[e]).wait()

        nblk = xs_hbm.shape[0] // TM_MOE

        def tail_start(b, c):
            fill(b * TM_MOE).start()
            return c

        def tail_wait(b, c):
            fill(b * TM_MOE).wait()
            return c

        lax.fori_loop(nused_ref[0], nblk, tail_start, 0)
        lax.fori_loop(nused_ref[0], nblk, tail_wait, 0)

    def issue(tl, c):
        src = h_ref.at[pl.ds(tl, 1)]
        for k in range(TOP_K):
            pltpu.make_async_copy(src, xs_hbm.at[pl.ds(dest_ref[0, tl * TOP_K + k], 1)], sem).start(
                priority=k % 2)
        return c

    lax.fori_loop(0, ntok, issue, 0)
    for _ in range(TOP_K):
        pltpu.make_async_copy(h_ref, xs_hbm.at[pl.ds(0, ntok)], sem).wait()


def _dispatch(fill_start, fill_flag, nused, dest, h2):
    t, d = h2.shape
    td = min(TD_DISPATCH, t)
    r = t * TOP_K + N_EXPERTS * TM_MOE
    grid_spec = pltpu.PrefetchScalarGridSpec(
        num_scalar_prefetch=3,
        grid=(t // td,),
        in_specs=[
            pl.BlockSpec((None, 1, td * TOP_K), lambda i, fs, ff, nu: (i, 0, 0), memory_space=pltpu.SMEM),
            pl.BlockSpec((td, d), lambda i, fs, ff, nu: (i, 0)),
        ],
        out_specs=pl.BlockSpec(memory_space=pl.ANY),
        scratch_shapes=[pltpu.VMEM((TM_MOE, d), F32), pltpu.SemaphoreType.DMA(()), pltpu.SemaphoreType.DMA(())],
    )
    return pl.pallas_call(
        _dispatch_kernel,
        grid_spec=grid_spec,
        out_shape=jax.ShapeDtypeStruct((r, d), F32),
        compiler_params=_params("arbitrary"),
        name="moe_dispatch",
    )(fill_start, fill_flag, nused, dest.reshape(t // td, 1, td * TOP_K), h2)


def _moe_ffn_kernel(blk_e_ref, nused_ref, x_ref, w1_ref, b1_ref, w2_ref, b2_ref, y_ref, w1b, w2b):
    i = pl.program_id(0)
    dff = w2_ref.shape[0]

    @pl.when(i < nused_ref[0])
    def _():
        prev = blk_e_ref[jnp.maximum(i - 1, 0)]

        @pl.when((i == 0) | (blk_e_ref[i] != prev))
        def _():
            w1b[...] = w1_ref[...].astype(BF16)
            w2b[...] = w2_ref[...].astype(BF16)

        u = _dot(x_ref[...].astype(BF16), w1b[...]) + b1_ref[...]
        gl = jnp.minimum(u[:, :dff], SWIGLU_LIMIT)
        lin = jnp.clip(u[:, dff:], -SWIGLU_LIMIT, SWIGLU_LIMIT)
        act = (lin + 1.0) * gl * _sigmoid(SWIGLU_ALPHA * gl)
        y_ref[...] = _dot(act.astype(BF16), w2b[...]) + b2_ref[...]

    @pl.when(i >= nused_ref[0])
    def _():
        y_ref[...] = jnp.zeros(y_ref.shape, y_ref.dtype)


def _moe_ffn(blk_e, nused, xs, w1, b1, w2, b2, layer):
    r, d = xs.shape
    depth, ne, _, f2 = w1.shape
    dff = f2 // 2
    tm = TM_MOE
    rows = lambda i, be, nu: (jnp.minimum(i, nu[0] - 1), 0)
    expert = lambda i, be, nu: (layer, be[i], 0, 0)
    grid_spec = pltpu.PrefetchScalarGridSpec(
        num_scalar_prefetch=2,
        grid=(r // tm,),
        in_specs=[
            pl.BlockSpec((tm, d), rows),
            pl.BlockSpec((None, None, d, f2), expert),
            pl.BlockSpec((None, None, 1, f2), expert),
            pl.BlockSpec((None, None, dff, d), expert),
            pl.BlockSpec((None, None, 1, d), expert),
        ],
        out_specs=pl.BlockSpec((tm, d), lambda i, be, nu: (i, 0)),
        scratch_shapes=[pltpu.VMEM((d, f2), BF16), pltpu.VMEM((dff, d), BF16)],
    )
    return pl.pallas_call(
        _moe_ffn_kernel,
        grid_spec=grid_spec,
        out_shape=jax.ShapeDtypeStruct((r, d), F32),
        compiler_params=_params("arbitrary"),
        name="moe_ffn",
    )(blk_e, nused, xs, w1, b1.reshape(depth, ne, 1, f2), w2, b2.reshape(depth, ne, 1, d))


def _moe_combine_kernel(pos_ref, nxt_ref, y_hbm, gate_ref, x1_ref, g2_ref, o_ref, buf, sems):
    i = pl.program_id(0)
    n = pl.num_programs(0)
    tc = x1_ref.shape[0]
    slot = i % 2

    def fetch(p_ref, s):
        def issue(tl, c):
            for k in range(TOP_K):
                pltpu.make_async_copy(y_hbm.at[pl.ds(p_ref[0, tl * TOP_K + k], 1)],
                                      buf.at[s, k, pl.ds(tl, 1)], sems.at[s]).start(priority=k % 2)
            return c
        lax.fori_loop(0, tc, issue, 0)

    @pl.when(i == 0)
    def _():
        fetch(pos_ref, 0)

    @pl.when(i + 1 < n)
    def _():
        fetch(nxt_ref, 1 - slot)

    for k in range(TOP_K):
        pltpu.make_async_copy(y_hbm.at[pl.ds(0, tc)], buf.at[slot, k], sems.at[slot]).wait()
    gate = gate_ref[...]
    moe = gate[:, 0:1] * buf[slot, 0]
    for k in range(1, TOP_K):
        moe = moe + gate[:, k:k + 1] * buf[slot, k]
    o_ref[...] = x1_ref[...] + g2_ref[...] * moe


def _moe_combine(pos, y, gate, x1, g2, seq):
    t, d = x1.shape
    tc = TC_COMB
    nb = seq // tc
    n = t // tc
    pos3 = pos.reshape(n, 1, tc * TOP_K)
    return pl.pallas_call(
        _moe_combine_kernel,
        grid=(n,),
        in_specs=[
            pl.BlockSpec((None, 1, tc * TOP_K), lambda i: (i, 0, 0), memory_space=pltpu.SMEM),
            pl.BlockSpec((None, 1, tc * TOP_K), lambda i: (jnp.minimum(i + 1, n - 1), 0, 0),
                         memory_space=pltpu.SMEM),
            pl.BlockSpec(memory_space=pl.ANY),
            pl.BlockSpec((tc, TOP_K), lambda i: (i, 0)),
            pl.BlockSpec((tc, d), lambda i: (i, 0)),
            pl.BlockSpec((None, 1, d), lambda i: (i // nb, 0, 0)),
        ],
        out_specs=pl.BlockSpec((tc, d), lambda i: (i, 0)),
        out_shape=jax.ShapeDtypeStruct((t, d), F32),
        scratch_shapes=[pltpu.VMEM((2, TOP_K, tc, d), F32), pltpu.SemaphoreType.DMA((2,))],
        compiler_params=_params("arbitrary"),
        name="moe_combine",
    )(pos3, pos3, y, gate, x1, g2)


def _tile_gain(gain, width, scale=1.0):
    return (jnp.tile(gain, width // HEAD_DIM) * scale).reshape(1, width).astype(F32)


def _layer(x2d, mod, b, s, ln1_g, ln2_g, w_in, nsa_q_gain, nsa_k_gain, cmp_pos, cmp_w1, cmp_w2,
           moba_q_gain, moba_k_gain, w_out, router_w, router_b, exp_w1, exp_b1, exp_w2, exp_b2, layer):
    t, d = x2d.shape
    sh1, sc1, g1, sh2, sc2, g2 = [m.reshape(b, 1, d) for m in jnp.split(mod, 6, axis=-1)]
    scale = HEAD_DIM ** -0.5

    (qt, kcv, ks_t, vs_t, kw_t, vw_t, gates_t, mq_t, mk_t, mv_t, kmean) = _inproj(
        x2d, ln1_g.reshape(1, d), 1.0 + sc1, sh1, _pack_w_in(w_in),
        _tile_gain(nsa_q_gain, 512, scale), _tile_gain(nsa_k_gain, 128),
        _tile_gain(moba_q_gain, 512, scale), _tile_gain(moba_k_gain, 512), b, s)

    g = NSA_GROUPS
    xc = kcv.reshape(b, s, 2, g, HEAD_DIM).transpose(0, 2, 3, 1, 4).reshape(
        b, 2, g, s // CMP_STRIDE, CMP_STRIDE * HEAD_DIM)
    cmp = _compress(xc, cmp_pos.reshape(2, 1, CMP_LEN * HEAD_DIM), cmp_w1.astype(BF16),
                    cmp_w2.astype(BF16), nsa_k_gain.reshape(1, HEAD_DIM))
    kc = cmp[:, 0].astype(BF16)
    vct = cmp[:, 1].transpose(0, 1, 3, 2).astype(BF16)
    acc_t, sel_t, tile_hits = _nsa_cmp(qt, kc, vct, gates_t)
    nsel = sel_t.shape[2]
    if nsel < MAX_SEL_BLOCKS:
        sel_t = jnp.concatenate([sel_t, jnp.ones((b, g, MAX_SEL_BLOCKS - nsel, s), F32)], axis=2)
    tile_flags = (tile_hits[..., 0] > 0).astype(I32).reshape(-1)
    o_nsa_t = _nsa_selwin(tile_flags, qt, ks_t, vs_t, kw_t, vw_t, sel_t, gates_t, acc_t)

    km = kmean.reshape(b, s // MOBA_BLOCK, MOBA_HEADS, HEAD_DIM).transpose(0, 2, 1, 3)
    o_moba_t = _moba(mq_t, mk_t, mv_t, km)

    wr = jnp.zeros((d, LANES), F32).at[:, :N_EXPERTS].set(router_w)
    wr_hi = wr.astype(BF16)
    wr_lo = (wr - wr_hi.astype(F32)).astype(BF16)
    rb = jnp.zeros((1, LANES), F32).at[0, :N_EXPERTS].set(router_b)
    x1, h2, logits_t = _outproj(o_nsa_t, o_moba_t, w_out.astype(BF16), x2d, g1, ln2_g.reshape(1, d),
                                1.0 + sc2, sh2, wr_hi, wr_lo, rb)

    idx8, gate8, rank8, counts = _route(logits_t)
    dest, blk_e, nused, fill_start, fill_flag = _dispatch_plan(idx8[:TOP_K], rank8[:TOP_K], counts[:, 0])
    xs = _dispatch(fill_start, fill_flag, nused, dest, h2)
    y = _moe_ffn(blk_e, nused, xs, exp_w1, exp_b1, exp_w2, exp_b2, layer)
    return _moe_combine(dest, y, gate8[:TOP_K].T, x1, g2, s)


def kernel(x, c, ada_w, ada_b, ln1_g, ln2_g, w_in, nsa_q_gain, nsa_k_gain, nsa_cmp_pos, nsa_cmp_w1,
           nsa_cmp_w2, moba_q_gain, moba_k_gain, w_out, router_w, router_b, exp_w1, exp_b1, exp_w2, exp_b2):
    b, s, d = x.shape
    depth = ada_w.shape[0]
    mods = _adaln(c, ada_w, ada_b)
    x2d = x.reshape(b * s, d)
    for l in range(depth):
        x2d = _layer(x2d, mods[l], b, s, ln1_g[l], ln2_g[l], w_in[l], nsa_q_gain[l], nsa_k_gain[l],
                     nsa_cmp_pos[l], nsa_cmp_w1[l], nsa_cmp_w2[l], moba_q_gain[l], moba_k_gain[l],
                     w_out[l], router_w[l], router_b[l], exp_w1, exp_b1, exp_w2, exp_b2, l)
    return x2d.reshape(b, s, d)
```

```python
import functools

import numpy as np
import jax
import jax.numpy as jnp
from jax import lax
from jax.experimental import pallas as pl
from jax.experimental.pallas import tpu as pltpu

F32 = jnp.float32
BF16 = jnp.bfloat16
I32 = jnp.int32

HEAD_DIM = 64
NSA_HEADS = 8
NSA_GROUPS = 2
NSA_HPG = NSA_HEADS // NSA_GROUPS
CMP_LEN = 32
CMP_STRIDE = 16
CMP_HID = 128
SEL_BLOCK = 64
SEL_TOP = 16
WINDOW = 512
MOBA_HEADS = 8
MOBA_BLOCK = 256
MOBA_TOP = 3
N_EXPERTS = 32
TOP_K = 4
SWIGLU_LIMIT = 7.0
SWIGLU_ALPHA = 1.702
EPS = 1e-6
NEG = -1e30
FORCE_SCORE = 1e4

LANES = 128
SUBLANES = 8
TQ = 256
TK = 256
POS_SPLIT = 256
BIG = 1e30
MAX_SEL_BLOCKS = 128
MAX_MOBA_BLOCKS = 32
MOBA_HB = 4
VAL_ROWS = HEAD_DIM + 16
RANK_CHUNK = 32
TM_PROJ = 512
TM_MOE = 512
TC_COMB = 128
TT_ROUTE = 512
TD_DISPATCH = 512
VMEM_LIMIT = 56 * 1024 * 1024


def _dot(a, b):
    return jnp.dot(a, b, preferred_element_type=F32)


def _hi_lo(x):
    hi = x.astype(BF16)
    lo = (x - hi.astype(F32)).astype(BF16)
    return hi, lo


def _sigmoid(x):
    return 1.0 / (1.0 + jnp.exp(-x))


def _params(*sem):
    return pltpu.CompilerParams(dimension_semantics=sem, vmem_limit_bytes=VMEM_LIMIT)


def _adaln_kernel(c_ref, w_ref, b_ref, o_ref):
    c = c_ref[...]
    cond = c * _sigmoid(c)
    ch, cl = _hi_lo(cond)
    wh, wl = _hi_lo(w_ref[...])
    o_ref[...] = _dot(ch, wh) + _dot(ch, wl) + _dot(cl, wh) + b_ref[...]


def _adaln(c, ada_w, ada_b):
    depth, d, n6 = ada_w.shape
    b = c.shape[0]
    tn = 1024
    c_pad = jnp.zeros((SUBLANES, d), F32).at[:b].set(c)
    out = pl.pallas_call(
        _adaln_kernel,
        grid=(depth, n6 // tn),
        in_specs=[
            pl.BlockSpec((SUBLANES, d), lambda l, j: (0, 0)),
            pl.BlockSpec((None, d, tn), lambda l, j: (l, 0, j)),
            pl.BlockSpec((None, 1, tn), lambda l, j: (l, 0, j)),
        ],
        out_specs=pl.BlockSpec((None, SUBLANES, tn), lambda l, j: (l, 0, j)),
        out_shape=jax.ShapeDtypeStruct((depth, SUBLANES, n6), F32),
        compiler_params=_params("arbitrary", "arbitrary"),
        name="adaln",
    )(c_pad, ada_w, ada_b.reshape(depth, 1, n6))
    return out[:, :b]


def _block_diag_ones():
    r = lax.broadcasted_iota(I32, (LANES, LANES), 0) // HEAD_DIM
    c = lax.broadcasted_iota(I32, (LANES, LANES), 1) // HEAD_DIM
    return jnp.where(r == c, 1.0, 0.0).astype(BF16)


def _head_norm(z, bd, gain):
    cols = []
    for c0 in range(0, z.shape[1], LANES):
        zc = z[:, c0:c0 + LANES]
        hi, lo = _hi_lo(zc * zc)
        ss = _dot(hi, bd) + _dot(lo, bd)
        cols.append(zc * lax.rsqrt(ss * (1.0 / HEAD_DIM) + EPS))
    zn = cols[0] if len(cols) == 1 else jnp.concatenate(cols, axis=1)
    return zn * gain


def _modulated_norm(x, g, sc1p, sh):
    ms = jnp.mean(x * x, axis=-1, keepdims=True)
    y = x * lax.rsqrt(ms + EPS)
    return (y * g) * sc1p + sh


def _heads_transposed(z, nh):
    return z.T.astype(BF16).reshape(nh, HEAD_DIM, z.shape[0])


def _store_value_tiles(o_ref, z, nh):
    zt = z.T.astype(BF16)
    extra = VAL_ROWS - HEAD_DIM
    ones_row = jnp.where(lax.broadcasted_iota(I32, (extra, TK), 0) == 0, 1.0, 0.0).astype(BF16)
    for h in range(nh):
        for t in range(z.shape[0] // TK):
            vt = zt[h * HEAD_DIM:(h + 1) * HEAD_DIM, t * TK:(t + 1) * TK]
            o_ref[h, t] = jnp.concatenate([vt, ones_row], axis=0)


def _store_key_tiles(o_ref, z, extra, nh):
    rows = z.shape[0]
    for h in range(nh):
        ka = jnp.concatenate([z[:, h * HEAD_DIM:(h + 1) * HEAD_DIM], extra], axis=1).astype(BF16)
        o_ref[h] = ka.reshape(rows // TK, TK, ka.shape[1])


def _pos_cols(pos, width):
    col = lax.broadcasted_iota(I32, (pos.shape[0], width), 1)
    hi = lax.shift_right_logical(pos, POS_SPLIT.bit_length() - 1).astype(F32)
    lo = (pos & (POS_SPLIT - 1)).astype(F32)
    return jnp.where(col == 0, hi, jnp.where(col == 1, lo, jnp.where(col == 2, 1.0, 0.0)))


def _onehot_cols(pos, block, width):
    col = lax.broadcasted_iota(I32, (pos.shape[0], width), 1)
    return jnp.where(lax.shift_right_logical(pos, block.bit_length() - 1) == col, 1.0, 0.0)


def _inproj_kernel(x_ref, g_ref, sc_ref, sh_ref, w_ref, gq_ref, gk_ref, gmq_ref, gmk_ref,
                   q_ref, kcv_ref, ks_ref, vs_ref, kw_ref, vw_ref, gate_ref,
                   mq_ref, mk_ref, mv_ref, kmean_ref):
    h = _modulated_norm(x_ref[...], g_ref[...], sc_ref[...], sh_ref[...])
    hb = h.astype(BF16)
    bd = _block_diag_ones()
    rows = hb.shape[0]
    pos = pl.program_id(1) * rows + lax.broadcasted_iota(I32, (rows, 1), 0)

    def proj(a, b):
        return _dot(hb, w_ref[:, a:b])

    q_ref[...] = _heads_transposed(_head_norm(proj(0, 512), bd, gq_ref[...]), NSA_HEADS)
    kv = proj(512, 1280)
    kcv_ref[...] = kv[:, 0:256]
    pos_cols = _pos_cols(pos, HEAD_DIM)
    sel_cols = jnp.concatenate([pos_cols, _onehot_cols(pos, SEL_BLOCK, MAX_SEL_BLOCKS)], axis=1)
    _store_key_tiles(ks_ref, _head_norm(kv[:, 256:384], bd, gk_ref[...]), sel_cols, NSA_GROUPS)
    _store_value_tiles(vs_ref, kv[:, 384:512], NSA_GROUPS)
    _store_key_tiles(kw_ref, _head_norm(kv[:, 512:640], bd, gk_ref[...]), pos_cols, NSA_GROUPS)
    _store_value_tiles(vw_ref, kv[:, 640:768], NSA_GROUPS)
    gate_ref[...] = _sigmoid(proj(1280, 1408)).T
    mq_ref[...] = _heads_transposed(_head_norm(proj(1408, 1920), bd, gmq_ref[...]), MOBA_HEADS)
    mkn = _head_norm(proj(1920, 2432), bd, gmk_ref[...])
    moba_cols = jnp.concatenate([_pos_cols(pos, MAX_MOBA_BLOCKS),
                                 _onehot_cols(pos, MOBA_BLOCK, MAX_MOBA_BLOCKS)], axis=1)
    _store_key_tiles(mk_ref, mkn, moba_cols, MOBA_HEADS)
    _store_value_tiles(mv_ref, proj(2432, 2944), MOBA_HEADS)
    means = [jnp.sum(mkn[r:r + MOBA_BLOCK], axis=0, keepdims=True) * (1.0 / MOBA_BLOCK)
             for r in range(0, rows, MOBA_BLOCK)]
    kmean_ref[...] = jnp.concatenate(means, axis=0)


def _pack_w_in(w_in):
    d = w_in.shape[0]
    wg = jnp.zeros((d, LANES), w_in.dtype).at[:, :3 * NSA_HEADS].set(w_in[:, 1280:1304])
    return jnp.concatenate([w_in[:, :1280], wg, w_in[:, 1304:]], axis=1).astype(BF16)


def _inproj(x2d, ln_g, sc1p, sh, w_pack, gq, gk, gmq, gmk, b, seq):
    t, d = x2d.shape
    tm = TM_PROJ
    nb = seq // tm
    nt = seq // TK
    tpt = tm // TK
    g = NSA_GROUPS
    row = lambda i, j: (i * nb + j, 0)
    bat = lambda i, j: (i, 0, 0)
    const = lambda i, j: (0, 0)
    heads_t = lambda nh: pl.BlockSpec((None, nh, HEAD_DIM, tm), lambda i, j: (i, 0, 0, j))
    key_tiles = lambda nh, w: pl.BlockSpec((None, nh, tpt, TK, w), lambda i, j: (i, 0, j, 0, 0))
    val_tiles = lambda nh: pl.BlockSpec((None, nh, tpt, VAL_ROWS, TK), lambda i, j: (i, 0, j, 0, 0))
    sel_w = 2 * HEAD_DIM + MAX_SEL_BLOCKS
    win_w = 2 * HEAD_DIM
    moba_w = HEAD_DIM + 2 * MAX_MOBA_BLOCKS
    nmean = tm // MOBA_BLOCK
    sds = jax.ShapeDtypeStruct
    out_shape = [
        sds((b, NSA_HEADS, HEAD_DIM, seq), BF16), sds((t, 256), F32),
        sds((b, g, nt, TK, sel_w), BF16), sds((b, g, nt, VAL_ROWS, TK), BF16),
        sds((b, g, nt, TK, win_w), BF16), sds((b, g, nt, VAL_ROWS, TK), BF16),
        sds((b, LANES, seq), F32),
        sds((b, MOBA_HEADS, HEAD_DIM, seq), BF16),
        sds((b, MOBA_HEADS, nt, TK, moba_w), BF16), sds((b, MOBA_HEADS, nt, VAL_ROWS, TK), BF16),
        sds((t // tm, nmean, 512), F32),
    ]
    out_specs = [
        heads_t(NSA_HEADS), pl.BlockSpec((tm, 256), row),
        key_tiles(g, sel_w), val_tiles(g), key_tiles(g, win_w), val_tiles(g),
        pl.BlockSpec((None, LANES, tm), lambda i, j: (i, 0, j)),
        heads_t(MOBA_HEADS), key_tiles(MOBA_HEADS, moba_w), val_tiles(MOBA_HEADS),
        pl.BlockSpec((None, nmean, 512), lambda i, j: (i * nb + j, 0, 0)),
    ]
    return pl.pallas_call(
        _inproj_kernel,
        grid=(b, nb),
        in_specs=[
            pl.BlockSpec((tm, d), row),
            pl.BlockSpec((1, d), const),
            pl.BlockSpec((None, 1, d), bat),
            pl.BlockSpec((None, 1, d), bat),
            pl.BlockSpec(w_pack.shape, const),
            pl.BlockSpec((1, 512), const),
            pl.BlockSpec((1, 128), const),
            pl.BlockSpec((1, 512), const),
            pl.BlockSpec((1, 512), const),
        ],
        out_specs=out_specs,
        out_shape=out_shape,
        compiler_params=_params("arbitrary", "arbitrary"),
        name="inproj",
    )(x2d, ln_g, sc1p, sh, w_pack, gq, gk, gmq, gmk)


def _compress_kernel(x_ref, pos_ref, w1_ref, w2_ref, gk_ref, o_ref):
    kv = pl.program_id(1)
    half = CMP_STRIDE * HEAD_DIM
    x = x_ref[...]
    xl = (x + pos_ref[:, :half]).astype(BF16)
    xh = (x + pos_ref[:, half:]).astype(BF16)
    a = _dot(xl, w1_ref[:half, :])
    b = _dot(xh, w1_ref[half:, :])
    hid = a + pltpu.roll(b, b.shape[0] - 1, 0)
    out = _dot(jax.nn.gelu(hid, approximate=True).astype(BF16), w2_ref[...])
    ms = jnp.mean(out * out, axis=-1, keepdims=True)
    normed = (out * lax.rsqrt(ms + EPS)) * gk_ref[...]
    o_ref[...] = jnp.where(kv == 0, normed, out)


def _compress(xc, pos, w1, w2, gk):
    b, _, g, nr, w = xc.shape
    return pl.pallas_call(
        _compress_kernel,
        grid=(b, 2, g),
        in_specs=[
            pl.BlockSpec((None, None, None, nr, w), lambda i, k, j: (i, k, j, 0, 0)),
            pl.BlockSpec((None, 1, w * 2), lambda i, k, j: (k, 0, 0)),
            pl.BlockSpec((None, w * 2, CMP_HID), lambda i, k, j: (k, 0, 0)),
            pl.BlockSpec((None, CMP_HID, HEAD_DIM), lambda i, k, j: (k, 0, 0)),
            pl.BlockSpec((1, HEAD_DIM), lambda i, k, j: (0, 0)),
        ],
        out_specs=pl.BlockSpec((None, None, None, nr, HEAD_DIM), lambda i, k, j: (i, k, j, 0, 0)),
        out_shape=jax.ShapeDtypeStruct((b, 2, g, nr, HEAD_DIM), F32),
        compiler_params=_params("arbitrary", "arbitrary", "arbitrary"),
        name="nsa_compress",
    )(xc, pos, w1, w2, gk)


def _nsa_slope(g, h):
    return 2.0 ** (-(g * NSA_HPG + h + 1))


def _rank_counts(imp_ref, cnt_ref, last_blk):
    nb = imp_ref.shape[0]
    chunk = min(RANK_CHUNK, nb)
    cnt_ref[...] = jnp.zeros(cnt_ref.shape, cnt_ref.dtype)
    sub = lax.broadcasted_iota(I32, (SUBLANES, imp_ref.shape[1]), 0)
    for c in range(nb // chunk):
        for d in range(nb // chunk):
            @pl.when(max(c, d) * chunk <= last_blk)
            def _():
                rows0 = range(d * chunk, (d + 1) * chunk, SUBLANES)
                pieces = [imp_ref[r:r + SUBLANES] for r in rows0]
                counts = [jnp.zeros(p.shape, I32) for p in pieces]
                for jp in range(c * chunk, (c + 1) * chunk):
                    row = imp_ref[jp:jp + 1]
                    for k, r0 in enumerate(rows0):
                        p = pieces[k]
                        if r0 + SUBLANES - 1 < jp:
                            inc = jnp.where(row > p, 1, 0)
                        elif r0 > jp:
                            inc = jnp.where(row >= p, 1, 0)
                        else:
                            inc = jnp.where(sub > jp - r0, jnp.where(row >= p, 1, 0), jnp.where(row > p, 1, 0))
                        counts[k] = counts[k] + inc
                rows = slice(d * chunk, (d + 1) * chunk)
                cnt_ref[rows] = cnt_ref[rows] + jnp.concatenate(counts, axis=0)


def _nsa_cmp_kernel(q_ref, kc_ref, vct_ref, inct_ref, gate_ref, acc_ref, sel_ref, flag_ref,
                    imp_ref, cnt_ref, *, n_top):
    g = pl.program_id(1)
    q0 = pl.program_id(2) * TQ
    nc = kc_ref.shape[0]
    nsel = inct_ref.shape[0]
    t = q0 + lax.broadcasted_iota(I32, (1, TQ), 1)
    width = NSA_HPG * TQ
    tw = q0 + (lax.broadcasted_iota(I32, (1, width), 1) & (TQ - 1))
    n = lax.broadcasted_iota(I32, (nc, 1), 0)
    center = (CMP_STRIDE * n).astype(F32) + 0.5 * (CMP_LEN - 1)
    dist = tw.astype(F32) - center
    mask = (CMP_STRIDE * n + (CMP_LEN - 1)) <= tw
    slopes = jnp.concatenate(
        [jnp.full((1, TQ), 1.0, F32) * jnp.where(g == 0, _nsa_slope(0, h), _nsa_slope(1, h))
         for h in range(NSA_HPG)], axis=1)
    qw = jnp.concatenate([q_ref[h] for h in range(NSA_HPG)], axis=1)
    s = _dot(kc_ref[...], qw) - slopes * dist
    s = jnp.where(mask, s, NEG)
    m = jnp.max(s, axis=0, keepdims=True)
    e = jnp.exp(s - m)
    l = jnp.sum(e, axis=0, keepdims=True)
    p = jnp.where(mask, e, 0.0) * (1.0 / l)
    o = _dot(vct_ref[...], p.astype(BF16))
    psum = jnp.zeros((nc, TQ), F32)
    for h in range(NSA_HPG):
        lanes = slice(h * TQ, (h + 1) * TQ)
        gate = gate_ref[pl.ds(g * (3 * NSA_HPG) + 3 * h, 1), :]
        acc_ref[h] = gate * o[:, lanes]
        psum = psum + p[:, lanes]
    hi, lo = _hi_lo(psum)
    inct = inct_ref[...]
    imp = _dot(inct, hi) + _dot(inct, lo)
    blk = lax.broadcasted_iota(I32, (nsel, 1), 0)
    cur = t // SEL_BLOCK
    forced = (blk == 0) | (blk == cur) | (blk == cur - 1)
    imp = jnp.where(forced, FORCE_SCORE, imp)
    imp_ref[...] = jnp.where(blk <= cur, imp, NEG)
    _rank_counts(imp_ref, cnt_ref, (q0 + TQ - 1) // SEL_BLOCK)
    chosen = (cnt_ref[...] < n_top) & (blk <= cur)
    sel = jnp.where(chosen, 1.0, 0.0)
    sel_ref[...] = sel
    bpt = TK // SEL_BLOCK
    ntile = nsel // bpt
    trow = lax.broadcasted_iota(I32, (ntile, nsel), 0)
    tcol = lax.broadcasted_iota(I32, (ntile, nsel), 1)
    member = jnp.where(lax.shift_right_logical(tcol, bpt.bit_length() - 1) == trow, 1.0, 0.0).astype(BF16)
    hits = jnp.sum(_dot(member, sel.astype(BF16)), axis=1, keepdims=True)
    flag_ref[...] = jnp.broadcast_to(hits, (ntile, LANES))


def _incidence_t(nc, nsel):
    cmp_start = CMP_STRIDE * np.arange(nc)
    sel_start = SEL_BLOCK * np.arange(nsel)
    inc = ((cmp_start[:, None] <= sel_start[None, :] + SEL_BLOCK - 1)
           & (cmp_start[:, None] + CMP_LEN - 1 >= sel_start[None, :]))
    inc[nc - 1] = False
    return jnp.asarray(inc.T, dtype=BF16)


def _nsa_cmp(qt, kc, vct, gates_t):
    b, _, _, s = qt.shape
    nc = kc.shape[2]
    nsel = s // SEL_BLOCK
    n_top = min(SEL_TOP, nsel)
    ntile = s // TK
    inct = _incidence_t(nc, nsel)
    return pl.pallas_call(
        functools.partial(_nsa_cmp_kernel, n_top=n_top),
        grid=(b, NSA_GROUPS, s // TQ),
        in_specs=[
            pl.BlockSpec((None, NSA_HPG, HEAD_DIM, TQ), lambda i, g, j: (i, g, 0, j)),
            pl.BlockSpec((None, None, nc, HEAD_DIM), lambda i, g, j: (i, g, 0, 0)),
            pl.BlockSpec((None, None, HEAD_DIM, nc), lambda i, g, j: (i, g, 0, 0)),
            pl.BlockSpec((nsel, nc), lambda i, g, j: (0, 0)),
            pl.BlockSpec((None, LANES, TQ), lambda i, g, j: (i, 0, j)),
        ],
        out_specs=[
            pl.BlockSpec((None, NSA_HPG, HEAD_DIM, TQ), lambda i, g, j: (i, g, 0, j)),
            pl.BlockSpec((None, None, nsel, TQ), lambda i, g, j: (i, g, 0, j)),
            pl.BlockSpec((None, None, None, ntile, LANES), lambda i, g, j: (i, g, j, 0, 0)),
        ],
        out_shape=[
            jax.ShapeDtypeStruct((b, NSA_HEADS, HEAD_DIM, s), F32),
            jax.ShapeDtypeStruct((b, NSA_GROUPS, nsel, s), F32),
            jax.ShapeDtypeStruct((b, NSA_GROUPS, s // TQ, ntile, LANES), F32),
        ],
        scratch_shapes=[pltpu.VMEM((nsel, TQ), F32), pltpu.VMEM((nsel, TQ), I32)],
        compiler_params=_params("arbitrary", "arbitrary", "arbitrary"),
        name="nsa_cmp_select",
    )(qt, kc, vct, inct, gates_t)


def _alibi_rows(slope, q0, rows):
    r = lax.broadcasted_iota(I32, (rows, TQ), 0)
    return jnp.where(r == 0, slope * POS_SPLIT, jnp.where(r == 1, slope, jnp.where(r == 2, -slope * q0, 0.0)))


def _wide_dist(width):
    lane = lax.broadcasted_iota(I32, (TK, width), 1) & (TQ - 1)
    return lane - lax.broadcasted_iota(I32, (TK, width), 0)


def _exp_weights(s, m):
    return jnp.exp((s - m).astype(BF16))


class _FlashPipe:
    def __init__(self, scores, values, m_ref, acc_ref, s_ref, p_ref, alpha_ref, tile_of=lambda j: j):
        self.scores, self.values, self.tile_of = scores, values, tile_of
        self.m_ref, self.acc_ref = m_ref, acc_ref
        self.s_ref, self.p_ref, self.alpha_ref = s_ref, p_ref, alpha_ref

    def run(self, first_scores, first_tile, n_past):
        m = jnp.max(first_scores, axis=0, keepdims=True)
        self.m_ref[...] = m
        self.p_ref[...] = _exp_weights(first_scores, m)
        self.alpha_ref[...] = jnp.ones(self.alpha_ref.shape, F32)
        self.acc_ref[...] = jnp.zeros(self.acc_ref.shape, F32)

        @pl.when(n_past > 0)
        def _():
            self.s_ref[...] = self.scores(self.tile_of(0))

        def step(j, carry):
            pending = jnp.where(j == 0, first_tile, self.tile_of(jnp.maximum(j - 1, 0)))
            s = self.s_ref[...]
            self.s_ref[...] = self.scores(self.tile_of(jnp.minimum(j + 1, n_past - 1)))
            pv = self.values(pending, self.p_ref[...])
            self.acc_ref[...] = self.alpha_ref[...] * self.acc_ref[...] + pv
            m_old = self.m_ref[...]
            m_new = jnp.maximum(m_old, jnp.max(s, axis=0, keepdims=True))
            self.m_ref[...] = m_new
            self.alpha_ref[...] = jnp.exp(m_old - m_new)
            self.p_ref[...] = _exp_weights(s, m_new)
            return carry

        lax.fori_loop(0, n_past, step, 0)
        pending = jnp.where(n_past == 0, first_tile, self.tile_of(jnp.maximum(n_past - 1, 0)))
        acc = self.alpha_ref[...] * self.acc_ref[...] + self.values(pending, self.p_ref[...])
        return acc[:HEAD_DIM] / acc[HEAD_DIM:HEAD_DIM + 1]


def _nsa_selwin_kernel(flags_ref, q_ref, ks_ref, vst_ref, kw_ref, vwt_ref, sel_ref, gate_ref, accin_ref,
                       o_ref, qa_ref, ms_ref, as_ref, s_ref, p_ref, alpha_ref, visit_ref):
    g = pl.program_id(1)
    i = pl.program_id(2)
    ntile = ks_ref.shape[0]
    base = ((pl.program_id(0) * NSA_GROUPS + g) * pl.num_programs(2) + i) * ntile

    def note(jt, n):
        visit_ref[n] = jt
        return n + flags_ref[base + jt]

    n_visit = lax.fori_loop(0, i, note, 0)
    q0 = (i * TQ).astype(F32)
    width = NSA_HPG * TQ
    selbias = ((sel_ref[...] - 1.0) * BIG).astype(BF16)
    for h in range(NSA_HPG):
        slope = jnp.where(g == 0, _nsa_slope(0, h), _nsa_slope(1, h))
        lanes = slice(h * TQ, (h + 1) * TQ)
        qa_ref[0:HEAD_DIM, lanes] = q_ref[h]
        qa_ref[HEAD_DIM:2 * HEAD_DIM, lanes] = _alibi_rows(slope, q0, HEAD_DIM).astype(BF16)
        qa_ref[2 * HEAD_DIM:, lanes] = selbias
    qa = qa_ref[...]
    qw = qa_ref[0:2 * HEAD_DIM, :]
    dist = _wide_dist(width)

    causal = dist >= 0
    tiles = [i]
    scores = [jnp.where(causal, _dot(kw_ref[i], qw), NEG)]
    for d in range(1, WINDOW // TK + 1):
        tiles.append(jnp.maximum(i - d, 0))
        limit = jnp.where(i >= d, WINDOW, -(2 ** 30))
        scores.append(jnp.where(dist + d * TK < limit, _dot(kw_ref[tiles[d]], qw), NEG))
    m_win = jnp.max(scores[0], axis=0, keepdims=True)
    for s in scores[1:]:
        m_win = jnp.maximum(m_win, jnp.max(s, axis=0, keepdims=True))
    a_win = jnp.zeros((VAL_ROWS, width), F32)
    for jt, s in zip(tiles, scores):
        a_win = a_win + _dot(vwt_ref[jt], _exp_weights(s, m_win))
    o_win = a_win[:HEAD_DIM] / a_win[HEAD_DIM:HEAD_DIM + 1]

    pipe = _FlashPipe(lambda j: _dot(ks_ref[j], qa_ref[...]), lambda j, p: _dot(vst_ref[j], p),
                      ms_ref, as_ref, s_ref, p_ref, alpha_ref, tile_of=lambda j: visit_ref[j])
    o_sel = pipe.run(jnp.where(causal, _dot(ks_ref[i], qa), NEG), i, n_visit)
    for h in range(NSA_HPG):
        lanes = slice(h * TQ, (h + 1) * TQ)
        row = g * (3 * NSA_HPG) + 3 * h
        g_sel = gate_ref[pl.ds(row + 1, 1), :]
        g_win = gate_ref[pl.ds(row + 2, 1), :]
        o_ref[h] = (accin_ref[h] + g_sel * o_sel[:, lanes] + g_win * o_win[:, lanes]).astype(o_ref.dtype)


def _nsa_selwin(tile_flags, qt, ks_t, vs_t, kw_t, vw_t, sel_t, gates_t, acc_t):
    b, _, _, s = qt.shape
    nt = s // TK
    width = NSA_HPG * TQ
    ksw = ks_t.shape[-1]
    kww = kw_t.shape[-1]
    vspec = pl.BlockSpec((None, None, nt, VAL_ROWS, TK), lambda i, g, j, fl: (i, g, 0, 0, 0))
    qspec = pl.BlockSpec((None, NSA_HPG, HEAD_DIM, TQ), lambda i, g, j, fl: (i, g, 0, j))
    stat = pltpu.VMEM((1, width), F32)
    accs = pltpu.VMEM((VAL_ROWS, width), F32)
    grid_spec = pltpu.PrefetchScalarGridSpec(
        num_scalar_prefetch=1,
        grid=(b, NSA_GROUPS, s // TQ),
        in_specs=[
            qspec,
            pl.BlockSpec((None, None, nt, TK, ksw), lambda i, g, j, fl: (i, g, 0, 0, 0)), vspec,
            pl.BlockSpec((None, None, nt, TK, kww), lambda i, g, j, fl: (i, g, 0, 0, 0)), vspec,
            pl.BlockSpec((None, None, MAX_SEL_BLOCKS, TQ), lambda i, g, j, fl: (i, g, 0, j)),
            pl.BlockSpec((None, LANES, TQ), lambda i, g, j, fl: (i, 0, j)),
            qspec,
        ],
        out_specs=qspec,
        scratch_shapes=[pltpu.VMEM((ksw, width), BF16), stat, accs,
                        pltpu.VMEM((TK, width), F32), pltpu.VMEM((TK, width), BF16), stat,
                        pltpu.SMEM((nt,), I32)],
    )
    return pl.pallas_call(
        _nsa_selwin_kernel,
        grid_spec=grid_spec,
        out_shape=jax.ShapeDtypeStruct((b, NSA_HEADS, HEAD_DIM, s), BF16),
        compiler_params=_params("arbitrary", "arbitrary", "arbitrary"),
        name="nsa_select_window",
    )(tile_flags, qt, ks_t, vs_t, kw_t, vw_t, sel_t, gates_t, acc_t)


def _moba_kernel(q_ref, k_ref, vt_ref, kmean_ref, o_ref, qa_ref, m_ref, acc_ref,
                 s_ref, p_ref, alpha_ref, *, n_top):
    hg = pl.program_id(1)
    i = pl.program_id(2)
    nblk = kmean_ref.shape[1]
    q0 = (i * TQ).astype(F32)
    blk = lax.broadcasted_iota(I32, (nblk, 1), 0)
    for h in range(MOBA_HB):
        head = hg * MOBA_HB + h
        slope = jnp.float32(2.0 ** -MOBA_HEADS)
        for hh in range(MOBA_HEADS - 1):
            slope = jnp.where(head == hh, 2.0 ** -(hh + 1), slope)
        q = q_ref[h]
        gate = jnp.where(blk < i, _dot(kmean_ref[h].astype(BF16), q), NEG)
        bias = jnp.full(gate.shape, -BIG, F32)
        for _ in range(n_top):
            mx = jnp.max(gate, axis=0, keepdims=True)
            first = jnp.min(jnp.where(gate == mx, blk, nblk), axis=0, keepdims=True)
            pick = blk == first
            bias = jnp.where(pick, 0.0, bias)
            gate = jnp.where(pick, -jnp.inf, gate)
        bias = jnp.where(blk < i, bias, 0.0)
        if nblk < MAX_MOBA_BLOCKS:
            bias = jnp.concatenate([bias, jnp.zeros((MAX_MOBA_BLOCKS - nblk, TQ), F32)], axis=0)
        qa_ref[h, 0:HEAD_DIM, :] = q
        qa_ref[h, HEAD_DIM:HEAD_DIM + MAX_MOBA_BLOCKS, :] = _alibi_rows(slope, q0, MAX_MOBA_BLOCKS).astype(BF16)
        qa_ref[h, HEAD_DIM + MAX_MOBA_BLOCKS:, :] = bias.astype(BF16)

    def scores(j):
        return jnp.concatenate([_dot(k_ref[h, j], qa_ref[h]) for h in range(MOBA_HB)], axis=1)

    def values(j, p):
        return jnp.concatenate(
            [_dot(vt_ref[h, j], p[:, h * TQ:(h + 1) * TQ]) for h in range(MOBA_HB)], axis=1)

    pipe = _FlashPipe(scores, values, m_ref, acc_ref, s_ref, p_ref, alpha_ref)
    o = pipe.run(jnp.where(_wide_dist(MOBA_HB * TQ) >= 0, scores(i), NEG), i, i)
    for h in range(MOBA_HB):
        o_ref[h] = o[:, h * TQ:(h + 1) * TQ].astype(o_ref.dtype)


def _moba(qt, k_t, v_t, kmean):
    b, nh, _, s = qt.shape
    nt = s // TK
    n_top = min(MOBA_TOP, nt - 1)
    hb = MOBA_HB
    kw = k_t.shape[-1]
    width = hb * TQ
    return pl.pallas_call(
        functools.partial(_moba_kernel, n_top=n_top),
        grid=(b, nh // hb, s // TQ),
        in_specs=[
            pl.BlockSpec((None, hb, HEAD_DIM, TQ), lambda i, h, j: (i, h, 0, j)),
            pl.BlockSpec((None, hb, nt, TK, kw), lambda i, h, j: (i, h, 0, 0, 0)),
            pl.BlockSpec((None, hb, nt, VAL_ROWS, TK), lambda i, h, j: (i, h, 0, 0, 0)),
            pl.BlockSpec((None, hb, nt, HEAD_DIM), lambda i, h, j: (i, h, 0, 0)),
        ],
        out_specs=pl.BlockSpec((None, hb, HEAD_DIM, TQ), lambda i, h, j: (i, h, 0, j)),
        out_shape=jax.ShapeDtypeStruct((b, nh, HEAD_DIM, s), BF16),
        scratch_shapes=[
            pltpu.VMEM((hb, kw, TQ), BF16), pltpu.VMEM((1, width), F32),
            pltpu.VMEM((VAL_ROWS, width), F32),
            pltpu.VMEM((TK, width), F32), pltpu.VMEM((TK, width), BF16), pltpu.VMEM((1, width), F32),
        ],
        compiler_params=_params("arbitrary", "arbitrary", "arbitrary"),
        name="moba",
    )(qt, k_t, v_t, kmean)


def _outproj_kernel(on_ref, om_ref, w_ref, x_ref, g1_ref, ln_ref, sc_ref, sh_ref,
                    wrh_ref, wrl_ref, rb_ref, x1_ref, h2_ref, lg_ref):
    nh, dh, rows = on_ref.shape
    half = nh * dh

    def rows_major(o_ref):
        return o_ref[...].reshape(half, rows).astype(F32).T.astype(BF16)

    y = _dot(rows_major(on_ref), w_ref[:half, :]) + _dot(rows_major(om_ref), w_ref[half:, :])
    x1 = x_ref[...] + g1_ref[...] * y
    x1_ref[...] = x1
    h2 = _modulated_norm(x1, ln_ref[...], sc_ref[...], sh_ref[...])
    h2_ref[...] = h2
    hh, hl = _hi_lo(h2)
    logits = _dot(hh, wrh_ref[...]) + _dot(hh, wrl_ref[...]) + _dot(hl, wrh_ref[...]) + rb_ref[...]
    lg_ref[...] = logits.T


def _outproj(o_nsa_t, o_moba_t, w_out, x2d, g1, ln_g, sc2p, sh2, wr_hi, wr_lo, rb):
    t, d = x2d.shape
    b, nh, dh, seq = o_nsa_t.shape
    tm = TM_PROJ
    nb = seq // tm
    row = lambda i, j: (i * nb + j, 0)
    bat = lambda i, j: (i, 0, 0)
    const = lambda i, j: (0, 0)
    heads = pl.BlockSpec((None, nh, dh, tm), lambda i, j: (i, 0, 0, j))
    return pl.pallas_call(
        _outproj_kernel,
        grid=(b, nb),
        in_specs=[
            heads, heads,
            pl.BlockSpec(w_out.shape, const),
            pl.BlockSpec((tm, d), row),
            pl.BlockSpec((None, 1, d), bat),
            pl.BlockSpec((1, d), const),
            pl.BlockSpec((None, 1, d), bat), pl.BlockSpec((None, 1, d), bat),
            pl.BlockSpec(wr_hi.shape, const), pl.BlockSpec(wr_lo.shape, const),
            pl.BlockSpec((1, LANES), const),
        ],
        out_specs=[pl.BlockSpec((tm, d), row), pl.BlockSpec((tm, d), row),
                   pl.BlockSpec((LANES, tm), lambda i, j: (0, i * nb + j))],
        out_shape=[jax.ShapeDtypeStruct((t, d), F32), jax.ShapeDtypeStruct((t, d), F32),
                   jax.ShapeDtypeStruct((LANES, t), F32)],
        compiler_params=_params("arbitrary", "arbitrary"),
        name="outproj_router",
    )(o_nsa_t, o_moba_t, w_out, x2d, g1, ln_g, sc2p, sh2, wr_hi, wr_lo, rb)


def _route_kernel(lg_ref, idx_ref, gate_ref, rank_ref, cnt_ref, carry_ref):
    @pl.when(pl.program_id(0) == 0)
    def _():
        carry_ref[...] = jnp.zeros(carry_ref.shape, carry_ref.dtype)

    lg = lg_ref[...]
    ne, tt = lg.shape
    eidx = lax.broadcasted_iota(I32, lg.shape, 0)
    vals, idxs, hots = [], [], []
    for _ in range(TOP_K):
        mx = jnp.max(lg, axis=0, keepdims=True)
        first = jnp.min(jnp.where(lg == mx, eidx, ne), axis=0, keepdims=True)
        hot = eidx == first
        vals.append(mx)
        idxs.append(first)
        hots.append(hot)
        lg = jnp.where(hot, -jnp.inf, lg)
    e = [jnp.exp(v - vals[0]) for v in vals]
    tot = e[0]
    for x in e[1:]:
        tot = tot + x
    cnt = jnp.where(hots[0], 1.0, 0.0)
    for hot in hots[1:]:
        cnt = cnt + jnp.where(hot, 1.0, 0.0)
    cntb = cnt.astype(BF16)
    r = lax.broadcasted_iota(I32, (tt, tt), 0)
    c = lax.broadcasted_iota(I32, (tt, tt), 1)
    before = jnp.where(r < c, 1.0, 0.0).astype(BF16)
    carry = carry_ref[...]
    prefix = _dot(cntb, before) + carry
    ranks = [jnp.sum(jnp.where(hot, prefix, 0.0), axis=0, keepdims=True) for hot in hots]
    carry = carry + _dot(cntb, jnp.ones((tt, tt), BF16))
    carry_ref[...] = carry
    cnt_ref[...] = carry[:, :LANES]
    pad_i = jnp.zeros((SUBLANES - TOP_K, tt), I32)
    pad_f = jnp.zeros((SUBLANES - TOP_K, tt), F32)
    idx_ref[...] = jnp.concatenate(idxs + [pad_i], axis=0)
    gate_ref[...] = jnp.concatenate([x / tot for x in e] + [pad_f], axis=0)
    rank_ref[...] = jnp.concatenate([x.astype(I32) for x in ranks] + [pad_i], axis=0)


def _route(logits_t):
    ne, t = N_EXPERTS, logits_t.shape[1]
    tt = min(TT_ROUTE, t)
    tok = pl.BlockSpec((SUBLANES, tt), lambda i: (0, i))
    return pl.pallas_call(
        _route_kernel,
        grid=(t // tt,),
        in_specs=[pl.BlockSpec((ne, tt), lambda i: (0, i))],
        out_specs=[tok, tok, tok, pl.BlockSpec((ne, LANES), lambda i: (0, 0))],
        out_shape=[jax.ShapeDtypeStruct((SUBLANES, t), I32), jax.ShapeDtypeStruct((SUBLANES, t), F32),
                   jax.ShapeDtypeStruct((SUBLANES, t), I32), jax.ShapeDtypeStruct((ne, LANES), F32)],
        scratch_shapes=[pltpu.VMEM((ne, tt), F32)],
        compiler_params=_params("arbitrary"),
        name="route_topk_rank",
    )(logits_t)


def _dispatch_plan(idx, rank, counts):
    tm = TM_MOE
    t = idx.shape[1]
    nb = (t * TOP_K + N_EXPERTS * tm) // tm
    experts = jnp.arange(N_EXPERTS, dtype=I32)
    counts = counts.astype(I32)
    padded = (counts + tm - 1) // tm * tm
    pend = jnp.sum(jnp.where(experts[None, :] <= experts[:, None], padded[None, :], 0), axis=1)
    pstart = pend - padded
    base = jnp.sum(jnp.where(idx[:, :, None] == experts, pstart, 0), axis=-1)
    dest = (rank + base).T.reshape(t * TOP_K).astype(I32)
    blk_e = jnp.sum(jnp.where(pend[None, :] <= (jnp.arange(nb, dtype=I32) * tm)[:, None], 1, 0), axis=1)
    blk_e = jnp.minimum(blk_e, N_EXPERTS - 1).astype(I32)
    nused = (pend[-1] // tm).astype(I32).reshape(1)
    fill_start = jnp.maximum(pend - tm, 0).astype(I32)
    fill_flag = (padded > counts).astype(I32)
    return dest, blk_e, nused, fill_start, fill_flag


def _dispatch_kernel(fill_start_ref, fill_flag_ref, nused_ref, dest_ref, h_ref, xs_hbm, zbuf, zsem, sem):
    i = pl.program_id(0)
    nrow = dest_ref.shape[1]
    ntok = nrow // TOP_K

    @pl.when(i == 0)
    def _():
        zbuf[...] = jnp.zeros(zbuf.shape, zbuf.dtype)

        def fill(start):
            return pltpu.make_async_copy(zbuf, xs_hbm.at[pl.ds(pl.multiple_of(start, TM_MOE), TM_MOE)], zsem)

        for e in range(N_EXPERTS):
            @pl.when(fill_flag_ref[e] != 0)
            def _():
                fill(fill_start_ref[e]).start()
        for e in range(N_EXPERTS):
            @pl.when(fill_flag_ref[e] != 0)
            def _():
                fill(fill_start_ref[e]).wait()

        nblk = xs_hbm.shape[0] // TM_MOE

        def tail_start(b, c):
            fill(b * TM_MOE).start()
            return c

        def tail_wait(b, c):
            fill(b * TM_MOE).wait()
            return c

        lax.fori_loop(nused_ref[0], nblk, tail_start, 0)
        lax.fori_loop(nused_ref[0], nblk, tail_wait, 0)

    def issue(tl, c):
        src = h_ref.at[pl.ds(tl, 1)]
        for k in range(TOP_K):
            pltpu.make_async_copy(src, xs_hbm.at[pl.ds(dest_ref[0, tl * TOP_K + k], 1)], sem).start(
                priority=k % 2)
        return c

    lax.fori_loop(0, ntok, issue, 0)
    for _ in range(TOP_K):
        pltpu.make_async_copy(h_ref, xs_hbm.at[pl.ds(0, ntok)], sem).wait()


def _dispatch(fill_start, fill_flag, nused, dest, h2):
    t, d = h2.shape
    td = min(TD_DISPATCH, t)
    r = t * TOP_K + N_EXPERTS * TM_MOE
    grid_spec = pltpu.PrefetchScalarGridSpec(
        num_scalar_prefetch=3,
        grid=(t // td,),
        in_specs=[
            pl.BlockSpec((None, 1, td * TOP_K), lambda i, fs, ff, nu: (i, 0, 0), memory_space=pltpu.SMEM),
            pl.BlockSpec((td, d), lambda i, fs, ff, nu: (i, 0)),
        ],
        out_specs=pl.BlockSpec(memory_space=pl.ANY),
        scratch_shapes=[pltpu.VMEM((TM_MOE, d), F32), pltpu.SemaphoreType.DMA(()), pltpu.SemaphoreType.DMA(())],
    )
    return pl.pallas_call(
        _dispatch_kernel,
        grid_spec=grid_spec,
        out_shape=jax.ShapeDtypeStruct((r, d), F32),
        compiler_params=_params("arbitrary"),
        name="moe_dispatch",
    )(fill_start, fill_flag, nused, dest.reshape(t // td, 1, td * TOP_K), h2)


def _moe_ffn_kernel(blk_e_ref, nused_ref, x_ref, w1_ref, b1_ref, w2_ref, b2_ref, y_ref, w1b, w2b):
    i = pl.program_id(0)
    dff = w2_ref.shape[0]

    @pl.when(i < nused_ref[0])
    def _():
        prev = blk_e_ref[jnp.maximum(i - 1, 0)]

        @pl.when((i == 0) | (blk_e_ref[i] != prev))
        def _():
            w1b[...] = w1_ref[...].astype(BF16)
            w2b[...] = w2_ref[...].astype(BF16)

        u = _dot(x_ref[...].astype(BF16), w1b[...]) + b1_ref[...]
        gl = jnp.minimum(u[:, :dff], SWIGLU_LIMIT)
        lin = jnp.clip(u[:, dff:], -SWIGLU_LIMIT, SWIGLU_LIMIT)
        act = (lin + 1.0) * gl * _sigmoid(SWIGLU_ALPHA * gl)
        y_ref[...] = _dot(act.astype(BF16), w2b[...]) + b2_ref[...]

    @pl.when(i >= nused_ref[0])
    def _():
        y_ref[...] = jnp.zeros(y_ref.shape, y_ref.dtype)


def _moe_ffn(blk_e, nused, xs, w1, b1, w2, b2, layer):
    r, d = xs.shape
    depth, ne, _, f2 = w1.shape
    dff = f2 // 2
    tm = TM_MOE
    rows = lambda i, be, nu: (jnp.minimum(i, nu[0] - 1), 0)
    expert = lambda i, be, nu: (layer, be[i], 0, 0)
    grid_spec = pltpu.PrefetchScalarGridSpec(
        num_scalar_prefetch=2,
        grid=(r // tm,),
        in_specs=[
            pl.BlockSpec((tm, d), rows),
            pl.BlockSpec((None, None, d, f2), expert),
            pl.BlockSpec((None, None, 1, f2), expert),
            pl.BlockSpec((None, None, dff, d), expert),
            pl.BlockSpec((None, None, 1, d), expert),
        ],
        out_specs=pl.BlockSpec((tm, d), lambda i, be, nu: (i, 0)),
        scratch_shapes=[pltpu.VMEM((d, f2), BF16), pltpu.VMEM((dff, d), BF16)],
    )
    return pl.pallas_call(
        _moe_ffn_kernel,
        grid_spec=grid_spec,
        out_shape=jax.ShapeDtypeStruct((r, d), F32),
        compiler_params=_params("arbitrary"),
        name="moe_ffn",
    )(blk_e, nused, xs, w1, b1.reshape(depth, ne, 1, f2), w2, b2.reshape(depth, ne, 1, d))


def _moe_combine_kernel(pos_ref, nxt_ref, y_hbm, gate_ref, x1_ref, g2_ref, o_ref, buf, sems):
    i = pl.program_id(0)
    n = pl.num_programs(0)
    tc = x1_ref.shape[0]
    slot = i % 2

    def fetch(p_ref, s):
        def issue(tl, c):
            for k in range(TOP_K):
                pltpu.make_async_copy(y_hbm.at[pl.ds(p_ref[0, tl * TOP_K + k], 1)],
                                      buf.at[s, k, pl.ds(tl, 1)], sems.at[s]).start(priority=k % 2)
            return c
        lax.fori_loop(0, tc, issue, 0)

    @pl.when(i == 0)
    def _():
        fetch(pos_ref, 0)

    @pl.when(i + 1 < n)
    def _():
        fetch(nxt_ref, 1 - slot)

    for k in range(TOP_K):
        pltpu.make_async_copy(y_hbm.at[pl.ds(0, tc)], buf.at[slot, k], sems.at[slot]).wait()
    gate = gate_ref[...]
    moe = gate[:, 0:1] * buf[slot, 0]
    for k in range(1, TOP_K):
        moe = moe + gate[:, k:k + 1] * buf[slot, k]
    o_ref[...] = x1_ref[...] + g2_ref[...] * moe


def _moe_combine(pos, y, gate, x1, g2, seq):
    t, d = x1.shape
    tc = TC_COMB
    nb = seq // tc
    n = t // tc
    pos3 = pos.reshape(n, 1, tc * TOP_K)
    return pl.pallas_call(
        _moe_combine_kernel,
        grid=(n,),
        in_specs=[
            pl.BlockSpec((None, 1, tc * TOP_K), lambda i: (i, 0, 0), memory_space=pltpu.SMEM),
            pl.BlockSpec((None, 1, tc * TOP_K), lambda i: (jnp.minimum(i + 1, n - 1), 0, 0),
                         memory_space=pltpu.SMEM),
            pl.BlockSpec(memory_space=pl.ANY),
            pl.BlockSpec((tc, TOP_K), lambda i: (i, 0)),
            pl.BlockSpec((tc, d), lambda i: (i, 0)),
            pl.BlockSpec((None, 1, d), lambda i: (i // nb, 0, 0)),
        ],
        out_specs=pl.BlockSpec((tc, d), lambda i: (i, 0)),
        out_shape=jax.ShapeDtypeStruct((t, d), F32),
        scratch_shapes=[pltpu.VMEM((2, TOP_K, tc, d), F32), pltpu.SemaphoreType.DMA((2,))],
        compiler_params=_params("arbitrary"),
        name="moe_combine",
    )(pos3, pos3, y, gate, x1, g2)


def _tile_gain(gain, width, scale=1.0):
    return (jnp.tile(gain, width // HEAD_DIM) * scale).reshape(1, width).astype(F32)


def _layer(x2d, mod, b, s, ln1_g, ln2_g, w_in, nsa_q_gain, nsa_k_gain, cmp_pos, cmp_w1, cmp_w2,
           moba_q_gain, moba_k_gain, w_out, router_w, router_b, exp_w1, exp_b1, exp_w2, exp_b2, layer):
    t, d = x2d.shape
    sh1, sc1, g1, sh2, sc2, g2 = [m.reshape(b, 1, d) for m in jnp.split(mod, 6, axis=-1)]
    scale = HEAD_DIM ** -0.5

    (qt, kcv, ks_t, vs_t, kw_t, vw_t, gates_t, mq_t, mk_t, mv_t, kmean) = _inproj(
        x2d, ln1_g.reshape(1, d), 1.0 + sc1, sh1, _pack_w_in(w_in),
        _tile_gain(nsa_q_gain, 512, scale), _tile_gain(nsa_k_gain, 128),
        _tile_gain(moba_q_gain, 512, scale), _tile_gain(moba_k_gain, 512), b, s)

    g = NSA_GROUPS
    xc = kcv.reshape(b, s, 2, g, HEAD_DIM).transpose(0, 2, 3, 1, 4).reshape(
        b, 2, g, s // CMP_STRIDE, CMP_STRIDE * HEAD_DIM)
    cmp = _compress(xc, cmp_pos.reshape(2, 1, CMP_LEN * HEAD_DIM), cmp_w1.astype(BF16),
                    cmp_w2.astype(BF16), nsa_k_gain.reshape(1, HEAD_DIM))
    kc = cmp[:, 0].astype(BF16)
    vct = cmp[:, 1].transpose(0, 1, 3, 2).astype(BF16)
    acc_t, sel_t, tile_hits = _nsa_cmp(qt, kc, vct, gates_t)
    nsel = sel_t.shape[2]
    if nsel < MAX_SEL_BLOCKS:
        sel_t = jnp.concatenate([sel_t, jnp.ones((b, g, MAX_SEL_BLOCKS - nsel, s), F32)], axis=2)
    tile_flags = (tile_hits[..., 0] > 0).astype(I32).reshape(-1)
    o_nsa_t = _nsa_selwin(tile_flags, qt, ks_t, vs_t, kw_t, vw_t, sel_t, gates_t, acc_t)

    km = kmean.reshape(b, s // MOBA_BLOCK, MOBA_HEADS, HEAD_DIM).transpose(0, 2, 1, 3)
    o_moba_t = _moba(mq_t, mk_t, mv_t, km)

    wr = jnp.zeros((d, LANES), F32).at[:, :N_EXPERTS].set(router_w)
    wr_hi = wr.astype(BF16)
    wr_lo = (wr - wr_hi.astype(F32)).astype(BF16)
    rb = jnp.zeros((1, LANES), F32).at[0, :N_EXPERTS].set(router_b)
    x1, h2, logits_t = _outproj(o_nsa_t, o_moba_t, w_out.astype(BF16), x2d, g1, ln2_g.reshape(1, d),
                                1.0 + sc2, sh2, wr_hi, wr_lo, rb)

    idx8, gate8, rank8, counts = _route(logits_t)
    dest, blk_e, nused, fill_start, fill_flag = _dispatch_plan(idx8[:TOP_K], rank8[:TOP_K], counts[:, 0])
    xs = _dispatch(fill_start, fill_flag, nused, dest, h2)
    y = _moe_ffn(blk_e, nused, xs, exp_w1, exp_b1, exp_w2, exp_b2, layer)
    return _moe_combine(dest, y, gate8[:TOP_K].T, x1, g2, s)


def kernel(x, c, ada_w, ada_b, ln1_g, ln2_g, w_in, nsa_q_gain, nsa_k_gain, nsa_cmp_pos, nsa_cmp_w1,
           nsa_cmp_w2, moba_q_gain, moba_k_gain, w_out, router_w, router_b, exp_w1, exp_b1, exp_w2, exp_b2):
    b, s, d = x.shape
    depth = ada_w.shape[0]
    mods = _adaln(c, ada_w, ada_b)
    x2d = x.reshape(b * s, d)
    for l in range(depth):
        x2d = _layer(x2d, mods[l], b, s, ln1_g[l], ln2_g[l], w_in[l], nsa_q_gain[l], nsa_k_gain[l],
                     nsa_cmp_pos[l], nsa_cmp_w1[l], nsa_cmp_w2[l], moba_q_gain[l], moba_k_gain[l],
                     w_out[l], router_w[l], router_b[l], exp_w1, exp_b1, exp_w2, exp_b2, l)
    return x2d.reshape(b, s, d)
```

```python
import functools

import numpy as np
import jax
import jax.numpy as jnp
from jax import lax
from jax.experimental import pallas as pl
from jax.experimental.pallas import tpu as pltpu

F32 = jnp.float32
BF16 = jnp.bfloat16
I32 = jnp.int32

HEAD_DIM = 64
NSA_HEADS = 8
NSA_GROUPS = 2
NSA_HPG = NSA_HEADS // NSA_GROUPS
CMP_LEN = 32
CMP_STRIDE = 16
CMP_HID = 128
SEL_BLOCK = 64
SEL_TOP = 16
WINDOW = 512
MOBA_HEADS = 8
MOBA_BLOCK = 256
MOBA_TOP = 3
N_EXPERTS = 32
TOP_K = 4
SWIGLU_LIMIT = 7.0
SWIGLU_ALPHA = 1.702
EPS = 1e-6
NEG = -1e30
FORCE_SCORE = 1e4

LANES = 128
SUBLANES = 8
TQ = 256
TK = 256
POS_SPLIT = 256
BIG = 1e30
MAX_SEL_BLOCKS = 128
MAX_MOBA_BLOCKS = 32
MOBA_HB = 4
VAL_ROWS = HEAD_DIM + 16
RANK_CHUNK = 32
TM_PROJ = 512
TM_MOE = 512
TC_COMB = 128
TT_ROUTE = 512
TD_DISPATCH = 512
VMEM_LIMIT = 56 * 1024 * 1024


def _dot(a, b):
    return jnp.dot(a, b, preferred_element_type=F32)


def _hi_lo(x):
    hi = x.astype(BF16)
    lo = (x - hi.astype(F32)).astype(BF16)
    return hi, lo


def _sigmoid(x):
    return 1.0 / (1.0 + jnp.exp(-x))


def _params(*sem):
    return pltpu.CompilerParams(dimension_semantics=sem, vmem_limit_bytes=VMEM_LIMIT)


def _adaln_kernel(c_ref, w_ref, b_ref, o_ref):
    c = c_ref[...]
    cond = c * _sigmoid(c)
    ch, cl = _hi_lo(cond)
    wh, wl = _hi_lo(w_ref[...])
    o_ref[...] = _dot(ch, wh) + _dot(ch, wl) + _dot(cl, wh) + b_ref[...]


def _adaln(c, ada_w, ada_b):
    depth, d, n6 = ada_w.shape
    b = c.shape[0]
    tn = 1024
    c_pad = jnp.zeros((SUBLANES, d), F32).at[:b].set(c)
    out = pl.pallas_call(
        _adaln_kernel,
        grid=(depth, n6 // tn),
        in_specs=[
            pl.BlockSpec((SUBLANES, d), lambda l, j: (0, 0)),
            pl.BlockSpec((None, d, tn), lambda l, j: (l, 0, j)),
            pl.BlockSpec((None, 1, tn), lambda l, j: (l, 0, j)),
        ],
        out_specs=pl.BlockSpec((None, SUBLANES, tn), lambda l, j: (l, 0, j)),
        out_shape=jax.ShapeDtypeStruct((depth, SUBLANES, n6), F32),
        compiler_params=_params("arbitrary", "arbitrary"),
        name="adaln",
    )(c_pad, ada_w, ada_b.reshape(depth, 1, n6))
    return out[:, :b]


def _block_diag_ones():
    r = lax.broadcasted_iota(I32, (LANES, LANES), 0) // HEAD_DIM
    c = lax.broadcasted_iota(I32, (LANES, LANES), 1) // HEAD_DIM
    return jnp.where(r == c, 1.0, 0.0).astype(BF16)


def _head_norm(z, bd, gain):
    cols = []
    for c0 in range(0, z.shape[1], LANES):
        zc = z[:, c0:c0 + LANES]
        hi, lo = _hi_lo(zc * zc)
        ss = _dot(hi, bd) + _dot(lo, bd)
        cols.append(zc * lax.rsqrt(ss * (1.0 / HEAD_DIM) + EPS))
    zn = cols[0] if len(cols) == 1 else jnp.concatenate(cols, axis=1)
    return zn * gain


def _modulated_norm(x, g, sc1p, sh):
    ms = jnp.mean(x * x, axis=-1, keepdims=True)
    y = x * lax.rsqrt(ms + EPS)
    return (y * g) * sc1p + sh


def _heads_transposed(z, nh):
    return z.T.astype(BF16).reshape(nh, HEAD_DIM, z.shape[0])


def _store_value_tiles(o_ref, z, nh):
    zt = z.T.astype(BF16)
    extra = VAL_ROWS - HEAD_DIM
    ones_row = jnp.where(lax.broadcasted_iota(I32, (extra, TK), 0) == 0, 1.0, 0.0).astype(BF16)
    for h in range(nh):
        for t in range(z.shape[0] // TK):
            vt = zt[h * HEAD_DIM:(h + 1) * HEAD_DIM, t * TK:(t + 1) * TK]
            o_ref[h, t] = jnp.concatenate([vt, ones_row], axis=0)


def _store_key_tiles(o_ref, z, extra, nh):
    rows = z.shape[0]
    for h in range(nh):
        ka = jnp.concatenate([z[:, h * HEAD_DIM:(h + 1) * HEAD_DIM], extra], axis=1).astype(BF16)
        o_ref[h] = ka.reshape(rows // TK, TK, ka.shape[1])


def _pos_cols(pos, width):
    col = lax.broadcasted_iota(I32, (pos.shape[0], width), 1)
    hi = lax.shift_right_logical(pos, POS_SPLIT.bit_length() - 1).astype(F32)
    lo = (pos & (POS_SPLIT - 1)).astype(F32)
    return jnp.where(col == 0, hi, jnp.where(col == 1, lo, jnp.where(col == 2, 1.0, 0.0)))


def _onehot_cols(pos, block, width):
    col = lax.broadcasted_iota(I32, (pos.shape[0], width), 1)
    return jnp.where(lax.shift_right_logical(pos, block.bit_length() - 1) == col, 1.0, 0.0)


def _inproj_kernel(x_ref, g_ref, sc_ref, sh_ref, w_ref, gq_ref, gk_ref, gmq_ref, gmk_ref,
                   q_ref, kcv_ref, ks_ref, vs_ref, kw_ref, vw_ref, gate_ref,
                   mq_ref, mk_ref, mv_ref, kmean_ref):
    h = _modulated_norm(x_ref[...], g_ref[...], sc_ref[...], sh_ref[...])
    hb = h.astype(BF16)
    bd = _block_diag_ones()
    rows = hb.shape[0]
    pos = pl.program_id(1) * rows + lax.broadcasted_iota(I32, (rows, 1), 0)

    def proj(a, b):
        return _dot(hb, w_ref[:, a:b])

    q_ref[...] = _heads_transposed(_head_norm(proj(0, 512), bd, gq_ref[...]), NSA_HEADS)
    kv = proj(512, 1280)
    kcv_ref[...] = kv[:, 0:256]
    pos_cols = _pos_cols(pos, HEAD_DIM)
    sel_cols = jnp.concatenate([pos_cols, _onehot_cols(pos, SEL_BLOCK, MAX_SEL_BLOCKS)], axis=1)
    _store_key_tiles(ks_ref, _head_norm(kv[:, 256:384], bd, gk_ref[...]), sel_cols, NSA_GROUPS)
    _store_value_tiles(vs_ref, kv[:, 384:512], NSA_GROUPS)
    _store_key_tiles(kw_ref, _head_norm(kv[:, 512:640], bd, gk_ref[...]), pos_cols, NSA_GROUPS)
    _store_value_tiles(vw_ref, kv[:, 640:768], NSA_GROUPS)
    gate_ref[...] = _sigmoid(proj(1280, 1408)).T
    mq_ref[...] = _heads_transposed(_head_norm(proj(1408, 1920), bd, gmq_ref[...]), MOBA_HEADS)
    mkn = _head_norm(proj(1920, 2432), bd, gmk_ref[...])
    moba_cols = jnp.concatenate([_pos_cols(pos, MAX_MOBA_BLOCKS),
                                 _onehot_cols(pos, MOBA_BLOCK, MAX_MOBA_BLOCKS)], axis=1)
    _store_key_tiles(mk_ref, mkn, moba_cols, MOBA_HEADS)
    _store_value_tiles(mv_ref, proj(2432, 2944), MOBA_HEADS)
    means = [jnp.sum(mkn[r:r + MOBA_BLOCK], axis=0, keepdims=True) * (1.0 / MOBA_BLOCK)
             for r in range(0, rows, MOBA_BLOCK)]
    kmean_ref[...] = jnp.concatenate(means, axis=0)


def _pack_w_in(w_in):
    d = w_in.shape[0]
    wg = jnp.zeros((d, LANES), w_in.dtype).at[:, :3 * NSA_HEADS].set(w_in[:, 1280:1304])
    return jnp.concatenate([w_in[:, :1280], wg, w_in[:, 1304:]], axis=1).astype(BF16)


def _inproj(x2d, ln_g, sc1p, sh, w_pack, gq, gk, gmq, gmk, b, seq):
    t, d = x2d.shape
    tm = TM_PROJ
    nb = seq // tm
    nt = seq // TK
    tpt = tm // TK
    g = NSA_GROUPS
    row = lambda i, j: (i * nb + j, 0)
    bat = lambda i, j: (i, 0, 0)
    const = lambda i, j: (0, 0)
    heads_t = lambda nh: pl.BlockSpec((None, nh, HEAD_DIM, tm), lambda i, j: (i, 0, 0, j))
    key_tiles = lambda nh, w: pl.BlockSpec((None, nh, tpt, TK, w), lambda i, j: (i, 0, j, 0, 0))
    val_tiles = lambda nh: pl.BlockSpec((None, nh, tpt, VAL_ROWS, TK), lambda i, j: (i, 0, j, 0, 0))
    sel_w = 2 * HEAD_DIM + MAX_SEL_BLOCKS
    win_w = 2 * HEAD_DIM
    moba_w = HEAD_DIM + 2 * MAX_MOBA_BLOCKS
    nmean = tm // MOBA_BLOCK
    sds = jax.ShapeDtypeStruct
    out_shape = [
        sds((b, NSA_HEADS, HEAD_DIM, seq), BF16), sds((t, 256), F32),
        sds((b, g, nt, TK, sel_w), BF16), sds((b, g, nt, VAL_ROWS, TK), BF16),
        sds((b, g, nt, TK, win_w), BF16), sds((b, g, nt, VAL_ROWS, TK), BF16),
        sds((b, LANES, seq), F32),
        sds((b, MOBA_HEADS, HEAD_DIM, seq), BF16),
        sds((b, MOBA_HEADS, nt, TK, moba_w), BF16), sds((b, MOBA_HEADS, nt, VAL_ROWS, TK), BF16),
        sds((t // tm, nmean, 512), F32),
    ]
    out_specs = [
        heads_t(NSA_HEADS), pl.BlockSpec((tm, 256), row),
        key_tiles(g, sel_w), val_tiles(g), key_tiles(g, win_w), val_tiles(g),
        pl.BlockSpec((None, LANES, tm), lambda i, j: (i, 0, j)),
        heads_t(MOBA_HEADS), key_tiles(MOBA_HEADS, moba_w), val_tiles(MOBA_HEADS),
        pl.BlockSpec((None, nmean, 512), lambda i, j: (i * nb + j, 0, 0)),
    ]
    return pl.pallas_call(
        _inproj_kernel,
        grid=(b, nb),
        in_specs=[
            pl.BlockSpec((tm, d), row),
            pl.BlockSpec((1, d), const),
            pl.BlockSpec((None, 1, d), bat),
            pl.BlockSpec((None, 1, d), bat),
            pl.BlockSpec(w_pack.shape, const),
            pl.BlockSpec((1, 512), const),
            pl.BlockSpec((1, 128), const),
            pl.BlockSpec((1, 512), const),
            pl.BlockSpec((1, 512), const),
        ],
        out_specs=out_specs,
        out_shape=out_shape,
        compiler_params=_params("arbitrary", "arbitrary"),
        name="inproj",
    )(x2d, ln_g, sc1p, sh, w_pack, gq, gk, gmq, gmk)


def _compress_kernel(x_ref, pos_ref, w1_ref, w2_ref, gk_ref, o_ref):
    kv = pl.program_id(1)
    half = CMP_STRIDE * HEAD_DIM
    x = x_ref[...]
    xl = (x + pos_ref[:, :half]).astype(BF16)
    xh = (x + pos_ref[:, half:]).astype(BF16)
    a = _dot(xl, w1_ref[:half, :])
    b = _dot(xh, w1_ref[half:, :])
    hid = a + pltpu.roll(b, b.shape[0] - 1, 0)
    out = _dot(jax.nn.gelu(hid, approximate=True).astype(BF16), w2_ref[...])
    ms = jnp.mean(out * out, axis=-1, keepdims=True)
    normed = (out * lax.rsqrt(ms + EPS)) * gk_ref[...]
    o_ref[...] = jnp.where(kv == 0, normed, out)


def _compress(xc, pos, w1, w2, gk):
    b, _, g, nr, w = xc.shape
    return pl.pallas_call(
        _compress_kernel,
        grid=(b, 2, g),
        in_specs=[
            pl.BlockSpec((None, None, None, nr, w), lambda i, k, j: (i, k, j, 0, 0)),
            pl.BlockSpec((None, 1, w * 2), lambda i, k, j: (k, 0, 0)),
            pl.BlockSpec((None, w * 2, CMP_HID), lambda i, k, j: (k, 0, 0)),
            pl.BlockSpec((None, CMP_HID, HEAD_DIM), lambda i, k, j: (k, 0, 0)),
            pl.BlockSpec((1, HEAD_DIM), lambda i, k, j: (0, 0)),
        ],
        out_specs=pl.BlockSpec((None, None, None, nr, HEAD_DIM), lambda i, k, j: (i, k, j, 0, 0)),
        out_shape=jax.ShapeDtypeStruct((b, 2, g, nr, HEAD_DIM), F32),
        compiler_params=_params("arbitrary", "arbitrary", "arbitrary"),
        name="nsa_compress",
    )(xc, pos, w1, w2, gk)


def _nsa_slope(g, h):
    return 2.0 ** (-(g * NSA_HPG + h + 1))


def _rank_counts(imp_ref, cnt_ref, last_blk):
    nb = imp_ref.shape[0]
    chunk = min(RANK_CHUNK, nb)
    cnt_ref[...] = jnp.zeros(cnt_ref.shape, cnt_ref.dtype)
    sub = lax.broadcasted_iota(I32, (SUBLANES, imp_ref.shape[1]), 0)
    for c in range(nb // chunk):
        for d in range(nb // chunk):
            @pl.when(max(c, d) * chunk <= last_blk)
            def _():
                rows0 = range(d * chunk, (d + 1) * chunk, SUBLANES)
                pieces = [imp_ref[r:r + SUBLANES] for r in rows0]
                counts = [jnp.zeros(p.shape, I32) for p in pieces]
                for jp in range(c * chunk, (c + 1) * chunk):
                    row = imp_ref[jp:jp + 1]
                    for k, r0 in enumerate(rows0):
                        p = pieces[k]
                        if r0 + SUBLANES - 1 < jp:
                            inc = jnp.where(row > p, 1, 0)
                        elif r0 > jp:
                            inc = jnp.where(row >= p, 1, 0)
                        else:
                            inc = jnp.where(sub > jp - r0, jnp.where(row >= p, 1, 0), jnp.where(row > p, 1, 0))
                        counts[k] = counts[k] + inc
                rows = slice(d * chunk, (d + 1) * chunk)
                cnt_ref[rows] = cnt_ref[rows] + jnp.concatenate(counts, axis=0)


def _nsa_cmp_kernel(q_ref, kc_ref, vct_ref, inct_ref, gate_ref, acc_ref, sel_ref, flag_ref,
                    imp_ref, cnt_ref, *, n_top):
    g = pl.program_id(1)
    q0 = pl.program_id(2) * TQ
    nc = kc_ref.shape[0]
    nsel = inct_ref.shape[0]
    t = q0 + lax.broadcasted_iota(I32, (1, TQ), 1)
    width = NSA_HPG * TQ
    tw = q0 + (lax.broadcasted_iota(I32, (1, width), 1) & (TQ - 1))
    n = lax.broadcasted_iota(I32, (nc, 1), 0)
    center = (CMP_STRIDE * n).astype(F32) + 0.5 * (CMP_LEN - 1)
    dist = tw.astype(F32) - center
    mask = (CMP_STRIDE * n + (CMP_LEN - 1)) <= tw
    slopes = jnp.concatenate(
        [jnp.full((1, TQ), 1.0, F32) * jnp.where(g == 0, _nsa_slope(0, h), _nsa_slope(1, h))
         for h in range(NSA_HPG)], axis=1)
    qw = jnp.concatenate([q_ref[h] for h in range(NSA_HPG)], axis=1)
    s = _dot(kc_ref[...], qw) - slopes * dist
    s = jnp.where(mask, s, NEG)
    m = jnp.max(s, axis=0, keepdims=True)
    e = jnp.exp(s - m)
    l = jnp.sum(e, axis=0, keepdims=True)
    p = jnp.where(mask, e, 0.0) * (1.0 / l)
    o = _dot(vct_ref[...], p.astype(BF16))
    psum = jnp.zeros((nc, TQ), F32)
    for h in range(NSA_HPG):
        lanes = slice(h * TQ, (h + 1) * TQ)
        gate = gate_ref[pl.ds(g * (3 * NSA_HPG) + 3 * h, 1), :]
        acc_ref[h] = gate * o[:, lanes]
        psum = psum + p[:, lanes]
    hi, lo = _hi_lo(psum)
    inct = inct_ref[...]
    imp = _dot(inct, hi) + _dot(inct, lo)
    blk = lax.broadcasted_iota(I32, (nsel, 1), 0)
    cur = t // SEL_BLOCK
    forced = (blk == 0) | (blk == cur) | (blk == cur - 1)
    imp = jnp.where(forced, FORCE_SCORE, imp)
    imp_ref[...] = jnp.where(blk <= cur, imp, NEG)
    _rank_counts(imp_ref, cnt_ref, (q0 + TQ - 1) // SEL_BLOCK)
    chosen = (cnt_ref[...] < n_top) & (blk <= cur)
    sel = jnp.where(chosen, 1.0, 0.0)
    sel_ref[...] = sel
    bpt = TK // SEL_BLOCK
    ntile = nsel // bpt
    trow = lax.broadcasted_iota(I32, (ntile, nsel), 0)
    tcol = lax.broadcasted_iota(I32, (ntile, nsel), 1)
    member = jnp.where(lax.shift_right_logical(tcol, bpt.bit_length() - 1) == trow, 1.0, 0.0).astype(BF16)
    hits = jnp.sum(_dot(member, sel.astype(BF16)), axis=1, keepdims=True)
    flag_ref[...] = jnp.broadcast_to(hits, (ntile, LANES))


def _incidence_t(nc, nsel):
    cmp_start = CMP_STRIDE * np.arange(nc)
    sel_start = SEL_BLOCK * np.arange(nsel)
    inc = ((cmp_start[:, None] <= sel_start[None, :] + SEL_BLOCK - 1)
           & (cmp_start[:, None] + CMP_LEN - 1 >= sel_start[None, :]))
    inc[nc - 1] = False
    return jnp.asarray(inc.T, dtype=BF16)


def _nsa_cmp(qt, kc, vct, gates_t):
    b, _, _, s = qt.shape
    nc = kc.shape[2]
    nsel = s // SEL_BLOCK
    n_top = min(SEL_TOP, nsel)
    ntile = s // TK
    inct = _incidence_t(nc, nsel)
    return pl.pallas_call(
        functools.partial(_nsa_cmp_kernel, n_top=n_top),
        grid=(b, NSA_GROUPS, s // TQ),
        in_specs=[
            pl.BlockSpec((None, NSA_HPG, HEAD_DIM, TQ), lambda i, g, j: (i, g, 0, j)),
            pl.BlockSpec((None, None, nc, HEAD_DIM), lambda i, g, j: (i, g, 0, 0)),
            pl.BlockSpec((None, None, HEAD_DIM, nc), lambda i, g, j: (i, g, 0, 0)),
            pl.BlockSpec((nsel, nc), lambda i, g, j: (0, 0)),
            pl.BlockSpec((None, LANES, TQ), lambda i, g, j: (i, 0, j)),
        ],
        out_specs=[
            pl.BlockSpec((None, NSA_HPG, HEAD_DIM, TQ), lambda i, g, j: (i, g, 0, j)),
            pl.BlockSpec((None, None, nsel, TQ), lambda i, g, j: (i, g, 0, j)),
            pl.BlockSpec((None, None, None, ntile, LANES), lambda i, g, j: (i, g, j, 0, 0)),
        ],
        out_shape=[
            jax.ShapeDtypeStruct((b, NSA_HEADS, HEAD_DIM, s), F32),
            jax.ShapeDtypeStruct((b, NSA_GROUPS, nsel, s), F32),
            jax.ShapeDtypeStruct((b, NSA_GROUPS, s // TQ, ntile, LANES), F32),
        ],
        scratch_shapes=[pltpu.VMEM((nsel, TQ), F32), pltpu.VMEM((nsel, TQ), I32)],
        compiler_params=_params("arbitrary", "arbitrary", "arbitrary"),
        name="nsa_cmp_select",
    )(qt, kc, vct, inct, gates_t)


def _alibi_rows(slope, q0, rows):
    r = lax.broadcasted_iota(I32, (rows, TQ), 0)
    return jnp.where(r == 0, slope * POS_SPLIT, jnp.where(r == 1, slope, jnp.where(r == 2, -slope * q0, 0.0)))


def _wide_dist(width):
    lane = lax.broadcasted_iota(I32, (TK, width), 1) & (TQ - 1)
    return lane - lax.broadcasted_iota(I32, (TK, width), 0)


def _exp_weights(s, m):
    return jnp.exp((s - m).astype(BF16))


class _FlashPipe:
    def __init__(self, scores, values, m_ref, acc_ref, s_ref, p_ref, alpha_ref, tile_of=lambda j: j):
        self.scores, self.values, self.tile_of = scores, values, tile_of
        self.m_ref, self.acc_ref = m_ref, acc_ref
        self.s_ref, self.p_ref, self.alpha_ref = s_ref, p_ref, alpha_ref

    def run(self, first_scores, first_tile, n_past):
        m = jnp.max(first_scores, axis=0, keepdims=True)
        self.m_ref[...] = m
        self.p_ref[...] = _exp_weights(first_scores, m)
        self.alpha_ref[...] = jnp.ones(self.alpha_ref.shape, F32)
        self.acc_ref[...] = jnp.zeros(self.acc_ref.shape, F32)

        @pl.when(n_past > 0)
        def _():
            self.s_ref[...] = self.scores(self.tile_of(0))

        def step(j, carry):
            pending = jnp.where(j == 0, first_tile, self.tile_of(jnp.maximum(j - 1, 0)))
            s = self.s_ref[...]
            self.s_ref[...] = self.scores(self.tile_of(jnp.minimum(j + 1, n_past - 1)))
            pv = self.values(pending, self.p_ref[...])
            self.acc_ref[...] = self.alpha_ref[...] * self.acc_ref[...] + pv
            m_old = self.m_ref[...]
            m_new = jnp.maximum(m_old, jnp.max(s, axis=0, keepdims=True))
            self.m_ref[...] = m_new
            self.alpha_ref[...] = jnp.exp(m_old - m_new)
            self.p_ref[...] = _exp_weights(s, m_new)
            return carry

        lax.fori_loop(0, n_past, step, 0)
        pending = jnp.where(n_past == 0, first_tile, self.tile_of(jnp.maximum(n_past - 1, 0)))
        acc = self.alpha_ref[...] * self.acc_ref[...] + self.values(pending, self.p_ref[...])
        return acc[:HEAD_DIM] / acc[HEAD_DIM:HEAD_DIM + 1]


def _nsa_selwin_kernel(flags_ref, q_ref, ks_ref, vst_ref, kw_ref, vwt_ref, sel_ref, gate_ref, accin_ref,
                       o_ref, qa_ref, ms_ref, as_ref, s_ref, p_ref, alpha_ref, visit_ref):
    g = pl.program_id(1)
    i = pl.program_id(2)
    ntile = ks_ref.shape[0]
    base = ((pl.program_id(0) * NSA_GROUPS + g) * pl.num_programs(2) + i) * ntile

    def note(jt, n):
        visit_ref[n] = jt
        return n + flags_ref[base + jt]

    n_visit = lax.fori_loop(0, i, note, 0)
    q0 = (i * TQ).astype(F32)
    width = NSA_HPG * TQ
    selbias = ((sel_ref[...] - 1.0) * BIG).astype(BF16)
    for h in range(NSA_HPG):
        slope = jnp.where(g == 0, _nsa_slope(0, h), _nsa_slope(1, h))
        lanes = slice(h * TQ, (h + 1) * TQ)
        qa_ref[0:HEAD_DIM, lanes] = q_ref[h]
        qa_ref[HEAD_DIM:2 * HEAD_DIM, lanes] = _alibi_rows(slope, q0, HEAD_DIM).astype(BF16)
        qa_ref[2 * HEAD_DIM:, lanes] = selbias
    qa = qa_ref[...]
    qw = qa_ref[0:2 * HEAD_DIM, :]
    dist = _wide_dist(width)

    causal = dist >= 0
    tiles = [i]
    scores = [jnp.where(causal, _dot(kw_ref[i], qw), NEG)]
    for d in range(1, WINDOW // TK + 1):
        tiles.append(jnp.maximum(i - d, 0))
        limit = jnp.where(i >= d, WINDOW, -(2 ** 30))
        scores.append(jnp.where(dist + d * TK < limit, _dot(kw_ref[tiles[d]], qw), NEG))
    m_win = jnp.max(scores[0], axis=0, keepdims=True)
    for s in scores[1:]:
        m_win = jnp.maximum(m_win, jnp.max(s, axis=0, keepdims=True))
    a_win = jnp.zeros((VAL_ROWS, width), F32)
    for jt, s in zip(tiles, scores):
        a_win = a_win + _dot(vwt_ref[jt], _exp_weights(s, m_win))
    o_win = a_win[:HEAD_DIM] / a_win[HEAD_DIM:HEAD_DIM + 1]

    pipe = _FlashPipe(lambda j: _dot(ks_ref[j], qa_ref[...]), lambda j, p: _dot(vst_ref[j], p),
                      ms_ref, as_ref, s_ref, p_ref, alpha_ref, tile_of=lambda j: visit_ref[j])
    o_sel = pipe.run(jnp.where(causal, _dot(ks_ref[i], qa), NEG), i, n_visit)
    for h in range(NSA_HPG):
        lanes = slice(h * TQ, (h + 1) * TQ)
        row = g * (3 * NSA_HPG) + 3 * h
        g_sel = gate_ref[pl.ds(row + 1, 1), :]
        g_win = gate_ref[pl.ds(row + 2, 1), :]
        o_ref[h] = (accin_ref[h] + g_sel * o_sel[:, lanes] + g_win * o_win[:, lanes]).astype(o_ref.dtype)


def _nsa_selwin(tile_flags, qt, ks_t, vs_t, kw_t, vw_t, sel_t, gates_t, acc_t):
    b, _, _, s = qt.shape
    nt = s // TK
    width = NSA_HPG * TQ
    ksw = ks_t.shape[-1]
    kww = kw_t.shape[-1]
    vspec = pl.BlockSpec((None, None, nt, VAL_ROWS, TK), lambda i, g, j, fl: (i, g, 0, 0, 0))
    qspec = pl.BlockSpec((None, NSA_HPG, HEAD_DIM, TQ), lambda i, g, j, fl: (i, g, 0, j))
    stat = pltpu.VMEM((1, width), F32)
    accs = pltpu.VMEM((VAL_ROWS, width), F32)
    grid_spec = pltpu.PrefetchScalarGridSpec(
        num_scalar_prefetch=1,
        grid=(b, NSA_GROUPS, s // TQ),
        in_specs=[
            qspec,
            pl.BlockSpec((None, None, nt, TK, ksw), lambda i, g, j, fl: (i, g, 0, 0, 0)), vspec,
            pl.BlockSpec((None, None, nt, TK, kww), lambda i, g, j, fl: (i, g, 0, 0, 0)), vspec,
            pl.BlockSpec((None, None, MAX_SEL_BLOCKS, TQ), lambda i, g, j, fl: (i, g, 0, j)),
            pl.BlockSpec((None, LANES, TQ), lambda i, g, j, fl: (i, 0, j)),
            qspec,
        ],
        out_specs=qspec,
        scratch_shapes=[pltpu.VMEM((ksw, width), BF16), stat, accs,
                        pltpu.VMEM((TK, width), F32), pltpu.VMEM((TK, width), BF16), stat,
                        pltpu.SMEM((nt,), I32)],
    )
    return pl.pallas_call(
        _nsa_selwin_kernel,
        grid_spec=grid_spec,
        out_shape=jax.ShapeDtypeStruct((b, NSA_HEADS, HEAD_DIM, s), BF16),
        compiler_params=_params("arbitrary", "arbitrary", "arbitrary"),
        name="nsa_select_window",
    )(tile_flags, qt, ks_t, vs_t, kw_t, vw_t, sel_t, gates_t, acc_t)


def _moba_kernel(q_ref, k_ref, vt_ref, kmean_ref, o_ref, qa_ref, m_ref, acc_ref,
                 s_ref, p_ref, alpha_ref, *, n_top):
    hg = pl.program_id(1)
    i = pl.program_id(2)
    nblk = kmean_ref.shape[1]
    q0 = (i * TQ).astype(F32)
    blk = lax.broadcasted_iota(I32, (nblk, 1), 0)
    for h in range(MOBA_HB):
        head = hg * MOBA_HB + h
        slope = jnp.float32(2.0 ** -MOBA_HEADS)
        for hh in range(MOBA_HEADS - 1):
            slope = jnp.where(head == hh, 2.0 ** -(hh + 1), slope)
        q = q_ref[h]
        gate = jnp.where(blk < i, _dot(kmean_ref[h].astype(BF16), q), NEG)
        bias = jnp.full(gate.shape, -BIG, F32)
        for _ in range(n_top):
            mx = jnp.max(gate, axis=0, keepdims=True)
            first = jnp.min(jnp.where(gate == mx, blk, nblk), axis=0, keepdims=True)
            pick = blk == first
            bias = jnp.where(pick, 0.0, bias)
            gate = jnp.where(pick, -jnp.inf, gate)
        bias = jnp.where(blk < i, bias, 0.0)
        if nblk < MAX_MOBA_BLOCKS:
            bias = jnp.concatenate([bias, jnp.zeros((MAX_MOBA_BLOCKS - nblk, TQ), F32)], axis=0)
        qa_ref[h, 0:HEAD_DIM, :] = q
        qa_ref[h, HEAD_DIM:HEAD_DIM + MAX_MOBA_BLOCKS, :] = _alibi_rows(slope, q0, MAX_MOBA_BLOCKS).astype(BF16)
        qa_ref[h, HEAD_DIM + MAX_MOBA_BLOCKS:, :] = bias.astype(BF16)

    def scores(j):
        return jnp.concatenate([_dot(k_ref[h, j], qa_ref[h]) for h in range(MOBA_HB)], axis=1)

    def values(j, p):
        return jnp.concatenate(
            [_dot(vt_ref[h, j], p[:, h * TQ:(h + 1) * TQ]) for h in range(MOBA_HB)], axis=1)

    pipe = _FlashPipe(scores, values, m_ref, acc_ref, s_ref, p_ref, alpha_ref)
    o = pipe.run(jnp.where(_wide_dist(MOBA_HB * TQ) >= 0, scores(i), NEG), i, i)
    for h in range(MOBA_HB):
        o_ref[h] = o[:, h * TQ:(h + 1) * TQ].astype(o_ref.dtype)


def _moba(qt, k_t, v_t, kmean):
    b, nh, _, s = qt.shape
    nt = s // TK
    n_top = min(MOBA_TOP, nt - 1)
    hb = MOBA_HB
    kw = k_t.shape[-1]
    width = hb * TQ
    return pl.pallas_call(
        functools.partial(_moba_kernel, n_top=n_top),
        grid=(b, nh // hb, s // TQ),
        in_specs=[
            pl.BlockSpec((None, hb, HEAD_DIM, TQ), lambda i, h, j: (i, h, 0, j)),
            pl.BlockSpec((None, hb, nt, TK, kw), lambda i, h, j: (i, h, 0, 0, 0)),
            pl.BlockSpec((None, hb, nt, VAL_ROWS, TK), lambda i, h, j: (i, h, 0, 0, 0)),
            pl.BlockSpec((None, hb, nt, HEAD_DIM), lambda i, h, j: (i, h, 0, 0)),
        ],
        out_specs=pl.BlockSpec((None, hb, HEAD_DIM, TQ), lambda i, h, j: (i, h, 0, j)),
        out_shape=jax.ShapeDtypeStruct((b, nh, HEAD_DIM, s), BF16),
        scratch_shapes=[
            pltpu.VMEM((hb, kw, TQ), BF16), pltpu.VMEM((1, width), F32),
            pltpu.VMEM((VAL_ROWS, width), F32),
            pltpu.VMEM((TK, width), F32), pltpu.VMEM((TK, width), BF16), pltpu.VMEM((1, width), F32),
        ],
        compiler_params=_params("arbitrary", "arbitrary", "arbitrary"),
        name="moba",
    )(qt, k_t, v_t, kmean)


def _outproj_kernel(on_ref, om_ref, w_ref, x_ref, g1_ref, ln_ref, sc_ref, sh_ref,
                    wrh_ref, wrl_ref, rb_ref, x1_ref, h2_ref, lg_ref):
    nh, dh, rows = on_ref.shape
    half = nh * dh

    def rows_major(o_ref):
        return o_ref[...].reshape(half, rows).astype(F32).T.astype(BF16)

    y = _dot(rows_major(on_ref), w_ref[:half, :]) + _dot(rows_major(om_ref), w_ref[half:, :])
    x1 = x_ref[...] + g1_ref[...] * y
    x1_ref[...] = x1
    h2 = _modulated_norm(x1, ln_ref[...], sc_ref[...], sh_ref[...])
    h2_ref[...] = h2
    hh, hl = _hi_lo(h2)
    logits = _dot(hh, wrh_ref[...]) + _dot(hh, wrl_ref[...]) + _dot(hl, wrh_ref[...]) + rb_ref[...]
    lg_ref[...] = logits.T


def _outproj(o_nsa_t, o_moba_t, w_out, x2d, g1, ln_g, sc2p, sh2, wr_hi, wr_lo, rb):
    t, d = x2d.shape
    b, nh, dh, seq = o_nsa_t.shape
    tm = TM_PROJ
    nb = seq // tm
    row = lambda i, j: (i * nb + j, 0)
    bat = lambda i, j: (i, 0, 0)
    const = lambda i, j: (0, 0)
    heads = pl.BlockSpec((None, nh, dh, tm), lambda i, j: (i, 0, 0, j))
    return pl.pallas_call(
        _outproj_kernel,
        grid=(b, nb),
        in_specs=[
            heads, heads,
            pl.BlockSpec(w_out.shape, const),
            pl.BlockSpec((tm, d), row),
            pl.BlockSpec((None, 1, d), bat),
            pl.BlockSpec((1, d), const),
            pl.BlockSpec((None, 1, d), bat), pl.BlockSpec((None, 1, d), bat),
            pl.BlockSpec(wr_hi.shape, const), pl.BlockSpec(wr_lo.shape, const),
            pl.BlockSpec((1, LANES), const),
        ],
        out_specs=[pl.BlockSpec((tm, d), row), pl.BlockSpec((tm, d), row),
                   pl.BlockSpec((LANES, tm), lambda i, j: (0, i * nb + j))],
        out_shape=[jax.ShapeDtypeStruct((t, d), F32), jax.ShapeDtypeStruct((t, d), F32),
                   jax.ShapeDtypeStruct((LANES, t), F32)],
        compiler_params=_params("arbitrary", "arbitrary"),
        name="outproj_router",
    )(o_nsa_t, o_moba_t, w_out, x2d, g1, ln_g, sc2p, sh2, wr_hi, wr_lo, rb)


def _route_kernel(lg_ref, idx_ref, gate_ref, rank_ref, cnt_ref, carry_ref):
    @pl.when(pl.program_id(0) == 0)
    def _():
        carry_ref[...] = jnp.zeros(carry_ref.shape, carry_ref.dtype)

    lg = lg_ref[...]
    ne, tt = lg.shape
    eidx = lax.broadcasted_iota(I32, lg.shape, 0)
    vals, idxs, hots = [], [], []
    for _ in range(TOP_K):
        mx = jnp.max(lg, axis=0, keepdims=True)
        first = jnp.min(jnp.where(lg == mx, eidx, ne), axis=0, keepdims=True)
        hot = eidx == first
        vals.append(mx)
        idxs.append(first)
        hots.append(hot)
        lg = jnp.where(hot, -jnp.inf, lg)
    e = [jnp.exp(v - vals[0]) for v in vals]
    tot = e[0]
    for x in e[1:]:
        tot = tot + x
    cnt = jnp.where(hots[0], 1.0, 0.0)
    for hot in hots[1:]:
        cnt = cnt + jnp.where(hot, 1.0, 0.0)
    cntb = cnt.astype(BF16)
    r = lax.broadcasted_iota(I32, (tt, tt), 0)
    c = lax.broadcasted_iota(I32, (tt, tt), 1)
    before = jnp.where(r < c, 1.0, 0.0).astype(BF16)
    carry = carry_ref[...]
    prefix = _dot(cntb, before) + carry
    ranks = [jnp.sum(jnp.where(hot, prefix, 0.0), axis=0, keepdims=True) for hot in hots]
    carry = carry + _dot(cntb, jnp.ones((tt, tt), BF16))
    carry_ref[...] = carry
    cnt_ref[...] = carry[:, :LANES]
    pad_i = jnp.zeros((SUBLANES - TOP_K, tt), I32)
    pad_f = jnp.zeros((SUBLANES - TOP_K, tt), F32)
    idx_ref[...] = jnp.concatenate(idxs + [pad_i], axis=0)
    gate_ref[...] = jnp.concatenate([x / tot for x in e] + [pad_f], axis=0)
    rank_ref[...] = jnp.concatenate([x.astype(I32) for x in ranks] + [pad_i], axis=0)


def _route(logits_t):
    ne, t = N_EXPERTS, logits_t.shape[1]
    tt = min(TT_ROUTE, t)
    tok = pl.BlockSpec((SUBLANES, tt), lambda i: (0, i))
    return pl.pallas_call(
        _route_kernel,
        grid=(t // tt,),
        in_specs=[pl.BlockSpec((ne, tt), lambda i: (0, i))],
        out_specs=[tok, tok, tok, pl.BlockSpec((ne, LANES), lambda i: (0, 0))],
        out_shape=[jax.ShapeDtypeStruct((SUBLANES, t), I32), jax.ShapeDtypeStruct((SUBLANES, t), F32),
                   jax.ShapeDtypeStruct((SUBLANES, t), I32), jax.ShapeDtypeStruct((ne, LANES), F32)],
        scratch_shapes=[pltpu.VMEM((ne, tt), F32)],
        compiler_params=_params("arbitrary"),
        name="route_topk_rank",
    )(logits_t)


def _dispatch_plan(idx, rank, counts):
    tm = TM_MOE
    t = idx.shape[1]
    nb = (t * TOP_K + N_EXPERTS * tm) // tm
    experts = jnp.arange(N_EXPERTS, dtype=I32)
    counts = counts.astype(I32)
    padded = (counts + tm - 1) // tm * tm
    pend = jnp.sum(jnp.where(experts[None, :] <= experts[:, None], padded[None, :], 0), axis=1)
    pstart = pend - padded
    base = jnp.sum(jnp.where(idx[:, :, None] == experts, pstart, 0), axis=-1)
    dest = (rank + base).T.reshape(t * TOP_K).astype(I32)
    blk_e = jnp.sum(jnp.where(pend[None, :] <= (jnp.arange(nb, dtype=I32) * tm)[:, None], 1, 0), axis=1)
    blk_e = jnp.minimum(blk_e, N_EXPERTS - 1).astype(I32)
    nused = (pend[-1] // tm).astype(I32).reshape(1)
    fill_start = jnp.maximum(pend - tm, 0).astype(I32)
    fill_flag = (padded > counts).astype(I32)
    return dest, blk_e, nused, fill_start, fill_flag


def _dispatch_kernel(fill_start_ref, fill_flag_ref, nused_ref, dest_ref, h_ref, xs_hbm, zbuf, zsem, sem):
    i = pl.program_id(0)
    nrow = dest_ref.shape[1]
    ntok = nrow // TOP_K

    @pl.when(i == 0)
    def _():
        zbuf[...] = jnp.zeros(zbuf.shape, zbuf.dtype)

        def fill(start):
            return pltpu.make_async_copy(zbuf, xs_hbm.at[pl.ds(pl.multiple_of(start, TM_MOE), TM_MOE)], zsem)

        for e in range(N_EXPERTS):
            @pl.when(fill_flag_ref[e] != 0)
            def _():
                fill(fill_start_ref[e]).start()
        for e in range(N_EXPERTS):
            @pl.when(fill_flag_ref[e] != 0)
            def _():
                fill(fill_start_ref[e]).wait()

        nblk = xs_hbm.shape[0] // TM_MOE

        def tail_start(b, c):
            fill(b * TM_MOE).start()
            return c

        def tail_wait(b, c):
            fill(b * TM_MOE).wait()
            return c

        lax.fori_loop(nused_ref[0], nblk, tail_start, 0)
        lax.fori_loop(nused_ref[0], nblk, tail_wait, 0)

    def issue(tl, c):
        src = h_ref.at[pl.ds(tl, 1)]
        for k in range(TOP_K):
            pltpu.make_async_copy(src, xs_hbm.at[pl.ds(dest_ref[0, tl * TOP_K + k], 1)], sem).start(
                priority=k % 2)
        return c

    lax.fori_loop(0, ntok, issue, 0)
    for _ in range(TOP_K):
        pltpu.make_async_copy(h_ref, xs_hbm.at[pl.ds(0, ntok)], sem).wait()


def _dispatch(fill_start, fill_flag, nused, dest, h2):
    t, d = h2.shape
    td = min(TD_DISPATCH, t)
    r = t * TOP_K + N_EXPERTS * TM_MOE
    grid_spec = pltpu.PrefetchScalarGridSpec(
        num_scalar_prefetch=3,
        grid=(t // td,),
        in_specs=[
            pl.BlockSpec((None, 1, td * TOP_K), lambda i, fs, ff, nu: (i, 0, 0), memory_space=pltpu.SMEM),
            pl.BlockSpec((td, d), lambda i, fs, ff, nu: (i, 0)),
        ],
        out_specs=pl.BlockSpec(memory_space=pl.ANY),
        scratch_shapes=[pltpu.VMEM((TM_MOE, d), F32), pltpu.SemaphoreType.DMA(()), pltpu.SemaphoreType.DMA(())],
    )
    return pl.pallas_call(
        _dispatch_kernel,
        grid_spec=grid_spec,
        out_shape=jax.ShapeDtypeStruct((r, d), F32),
        compiler_params=_params("arbitrary"),
        name="moe_dispatch",
    )(fill_start, fill_flag, nused, dest.reshape(t // td, 1, td * TOP_K), h2)


def _moe_ffn_kernel(blk_e_ref, nused_ref, x_ref, w1_ref, b1_ref, w2_ref, b2_ref, y_ref, w1b, w2b):
    i = pl.program_id(0)
    dff = w2_ref.shape[0]

    @pl.when(i < nused_ref[0])
    def _():
        prev = blk_e_ref[jnp.maximum(i - 1, 0)]

        @pl.when((i == 0) | (blk_e_ref[i] != prev))
        def _():
            w1b[...] = w1_ref[...].astype(BF16)
            w2b[...] = w2_ref[...].astype(BF16)

        u = _dot(x_ref[...].astype(BF16), w1b[...]) + b1_ref[...]
        gl = jnp.minimum(u[:, :dff], SWIGLU_LIMIT)
        lin = jnp.clip(u[:, dff:], -SWIGLU_LIMIT, SWIGLU_LIMIT)
        act = (lin + 1.0) * gl * _sigmoid(SWIGLU_ALPHA * gl)
        y_ref[...] = _dot(act.astype(BF16), w2b[...]) + b2_ref[...]

    @pl.when(i >= nused_ref[0])
    def _():
        y_ref[...] = jnp.zeros(y_ref.shape, y_ref.dtype)


def _moe_ffn(blk_e, nused, xs, w1, b1, w2, b2, layer):
    r, d = xs.shape
    depth, ne, _, f2 = w1.shape
    dff = f2 // 2
    tm = TM_MOE
    rows = lambda i, be, nu: (jnp.minimum(i, nu[0] - 1), 0)
    expert = lambda i, be, nu: (layer, be[i], 0, 0)
    grid_spec = pltpu.PrefetchScalarGridSpec(
        num_scalar_prefetch=2,
        grid=(r // tm,),
        in_specs=[
            pl.BlockSpec((tm, d), rows),
            pl.BlockSpec((None, None, d, f2), expert),
            pl.BlockSpec((None, None, 1, f2), expert),
            pl.BlockSpec((None, None, dff, d), expert),
            pl.BlockSpec((None, None, 1, d), expert),
        ],
        out_specs=pl.BlockSpec((tm, d), lambda i, be, nu: (i, 0)),
        scratch_shapes=[pltpu.VMEM((d, f2), BF16), pltpu.VMEM((dff, d), BF16)],
    )
    return pl.pallas_call(
        _moe_ffn_kernel,
        grid_spec=grid_spec,
        out_shape=jax.ShapeDtypeStruct((r, d), F32),
        compiler_params=_params("arbitrary"),
        name="moe_ffn",
    )(blk_e, nused, xs, w1, b1.reshape(depth, ne, 1, f2), w2, b2.reshape(depth, ne, 1, d))


def _moe_combine_kernel(pos_ref, nxt_ref, y_hbm, gate_ref, x1_ref, g2_ref, o_ref, buf, sems):
    i = pl.program_id(0)
    n = pl.num_programs(0)
    tc = x1_ref.shape[0]
    slot = i % 2

    def fetch(p_ref, s):
        def issue(tl, c):
            for k in range(TOP_K):
                pltpu.make_async_copy(y_hbm.at[pl.ds(p_ref[0, tl * TOP_K + k], 1)],
                                      buf.at[s, k, pl.ds(tl, 1)], sems.at[s]).start(priority=k % 2)
            return c
        lax.fori_loop(0, tc, issue, 0, unroll=2)

    @pl.when(i == 0)
    def _():
        fetch(pos_ref, 0)

    for s in range(2):
        @pl.when((i + 1 < n) & (slot != s))
        def _():
            fetch(nxt_ref, s)

    for s in range(2):
        @pl.when(slot == s)
        def _():
            for k in range(TOP_K):
                pltpu.make_async_copy(y_hbm.at[pl.ds(0, tc)], buf.at[s, k], sems.at[s]).wait()
    gate = gate_ref[...]
    moe = gate[:, 0:1] * buf[slot, 0]
    for k in range(1, TOP_K):
        moe = moe + gate[:, k:k + 1] * buf[slot, k]
    o_ref[...] = x1_ref[...] + g2_ref[...] * moe


def _moe_combine(pos, y, gate, x1, g2, seq):
    t, d = x1.shape
    tc = TC_COMB
    nb = seq // tc
    n = t // tc
    pos3 = pos.reshape(n, 1, tc * TOP_K)
    return pl.pallas_call(
        _moe_combine_kernel,
        grid=(n,),
        in_specs=[
            pl.BlockSpec((None, 1, tc * TOP_K), lambda i: (i, 0, 0), memory_space=pltpu.SMEM),
            pl.BlockSpec((None, 1, tc * TOP_K), lambda i: (jnp.minimum(i + 1, n - 1), 0, 0),
                         memory_space=pltpu.SMEM),
            pl.BlockSpec(memory_space=pl.ANY),
            pl.BlockSpec((tc, TOP_K), lambda i: (i, 0)),
            pl.BlockSpec((tc, d), lambda i: (i, 0)),
            pl.BlockSpec((None, 1, d), lambda i: (i // nb, 0, 0)),
        ],
        out_specs=pl.BlockSpec((tc, d), lambda i: (i, 0)),
        out_shape=jax.ShapeDtypeStruct((t, d), F32),
        scratch_shapes=[pltpu.VMEM((2, TOP_K, tc, d), F32), pltpu.SemaphoreType.DMA((2,))],
        compiler_params=_params("arbitrary"),
        name="moe_combine",
    )(pos3, pos3, y, gate, x1, g2)


def _tile_gain(gain, width, scale=1.0):
    return (jnp.tile(gain, width // HEAD_DIM) * scale).reshape(1, width).astype(F32)


def _layer(x2d, mod, b, s, ln1_g, ln2_g, w_in, nsa_q_gain, nsa_k_gain, cmp_pos, cmp_w1, cmp_w2,
           moba_q_gain, moba_k_gain, w_out, router_w, router_b, exp_w1, exp_b1, exp_w2, exp_b2, layer):
    t, d = x2d.shape
    sh1, sc1, g1, sh2, sc2, g2 = [m.reshape(b, 1, d) for m in jnp.split(mod, 6, axis=-1)]
    scale = HEAD_DIM ** -0.5

    (qt, kcv, ks_t, vs_t, kw_t, vw_t, gates_t, mq_t, mk_t, mv_t, kmean) = _inproj(
        x2d, ln1_g.reshape(1, d), 1.0 + sc1, sh1, _pack_w_in(w_in),
        _tile_gain(nsa_q_gain, 512, scale), _tile_gain(nsa_k_gain, 128),
        _tile_gain(moba_q_gain, 512, scale), _tile_gain(moba_k_gain, 512), b, s)

    g = NSA_GROUPS
    xc = kcv.reshape(b, s, 2, g, HEAD_DIM).transpose(0, 2, 3, 1, 4).reshape(
        b, 2, g, s // CMP_STRIDE, CMP_STRIDE * HEAD_DIM)
    cmp = _compress(xc, cmp_pos.reshape(2, 1, CMP_LEN * HEAD_DIM), cmp_w1.astype(BF16),
                    cmp_w2.astype(BF16), nsa_k_gain.reshape(1, HEAD_DIM))
    kc = cmp[:, 0].astype(BF16)
    vct = cmp[:, 1].transpose(0, 1, 3, 2).astype(BF16)
    acc_t, sel_t, tile_hits = _nsa_cmp(qt, kc, vct, gates_t)
    nsel = sel_t.shape[2]
    if nsel < MAX_SEL_BLOCKS:
        sel_t = jnp.concatenate([sel_t, jnp.ones((b, g, MAX_SEL_BLOCKS - nsel, s), F32)], axis=2)
    tile_flags = (tile_hits[..., 0] > 0).astype(I32).reshape(-1)
    o_nsa_t = _nsa_selwin(tile_flags, qt, ks_t, vs_t, kw_t, vw_t, sel_t, gates_t, acc_t)

    km = kmean.reshape(b, s // MOBA_BLOCK, MOBA_HEADS, HEAD_DIM).transpose(0, 2, 1, 3)
    o_moba_t = _moba(mq_t, mk_t, mv_t, km)

    wr = jnp.zeros((d, LANES), F32).at[:, :N_EXPERTS].set(router_w)
    wr_hi = wr.astype(BF16)
    wr_lo = (wr - wr_hi.astype(F32)).astype(BF16)
    rb = jnp.zeros((1, LANES), F32).at[0, :N_EXPERTS].set(router_b)
    x1, h2, logits_t = _outproj(o_nsa_t, o_moba_t, w_out.astype(BF16), x2d, g1, ln2_g.reshape(1, d),
                                1.0 + sc2, sh2, wr_hi, wr_lo, rb)

    idx8, gate8, rank8, counts = _route(logits_t)
    dest, blk_e, nused, fill_start, fill_flag = _dispatch_plan(idx8[:TOP_K], rank8[:TOP_K], counts[:, 0])
    xs = _dispatch(fill_start, fill_flag, nused, dest, h2)
    y = _moe_ffn(blk_e, nused, xs, exp_w1, exp_b1, exp_w2, exp_b2, layer)
    return _moe_combine(dest, y, gate8[:TOP_K].T, x1, g2, s)


def kernel(x, c, ada_w, ada_b, ln1_g, ln2_g, w_in, nsa_q_gain, nsa_k_gain, nsa_cmp_pos, nsa_cmp_w1,
           nsa_cmp_w2, moba_q_gain, moba_k_gain, w_out, router_w, router_b, exp_w1, exp_b1, exp_w2, exp_b2):
    b, s, d = x.shape
    depth = ada_w.shape[0]
    mods = _adaln(c, ada_w, ada_b)
    x2d = x.reshape(b * s, d)
    for l in range(depth):
        x2d = _layer(x2d, mods[l], b, s, ln1_g[l], ln2_g[l], w_in[l], nsa_q_gain[l], nsa_k_gain[l],
                     nsa_cmp_pos[l], nsa_cmp_w1[l], nsa_cmp_w2[l], moba_q_gain[l], moba_k_gain[l],
                     w_out[l], router_w[l], router_b[l], exp_w1, exp_b1, exp_w2, exp_b2, l)
    return x2d.reshape(b, s, d)
```

```python
import functools

import numpy as np
import jax
import jax.numpy as jnp
from jax import lax
from jax.experimental import pallas as pl
from jax.experimental.pallas import tpu as pltpu

F32 = jnp.float32
BF16 = jnp.bfloat16
I32 = jnp.int32

HEAD_DIM = 64
NSA_HEADS = 8
NSA_GROUPS = 2
NSA_HPG = NSA_HEADS // NSA_GROUPS
CMP_LEN = 32
CMP_STRIDE = 16
CMP_HID = 128
SEL_BLOCK = 64
SEL_TOP = 16
WINDOW = 512
MOBA_HEADS = 8
MOBA_BLOCK = 256
MOBA_TOP = 3
N_EXPERTS = 32
TOP_K = 4
SWIGLU_LIMIT = 7.0
SWIGLU_ALPHA = 1.702
EPS = 1e-6
NEG = -1e30
FORCE_SCORE = 1e4

LANES = 128
SUBLANES = 8
TQ = 256
TK = 256
POS_SPLIT = 256
BIG = 1e30
MAX_SEL_BLOCKS = 128
MAX_MOBA_BLOCKS = 32
MOBA_HB = 8
VAL_ROWS = HEAD_DIM + 16
RANK_CHUNK = 32
TM_PROJ = 512
TM_MOE = 512
TC_COMB = 128
TT_ROUTE = 512
TD_DISPATCH = 512
VMEM_LIMIT = 56 * 1024 * 1024


def _dot(a, b):
    return jnp.dot(a, b, preferred_element_type=F32)


def _hi_lo(x):
    hi = x.astype(BF16)
    lo = (x - hi.astype(F32)).astype(BF16)
    return hi, lo


def _sigmoid(x):
    return 1.0 / (1.0 + jnp.exp(-x))


def _params(*sem):
    return pltpu.CompilerParams(dimension_semantics=sem, vmem_limit_bytes=VMEM_LIMIT)


def _adaln_kernel(c_ref, w_ref, b_ref, o_ref):
    c = c_ref[...]
    cond = c * _sigmoid(c)
    ch, cl = _hi_lo(cond)
    wh, wl = _hi_lo(w_ref[...])
    o_ref[...] = _dot(ch, wh) + _dot(ch, wl) + _dot(cl, wh) + b_ref[...]


def _adaln(c, ada_w, ada_b):
    depth, d, n6 = ada_w.shape
    b = c.shape[0]
    tn = 1024
    c_pad = jnp.zeros((SUBLANES, d), F32).at[:b].set(c)
    out = pl.pallas_call(
        _adaln_kernel,
        grid=(depth, n6 // tn),
        in_specs=[
            pl.BlockSpec((SUBLANES, d), lambda l, j: (0, 0)),
            pl.BlockSpec((None, d, tn), lambda l, j: (l, 0, j)),
            pl.BlockSpec((None, 1, tn), lambda l, j: (l, 0, j)),
        ],
        out_specs=pl.BlockSpec((None, SUBLANES, tn), lambda l, j: (l, 0, j)),
        out_shape=jax.ShapeDtypeStruct((depth, SUBLANES, n6), F32),
        compiler_params=_params("arbitrary", "arbitrary"),
        name="adaln",
    )(c_pad, ada_w, ada_b.reshape(depth, 1, n6))
    return out[:, :b]


def _block_diag_ones():
    r = lax.broadcasted_iota(I32, (LANES, LANES), 0) // HEAD_DIM
    c = lax.broadcasted_iota(I32, (LANES, LANES), 1) // HEAD_DIM
    return jnp.where(r == c, 1.0, 0.0).astype(BF16)


def _head_norm(z, bd, gain):
    cols = []
    for c0 in range(0, z.shape[1], LANES):
        zc = z[:, c0:c0 + LANES]
        hi, lo = _hi_lo(zc * zc)
        ss = _dot(hi, bd) + _dot(lo, bd)
        cols.append(zc * lax.rsqrt(ss * (1.0 / HEAD_DIM) + EPS))
    zn = cols[0] if len(cols) == 1 else jnp.concatenate(cols, axis=1)
    return zn * gain


def _modulated_norm(x, g, sc1p, sh):
    ms = jnp.mean(x * x, axis=-1, keepdims=True)
    y = x * lax.rsqrt(ms + EPS)
    return (y * g) * sc1p + sh


def _heads_transposed(z, nh):
    return z.T.astype(BF16).reshape(nh, HEAD_DIM, z.shape[0])


def _store_value_tiles(o_ref, z, nh):
    zt = z.T.astype(BF16)
    extra = VAL_ROWS - HEAD_DIM
    ones_row = jnp.where(lax.broadcasted_iota(I32, (extra, TK), 0) == 0, 1.0, 0.0).astype(BF16)
    for h in range(nh):
        for t in range(z.shape[0] // TK):
            vt = zt[h * HEAD_DIM:(h + 1) * HEAD_DIM, t * TK:(t + 1) * TK]
            o_ref[h, t] = jnp.concatenate([vt, ones_row], axis=0)


def _store_key_tiles(o_ref, z, extra, nh):
    rows = z.shape[0]
    for h in range(nh):
        ka = jnp.concatenate([z[:, h * HEAD_DIM:(h + 1) * HEAD_DIM], extra], axis=1).astype(BF16)
        o_ref[h] = ka.reshape(rows // TK, TK, ka.shape[1])


def _pos_cols(pos, width):
    col = lax.broadcasted_iota(I32, (pos.shape[0], width), 1)
    hi = lax.shift_right_logical(pos, POS_SPLIT.bit_length() - 1).astype(F32)
    lo = (pos & (POS_SPLIT - 1)).astype(F32)
    return jnp.where(col == 0, hi, jnp.where(col == 1, lo, jnp.where(col == 2, 1.0, 0.0)))


def _onehot_cols(pos, block, width):
    col = lax.broadcasted_iota(I32, (pos.shape[0], width), 1)
    return jnp.where(lax.shift_right_logical(pos, block.bit_length() - 1) == col, 1.0, 0.0)


def _inproj_kernel(x_ref, g_ref, sc_ref, sh_ref, w_ref, gq_ref, gk_ref, gmq_ref, gmk_ref,
                   q_ref, kcv_ref, ks_ref, vs_ref, kw_ref, vw_ref, gate_ref,
                   mq_ref, mk_ref, mv_ref, kmean_ref):
    h = _modulated_norm(x_ref[...], g_ref[...], sc_ref[...], sh_ref[...])
    hb = h.astype(BF16)
    bd = _block_diag_ones()
    rows = hb.shape[0]
    pos = pl.program_id(1) * rows + lax.broadcasted_iota(I32, (rows, 1), 0)

    def proj(a, b):
        return _dot(hb, w_ref[:, a:b])

    q_ref[...] = _heads_transposed(_head_norm(proj(0, 512), bd, gq_ref[...]), NSA_HEADS)
    kv = proj(512, 1280)
    kcv_ref[...] = kv[:, 0:256]
    pos_cols = _pos_cols(pos, HEAD_DIM)
    sel_cols = jnp.concatenate([pos_cols, _onehot_cols(pos, SEL_BLOCK, MAX_SEL_BLOCKS)], axis=1)
    _store_key_tiles(ks_ref, _head_norm(kv[:, 256:384], bd, gk_ref[...]), sel_cols, NSA_GROUPS)
    _store_value_tiles(vs_ref, kv[:, 384:512], NSA_GROUPS)
    _store_key_tiles(kw_ref, _head_norm(kv[:, 512:640], bd, gk_ref[...]), pos_cols, NSA_GROUPS)
    _store_value_tiles(vw_ref, kv[:, 640:768], NSA_GROUPS)
    gate_ref[...] = _sigmoid(proj(1280, 1408)).T
    mq_ref[...] = _heads_transposed(_head_norm(proj(1408, 1920), bd, gmq_ref[...]), MOBA_HEADS)
    mkn = _head_norm(proj(1920, 2432), bd, gmk_ref[...])
    moba_cols = jnp.concatenate([_pos_cols(pos, MAX_MOBA_BLOCKS),
                                 _onehot_cols(pos, MOBA_BLOCK, MAX_MOBA_BLOCKS)], axis=1)
    _store_key_tiles(mk_ref, mkn, moba_cols, MOBA_HEADS)
    _store_value_tiles(mv_ref, proj(2432, 2944), MOBA_HEADS)
    means = [jnp.sum(mkn[r:r + MOBA_BLOCK], axis=0, keepdims=True) * (1.0 / MOBA_BLOCK)
             for r in range(0, rows, MOBA_BLOCK)]
    kmean_ref[...] = jnp.concatenate(means, axis=0)


def _pack_w_in(w_in):
    d = w_in.shape[0]
    wg = jnp.zeros((d, LANES), w_in.dtype).at[:, :3 * NSA_HEADS].set(w_in[:, 1280:1304])
    return jnp.concatenate([w_in[:, :1280], wg, w_in[:, 1304:]], axis=1).astype(BF16)


def _inproj(x2d, ln_g, sc1p, sh, w_pack, gq, gk, gmq, gmk, b, seq):
    t, d = x2d.shape
    tm = TM_PROJ
    nb = seq // tm
    nt = seq // TK
    tpt = tm // TK
    g = NSA_GROUPS
    row = lambda i, j: (i * nb + j, 0)
    bat = lambda i, j: (i, 0, 0)
    const = lambda i, j: (0, 0)
    heads_t = lambda nh: pl.BlockSpec((None, nh, HEAD_DIM, tm), lambda i, j: (i, 0, 0, j))
    key_tiles = lambda nh, w: pl.BlockSpec((None, nh, tpt, TK, w), lambda i, j: (i, 0, j, 0, 0))
    val_tiles = lambda nh: pl.BlockSpec((None, nh, tpt, VAL_ROWS, TK), lambda i, j: (i, 0, j, 0, 0))
    sel_w = 2 * HEAD_DIM + MAX_SEL_BLOCKS
    win_w = 2 * HEAD_DIM
    moba_w = HEAD_DIM + 2 * MAX_MOBA_BLOCKS
    nmean = tm // MOBA_BLOCK
    sds = jax.ShapeDtypeStruct
    out_shape = [
        sds((b, NSA_HEADS, HEAD_DIM, seq), BF16), sds((t, 256), F32),
        sds((b, g, nt, TK, sel_w), BF16), sds((b, g, nt, VAL_ROWS, TK), BF16),
        sds((b, g, nt, TK, win_w), BF16), sds((b, g, nt, VAL_ROWS, TK), BF16),
        sds((b, LANES, seq), F32),
        sds((b, MOBA_HEADS, HEAD_DIM, seq), BF16),
        sds((b, MOBA_HEADS, nt, TK, moba_w), BF16), sds((b, MOBA_HEADS, nt, VAL_ROWS, TK), BF16),
        sds((t // tm, nmean, 512), F32),
    ]
    out_specs = [
        heads_t(NSA_HEADS), pl.BlockSpec((tm, 256), row),
        key_tiles(g, sel_w), val_tiles(g), key_tiles(g, win_w), val_tiles(g),
        pl.BlockSpec((None, LANES, tm), lambda i, j: (i, 0, j)),
        heads_t(MOBA_HEADS), key_tiles(MOBA_HEADS, moba_w), val_tiles(MOBA_HEADS),
        pl.BlockSpec((None, nmean, 512), lambda i, j: (i * nb + j, 0, 0)),
    ]
    return pl.pallas_call(
        _inproj_kernel,
        grid=(b, nb),
        in_specs=[
            pl.BlockSpec((tm, d), row),
            pl.BlockSpec((1, d), const),
            pl.BlockSpec((None, 1, d), bat),
            pl.BlockSpec((None, 1, d), bat),
            pl.BlockSpec(w_pack.shape, const),
            pl.BlockSpec((1, 512), const),
            pl.BlockSpec((1, 128), const),
            pl.BlockSpec((1, 512), const),
            pl.BlockSpec((1, 512), const),
        ],
        out_specs=out_specs,
        out_shape=out_shape,
        compiler_params=_params("arbitrary", "arbitrary"),
        name="inproj",
    )(x2d, ln_g, sc1p, sh, w_pack, gq, gk, gmq, gmk)


def _compress_kernel(x_ref, pos_ref, w1_ref, w2_ref, gk_ref, o_ref):
    kv = pl.program_id(1)
    half = CMP_STRIDE * HEAD_DIM
    x = x_ref[...]
    xl = (x + pos_ref[:, :half]).astype(BF16)
    xh = (x + pos_ref[:, half:]).astype(BF16)
    a = _dot(xl, w1_ref[:half, :])
    b = _dot(xh, w1_ref[half:, :])
    hid = a + pltpu.roll(b, b.shape[0] - 1, 0)
    out = _dot(jax.nn.gelu(hid, approximate=True).astype(BF16), w2_ref[...])
    ms = jnp.mean(out * out, axis=-1, keepdims=True)
    normed = (out * lax.rsqrt(ms + EPS)) * gk_ref[...]
    o_ref[...] = jnp.where(kv == 0, normed, out)


def _compress(xc, pos, w1, w2, gk):
    b, _, g, nr, w = xc.shape
    return pl.pallas_call(
        _compress_kernel,
        grid=(b, 2, g),
        in_specs=[
            pl.BlockSpec((None, None, None, nr, w), lambda i, k, j: (i, k, j, 0, 0)),
            pl.BlockSpec((None, 1, w * 2), lambda i, k, j: (k, 0, 0)),
            pl.BlockSpec((None, w * 2, CMP_HID), lambda i, k, j: (k, 0, 0)),
            pl.BlockSpec((None, CMP_HID, HEAD_DIM), lambda i, k, j: (k, 0, 0)),
            pl.BlockSpec((1, HEAD_DIM), lambda i, k, j: (0, 0)),
        ],
        out_specs=pl.BlockSpec((None, None, None, nr, HEAD_DIM), lambda i, k, j: (i, k, j, 0, 0)),
        out_shape=jax.ShapeDtypeStruct((b, 2, g, nr, HEAD_DIM), F32),
        compiler_params=_params("arbitrary", "arbitrary", "arbitrary"),
        name="nsa_compress",
    )(xc, pos, w1, w2, gk)


def _nsa_slope(g, h):
    return 2.0 ** (-(g * NSA_HPG + h + 1))


def _rank_counts(imp_ref, cnt_ref, last_blk):
    nb = imp_ref.shape[0]
    chunk = min(RANK_CHUNK, nb)
    cnt_ref[...] = jnp.zeros(cnt_ref.shape, cnt_ref.dtype)
    sub = lax.broadcasted_iota(I32, (SUBLANES, imp_ref.shape[1]), 0)
    for c in range(nb // chunk):
        for d in range(nb // chunk):
            @pl.when(max(c, d) * chunk <= last_blk)
            def _():
                rows0 = range(d * chunk, (d + 1) * chunk, SUBLANES)
                pieces = [imp_ref[r:r + SUBLANES] for r in rows0]
                counts = [jnp.zeros(p.shape, I32) for p in pieces]
                for jp in range(c * chunk, (c + 1) * chunk):
                    row = imp_ref[jp:jp + 1]
                    for k, r0 in enumerate(rows0):
                        p = pieces[k]
                        if r0 + SUBLANES - 1 < jp:
                            inc = jnp.where(row > p, 1, 0)
                        elif r0 > jp:
                            inc = jnp.where(row >= p, 1, 0)
                        else:
                            inc = jnp.where(sub > jp - r0, jnp.where(row >= p, 1, 0), jnp.where(row > p, 1, 0))
                        counts[k] = counts[k] + inc
                rows = slice(d * chunk, (d + 1) * chunk)
                cnt_ref[rows] = cnt_ref[rows] + jnp.concatenate(counts, axis=0)


def _nsa_cmp_kernel(q_ref, kc_ref, vct_ref, inct_ref, gate_ref, acc_ref, sel_ref, flag_ref,
                    imp_ref, cnt_ref, *, n_top):
    g = pl.program_id(1)
    q0 = pl.program_id(2) * TQ
    nc = kc_ref.shape[0]
    nsel = inct_ref.shape[0]
    t = q0 + lax.broadcasted_iota(I32, (1, TQ), 1)
    width = NSA_HPG * TQ
    tw = q0 + (lax.broadcasted_iota(I32, (1, width), 1) & (TQ - 1))
    n = lax.broadcasted_iota(I32, (nc, 1), 0)
    center = (CMP_STRIDE * n).astype(F32) + 0.5 * (CMP_LEN - 1)
    dist = tw.astype(F32) - center
    mask = (CMP_STRIDE * n + (CMP_LEN - 1)) <= tw
    slopes = jnp.concatenate(
        [jnp.full((1, TQ), 1.0, F32) * jnp.where(g == 0, _nsa_slope(0, h), _nsa_slope(1, h))
         for h in range(NSA_HPG)], axis=1)
    qw = jnp.concatenate([q_ref[h] for h in range(NSA_HPG)], axis=1)
    s = _dot(kc_ref[...], qw) - slopes * dist
    s = jnp.where(mask, s, NEG)
    m = jnp.max(s, axis=0, keepdims=True)
    e = jnp.exp(s - m)
    l = jnp.sum(e, axis=0, keepdims=True)
    p = jnp.where(mask, e, 0.0) * (1.0 / l)
    o = _dot(vct_ref[...], p.astype(BF16))
    psum = jnp.zeros((nc, TQ), F32)
    for h in range(NSA_HPG):
        lanes = slice(h * TQ, (h + 1) * TQ)
        gate = gate_ref[pl.ds(g * (3 * NSA_HPG) + 3 * h, 1), :]
        acc_ref[h] = gate * o[:, lanes]
        psum = psum + p[:, lanes]
    hi, lo = _hi_lo(psum)
    inct = inct_ref[...]
    imp = _dot(inct, hi) + _dot(inct, lo)
    blk = lax.broadcasted_iota(I32, (nsel, 1), 0)
    cur = t // SEL_BLOCK
    forced = (blk == 0) | (blk == cur) | (blk == cur - 1)
    imp = jnp.where(forced, FORCE_SCORE, imp)
    imp_ref[...] = jnp.where(blk <= cur, imp, NEG)
    _rank_counts(imp_ref, cnt_ref, (q0 + TQ - 1) // SEL_BLOCK)
    chosen = (cnt_ref[...] < n_top) & (blk <= cur)
    sel = jnp.where(chosen, 1.0, 0.0)
    sel_ref[...] = sel
    bpt = TK // SEL_BLOCK
    ntile = nsel // bpt
    trow = lax.broadcasted_iota(I32, (ntile, nsel), 0)
    tcol = lax.broadcasted_iota(I32, (ntile, nsel), 1)
    member = jnp.where(lax.shift_right_logical(tcol, bpt.bit_length() - 1) == trow, 1.0, 0.0).astype(BF16)
    hits = jnp.sum(_dot(member, sel.astype(BF16)), axis=1, keepdims=True)
    flag_ref[...] = jnp.broadcast_to(hits, (ntile, LANES))


def _incidence_t(nc, nsel):
    cmp_start = CMP_STRIDE * np.arange(nc)
    sel_start = SEL_BLOCK * np.arange(nsel)
    inc = ((cmp_start[:, None] <= sel_start[None, :] + SEL_BLOCK - 1)
           & (cmp_start[:, None] + CMP_LEN - 1 >= sel_start[None, :]))
    inc[nc - 1] = False
    return jnp.asarray(inc.T, dtype=BF16)


def _nsa_cmp(qt, kc, vct, gates_t):
    b, _, _, s = qt.shape
    nc = kc.shape[2]
    nsel = s // SEL_BLOCK
    n_top = min(SEL_TOP, nsel)
    ntile = s // TK
    inct = _incidence_t(nc, nsel)
    return pl.pallas_call(
        functools.partial(_nsa_cmp_kernel, n_top=n_top),
        grid=(b, NSA_GROUPS, s // TQ),
        in_specs=[
            pl.BlockSpec((None, NSA_HPG, HEAD_DIM, TQ), lambda i, g, j: (i, g, 0, j)),
            pl.BlockSpec((None, None, nc, HEAD_DIM), lambda i, g, j: (i, g, 0, 0)),
            pl.BlockSpec((None, None, HEAD_DIM, nc), lambda i, g, j: (i, g, 0, 0)),
            pl.BlockSpec((nsel, nc), lambda i, g, j: (0, 0)),
            pl.BlockSpec((None, LANES, TQ), lambda i, g, j: (i, 0, j)),
        ],
        out_specs=[
            pl.BlockSpec((None, NSA_HPG, HEAD_DIM, TQ), lambda i, g, j: (i, g, 0, j)),
            pl.BlockSpec((None, None, nsel, TQ), lambda i, g, j: (i, g, 0, j)),
            pl.BlockSpec((None, None, None, ntile, LANES), lambda i, g, j: (i, g, j, 0, 0)),
        ],
        out_shape=[
            jax.ShapeDtypeStruct((b, NSA_HEADS, HEAD_DIM, s), F32),
            jax.ShapeDtypeStruct((b, NSA_GROUPS, nsel, s), F32),
            jax.ShapeDtypeStruct((b, NSA_GROUPS, s // TQ, ntile, LANES), F32),
        ],
        scratch_shapes=[pltpu.VMEM((nsel, TQ), F32), pltpu.VMEM((nsel, TQ), I32)],
        compiler_params=_params("arbitrary", "arbitrary", "arbitrary"),
        name="nsa_cmp_select",
    )(qt, kc, vct, inct, gates_t)


def _alibi_rows(slope, q0, rows):
    r = lax.broadcasted_iota(I32, (rows, TQ), 0)
    return jnp.where(r == 0, slope * POS_SPLIT, jnp.where(r == 1, slope, jnp.where(r == 2, -slope * q0, 0.0)))


def _wide_dist(width):
    lane = lax.broadcasted_iota(I32, (TK, width), 1) & (TQ - 1)
    return lane - lax.broadcasted_iota(I32, (TK, width), 0)


def _exp_weights(s, m):
    return jnp.exp((s - m).astype(BF16))


class _FlashPipe:
    def __init__(self, scores, values, m_ref, acc_ref, s_ref, p_ref, alpha_ref, tile_of=lambda j: j):
        self.scores, self.values, self.tile_of = scores, values, tile_of
        self.m_ref, self.acc_ref = m_ref, acc_ref
        self.s_ref, self.p_ref, self.alpha_ref = s_ref, p_ref, alpha_ref

    def run(self, first_scores, first_tile, n_past):
        m = jnp.max(first_scores, axis=0, keepdims=True)
        self.m_ref[...] = m
        self.p_ref[...] = _exp_weights(first_scores, m)
        self.alpha_ref[...] = jnp.ones(self.alpha_ref.shape, F32)
        self.acc_ref[...] = jnp.zeros(self.acc_ref.shape, F32)

        @pl.when(n_past > 0)
        def _():
            self.s_ref[...] = self.scores(self.tile_of(0))

        def step(j, carry):
            pending = jnp.where(j == 0, first_tile, self.tile_of(jnp.maximum(j - 1, 0)))
            s = self.s_ref[...]
            self.s_ref[...] = self.scores(self.tile_of(jnp.minimum(j + 1, n_past - 1)))
            pv = self.values(pending, self.p_ref[...])
            self.acc_ref[...] = self.alpha_ref[...] * self.acc_ref[...] + pv
            m_old = self.m_ref[...]
            m_new = jnp.maximum(m_old, jnp.max(s, axis=0, keepdims=True))
            self.m_ref[...] = m_new
            self.alpha_ref[...] = jnp.exp(m_old - m_new)
            self.p_ref[...] = _exp_weights(s, m_new)
            return carry

        lax.fori_loop(0, n_past, step, 0)
        pending = jnp.where(n_past == 0, first_tile, self.tile_of(jnp.maximum(n_past - 1, 0)))
        acc = self.alpha_ref[...] * self.acc_ref[...] + self.values(pending, self.p_ref[...])
        return acc[:HEAD_DIM] / acc[HEAD_DIM:HEAD_DIM + 1]


def _nsa_selwin_kernel(flags_ref, q_ref, ks_ref, vst_ref, kw_ref, vwt_ref, sel_ref, gate_ref, accin_ref,
                       o_ref, qa_ref, ms_ref, as_ref, s_ref, p_ref, alpha_ref, visit_ref):
    g = pl.program_id(1)
    i = pl.program_id(2)
    ntile = ks_ref.shape[0]
    base = ((pl.program_id(0) * NSA_GROUPS + g) * pl.num_programs(2) + i) * ntile

    def note(jt, n):
        visit_ref[n] = jt
        return n + flags_ref[base + jt]

    n_visit = lax.fori_loop(0, i, note, 0)
    q0 = (i * TQ).astype(F32)
    width = NSA_HPG * TQ
    selbias = ((sel_ref[...] - 1.0) * BIG).astype(BF16)
    for h in range(NSA_HPG):
        slope = jnp.where(g == 0, _nsa_slope(0, h), _nsa_slope(1, h))
        lanes = slice(h * TQ, (h + 1) * TQ)
        qa_ref[0:HEAD_DIM, lanes] = q_ref[h]
        qa_ref[HEAD_DIM:2 * HEAD_DIM, lanes] = _alibi_rows(slope, q0, HEAD_DIM).astype(BF16)
        qa_ref[2 * HEAD_DIM:, lanes] = selbias
    qa = qa_ref[...]
    qw = qa_ref[0:2 * HEAD_DIM, :]
    dist = _wide_dist(width)

    causal = dist >= 0
    tiles = [i]
    scores = [jnp.where(causal, _dot(kw_ref[i], qw), NEG)]
    for d in range(1, WINDOW // TK + 1):
        tiles.append(jnp.maximum(i - d, 0))
        limit = jnp.where(i >= d, WINDOW, -(2 ** 30))
        scores.append(jnp.where(dist + d * TK < limit, _dot(kw_ref[tiles[d]], qw), NEG))
    m_win = jnp.max(scores[0], axis=0, keepdims=True)
    for s in scores[1:]:
        m_win = jnp.maximum(m_win, jnp.max(s, axis=0, keepdims=True))
    a_win = jnp.zeros((VAL_ROWS, width), F32)
    for jt, s in zip(tiles, scores):
        a_win = a_win + _dot(vwt_ref[jt], _exp_weights(s, m_win))
    o_win = a_win[:HEAD_DIM] / a_win[HEAD_DIM:HEAD_DIM + 1]

    pipe = _FlashPipe(lambda j: _dot(ks_ref[j], qa_ref[...]), lambda j, p: _dot(vst_ref[j], p),
                      ms_ref, as_ref, s_ref, p_ref, alpha_ref, tile_of=lambda j: visit_ref[j])
    o_sel = pipe.run(jnp.where(causal, _dot(ks_ref[i], qa), NEG), i, n_visit)
    for h in range(NSA_HPG):
        lanes = slice(h * TQ, (h + 1) * TQ)
        row = g * (3 * NSA_HPG) + 3 * h
        g_sel = gate_ref[pl.ds(row + 1, 1), :]
        g_win = gate_ref[pl.ds(row + 2, 1), :]
        o_ref[h] = (accin_ref[h] + g_sel * o_sel[:, lanes] + g_win * o_win[:, lanes]).astype(o_ref.dtype)


def _nsa_selwin(tile_flags, qt, ks_t, vs_t, kw_t, vw_t, sel_t, gates_t, acc_t):
    b, _, _, s = qt.shape
    nt = s // TK
    width = NSA_HPG * TQ
    ksw = ks_t.shape[-1]
    kww = kw_t.shape[-1]
    vspec = pl.BlockSpec((None, None, nt, VAL_ROWS, TK), lambda i, g, j, fl: (i, g, 0, 0, 0))
    qspec = pl.BlockSpec((None, NSA_HPG, HEAD_DIM, TQ), lambda i, g, j, fl: (i, g, 0, j))
    stat = pltpu.VMEM((1, width), F32)
    accs = pltpu.VMEM((VAL_ROWS, width), F32)
    grid_spec = pltpu.PrefetchScalarGridSpec(
        num_scalar_prefetch=1,
        grid=(b, NSA_GROUPS, s // TQ),
        in_specs=[
            qspec,
            pl.BlockSpec((None, None, nt, TK, ksw), lambda i, g, j, fl: (i, g, 0, 0, 0)), vspec,
            pl.BlockSpec((None, None, nt, TK, kww), lambda i, g, j, fl: (i, g, 0, 0, 0)), vspec,
            pl.BlockSpec((None, None, MAX_SEL_BLOCKS, TQ), lambda i, g, j, fl: (i, g, 0, j)),
            pl.BlockSpec((None, LANES, TQ), lambda i, g, j, fl: (i, 0, j)),
            qspec,
        ],
        out_specs=qspec,
        scratch_shapes=[pltpu.VMEM((ksw, width), BF16), stat, accs,
                        pltpu.VMEM((TK, width), F32), pltpu.VMEM((TK, width), BF16), stat,
                        pltpu.SMEM((nt,), I32)],
    )
    return pl.pallas_call(
        _nsa_selwin_kernel,
        grid_spec=grid_spec,
        out_shape=jax.ShapeDtypeStruct((b, NSA_HEADS, HEAD_DIM, s), BF16),
        compiler_params=_params("arbitrary", "arbitrary", "arbitrary"),
        name="nsa_select_window",
    )(tile_flags, qt, ks_t, vs_t, kw_t, vw_t, sel_t, gates_t, acc_t)


def _moba_kernel(q_ref, k_ref, vt_ref, kmean_ref, o_ref, qa_ref, m_ref, acc_ref,
                 s_ref, p_ref, alpha_ref, *, n_top):
    hg = pl.program_id(1)
    i = pl.program_id(2)
    nblk = kmean_ref.shape[1]
    q0 = (i * TQ).astype(F32)
    blk = lax.broadcasted_iota(I32, (nblk, 1), 0)
    for h in range(MOBA_HB):
        head = hg * MOBA_HB + h
        slope = jnp.float32(2.0 ** -MOBA_HEADS)
        for hh in range(MOBA_HEADS - 1):
            slope = jnp.where(head == hh, 2.0 ** -(hh + 1), slope)
        q = q_ref[h]
        gate = jnp.where(blk < i, _dot(kmean_ref[h].astype(BF16), q), NEG)
        bias = jnp.full(gate.shape, -BIG, F32)
        for _ in range(n_top):
            mx = jnp.max(gate, axis=0, keepdims=True)
            first = jnp.min(jnp.where(gate == mx, blk, nblk), axis=0, keepdims=True)
            pick = blk == first
            bias = jnp.where(pick, 0.0, bias)
            gate = jnp.where(pick, -jnp.inf, gate)
        bias = jnp.where(blk < i, bias, 0.0)
        if nblk < MAX_MOBA_BLOCKS:
            bias = jnp.concatenate([bias, jnp.zeros((MAX_MOBA_BLOCKS - nblk, TQ), F32)], axis=0)
        qa_ref[h, 0:HEAD_DIM, :] = q
        qa_ref[h, HEAD_DIM:HEAD_DIM + MAX_MOBA_BLOCKS, :] = _alibi_rows(slope, q0, MAX_MOBA_BLOCKS).astype(BF16)
        qa_ref[h, HEAD_DIM + MAX_MOBA_BLOCKS:, :] = bias.astype(BF16)

    def scores(j):
        return jnp.concatenate([_dot(k_ref[h, j], qa_ref[h]) for h in range(MOBA_HB)], axis=1)

    def values(j, p):
        return jnp.concatenate(
            [_dot(vt_ref[h, j], p[:, h * TQ:(h + 1) * TQ]) for h in range(MOBA_HB)], axis=1)

    pipe = _FlashPipe(scores, values, m_ref, acc_ref, s_ref, p_ref, alpha_ref)
    o = pipe.run(jnp.where(_wide_dist(MOBA_HB * TQ) >= 0, scores(i), NEG), i, i)
    for h in range(MOBA_HB):
        o_ref[h] = o[:, h * TQ:(h + 1) * TQ].astype(o_ref.dtype)


def _moba(qt, k_t, v_t, kmean):
    b, nh, _, s = qt.shape
    nt = s // TK
    n_top = min(MOBA_TOP, nt - 1)
    hb = MOBA_HB
    kw = k_t.shape[-1]
    width = hb * TQ
    return pl.pallas_call(
        functools.partial(_moba_kernel, n_top=n_top),
        grid=(b, nh // hb, s // TQ),
        in_specs=[
            pl.BlockSpec((None, hb, HEAD_DIM, TQ), lambda i, h, j: (i, h, 0, j)),
            pl.BlockSpec((None, hb, nt, TK, kw), lambda i, h, j: (i, h, 0, 0, 0),
                         pipeline_mode=pl.Buffered(1)),
            pl.BlockSpec((None, hb, nt, VAL_ROWS, TK), lambda i, h, j: (i, h, 0, 0, 0),
                         pipeline_mode=pl.Buffered(1)),
            pl.BlockSpec((None, hb, nt, HEAD_DIM), lambda i, h, j: (i, h, 0, 0)),
        ],
        out_specs=pl.BlockSpec((None, hb, HEAD_DIM, TQ), lambda i, h, j: (i, h, 0, j)),
        out_shape=jax.ShapeDtypeStruct((b, nh, HEAD_DIM, s), BF16),
        scratch_shapes=[
            pltpu.VMEM((hb, kw, TQ), BF16), pltpu.VMEM((1, width), F32),
            pltpu.VMEM((VAL_ROWS, width), F32),
            pltpu.VMEM((TK, width), F32), pltpu.VMEM((TK, width), BF16), pltpu.VMEM((1, width), F32),
        ],
        compiler_params=_params("arbitrary", "arbitrary", "arbitrary"),
        name="moba",
    )(qt, k_t, v_t, kmean)


def _outproj_kernel(on_ref, om_ref, w_ref, x_ref, g1_ref, ln_ref, sc_ref, sh_ref,
                    wrh_ref, wrl_ref, rb_ref, x1_ref, h2_ref, lg_ref):
    nh, dh, rows = on_ref.shape
    half = nh * dh

    def rows_major(o_ref):
        return o_ref[...].reshape(half, rows).astype(F32).T.astype(BF16)

    y = _dot(rows_major(on_ref), w_ref[:half, :]) + _dot(rows_major(om_ref), w_ref[half:, :])
    x1 = x_ref[...] + g1_ref[...] * y
    x1_ref[...] = x1
    h2 = _modulated_norm(x1, ln_ref[...], sc_ref[...], sh_ref[...])
    h2_ref[...] = h2
    hh, hl = _hi_lo(h2)
    logits = _dot(hh, wrh_ref[...]) + _dot(hh, wrl_ref[...]) + _dot(hl, wrh_ref[...]) + rb_ref[...]
    lg_ref[...] = logits.T


def _outproj(o_nsa_t, o_moba_t, w_out, x2d, g1, ln_g, sc2p, sh2, wr_hi, wr_lo, rb):
    t, d = x2d.shape
    b, nh, dh, seq = o_nsa_t.shape
    tm = TM_PROJ
    nb = seq // tm
    row = lambda i, j: (i * nb + j, 0)
    bat = lambda i, j: (i, 0, 0)
    const = lambda i, j: (0, 0)
    heads = pl.BlockSpec((None, nh, dh, tm), lambda i, j: (i, 0, 0, j))
    return pl.pallas_call(
        _outproj_kernel,
        grid=(b, nb),
        in_specs=[
            heads, heads,
            pl.BlockSpec(w_out.shape, const),
            pl.BlockSpec((tm, d), row),
            pl.BlockSpec((None, 1, d), bat),
            pl.BlockSpec((1, d), const),
            pl.BlockSpec((None, 1, d), bat), pl.BlockSpec((None, 1, d), bat),
            pl.BlockSpec(wr_hi.shape, const), pl.BlockSpec(wr_lo.shape, const),
            pl.BlockSpec((1, LANES), const),
        ],
        out_specs=[pl.BlockSpec((tm, d), row), pl.BlockSpec((tm, d), row),
                   pl.BlockSpec((LANES, tm), lambda i, j: (0, i * nb + j))],
        out_shape=[jax.ShapeDtypeStruct((t, d), F32), jax.ShapeDtypeStruct((t, d), F32),
                   jax.ShapeDtypeStruct((LANES, t), F32)],
        compiler_params=_params("arbitrary", "arbitrary"),
        name="outproj_router",
    )(o_nsa_t, o_moba_t, w_out, x2d, g1, ln_g, sc2p, sh2, wr_hi, wr_lo, rb)


def _route_kernel(lg_ref, idx_ref, gate_ref, rank_ref, cnt_ref, carry_ref):
    @pl.when(pl.program_id(0) == 0)
    def _():
        carry_ref[...] = jnp.zeros(carry_ref.shape, carry_ref.dtype)

    lg = lg_ref[...]
    ne, tt = lg.shape
    eidx = lax.broadcasted_iota(I32, lg.shape, 0)
    vals, idxs, hots = [], [], []
    for _ in range(TOP_K):
        mx = jnp.max(lg, axis=0, keepdims=True)
        first = jnp.min(jnp.where(lg == mx, eidx, ne), axis=0, keepdims=True)
        hot = eidx == first
        vals.append(mx)
        idxs.append(first)
        hots.append(hot)
        lg = jnp.where(hot, -jnp.inf, lg)
    e = [jnp.exp(v - vals[0]) for v in vals]
    tot = e[0]
    for x in e[1:]:
        tot = tot + x
    cnt = jnp.where(hots[0], 1.0, 0.0)
    for hot in hots[1:]:
        cnt = cnt + jnp.where(hot, 1.0, 0.0)
    cntb = cnt.astype(BF16)
    r = lax.broadcasted_iota(I32, (tt, tt), 0)
    c = lax.broadcasted_iota(I32, (tt, tt), 1)
    before = jnp.where(r < c, 1.0, 0.0).astype(BF16)
    carry = carry_ref[...]
    prefix = _dot(cntb, before) + carry
    ranks = [jnp.sum(jnp.where(hot, prefix, 0.0), axis=0, keepdims=True) for hot in hots]
    carry = carry + _dot(cntb, jnp.ones((tt, tt), BF16))
    carry_ref[...] = carry
    cnt_ref[...] = carry[:, :LANES]
    pad_i = jnp.zeros((SUBLANES - TOP_K, tt), I32)
    pad_f = jnp.zeros((SUBLANES - TOP_K, tt), F32)
    idx_ref[...] = jnp.concatenate(idxs + [pad_i], axis=0)
    gate_ref[...] = jnp.concatenate([x / tot for x in e] + [pad_f], axis=0)
    rank_ref[...] = jnp.concatenate([x.astype(I32) for x in ranks] + [pad_i], axis=0)


def _route(logits_t):
    ne, t = N_EXPERTS, logits_t.shape[1]
    tt = min(TT_ROUTE, t)
    tok = pl.BlockSpec((SUBLANES, tt), lambda i: (0, i))
    return pl.pallas_call(
        _route_kernel,
        grid=(t // tt,),
        in_specs=[pl.BlockSpec((ne, tt), lambda i: (0, i))],
        out_specs=[tok, tok, tok, pl.BlockSpec((ne, LANES), lambda i: (0, 0))],
        out_shape=[jax.ShapeDtypeStruct((SUBLANES, t), I32), jax.ShapeDtypeStruct((SUBLANES, t), F32),
                   jax.ShapeDtypeStruct((SUBLANES, t), I32), jax.ShapeDtypeStruct((ne, LANES), F32)],
        scratch_shapes=[pltpu.VMEM((ne, tt), F32)],
        compiler_params=_params("arbitrary"),
        name="route_topk_rank",
    )(logits_t)


def _dispatch_plan(idx, rank, counts):
    tm = TM_MOE
    t = idx.shape[1]
    nb = (t * TOP_K + N_EXPERTS * tm) // tm
    experts = jnp.arange(N_EXPERTS, dtype=I32)
    counts = counts.astype(I32)
    padded = (counts + tm - 1) // tm * tm
    pend = jnp.sum(jnp.where(experts[None, :] <= experts[:, None], padded[None, :], 0), axis=1)
    pstart = pend - padded
    base = jnp.sum(jnp.where(idx[:, :, None] == experts, pstart, 0), axis=-1)
    dest = (rank + base).T.reshape(t * TOP_K).astype(I32)
    blk_e = jnp.sum(jnp.where(pend[None, :] <= (jnp.arange(nb, dtype=I32) * tm)[:, None], 1, 0), axis=1)
    blk_e = jnp.minimum(blk_e, N_EXPERTS - 1).astype(I32)
    nused = (pend[-1] // tm).astype(I32).reshape(1)
    fill_start = jnp.maximum(pend - tm, 0).astype(I32)
    fill_flag = (padded > counts).astype(I32)
    return dest, blk_e, nused, fill_start, fill_flag


def _dispatch_kernel(fill_start_ref, fill_flag_ref, nused_ref, dest_ref, h_ref, xs_hbm, zbuf, zsem, sem):
    i = pl.program_id(0)
    nrow = dest_ref.shape[1]
    ntok = nrow // TOP_K

    @pl.when(i == 0)
    def _():
        zbuf[...] = jnp.zeros(zbuf.shape, zbuf.dtype)

        def fill(start):
            return pltpu.make_async_copy(zbuf, xs_hbm.at[pl.ds(pl.multiple_of(start, TM_MOE), TM_MOE)], zsem)

        for e in range(N_EXPERTS):
            @pl.when(fill_flag_ref[e] != 0)
            def _():
                fill(fill_start_ref[e]).start()
        for e in range(N_EXPERTS):
            @pl.when(fill_flag_ref[e] != 0)
            def _():
                fill(fill_start_ref[e]).wait()

        nblk = xs_hbm.shape[0] // TM_MOE

        def tail_start(b, c):
            fill(b * TM_MOE).start()
            return c

        def tail_wait(b, c):
            fill(b * TM_MOE).wait()
            return c

        lax.fori_loop(nused_ref[0], nblk, tail_start, 0)
        lax.fori_loop(nused_ref[0], nblk, tail_wait, 0)

    def issue(tl, c):
        src = h_ref.at[pl.ds(tl, 1)]
        for k in range(TOP_K):
            pltpu.make_async_copy(src, xs_hbm.at[pl.ds(dest_ref[0, tl * TOP_K + k], 1)], sem).start(
                priority=k % 2)
        return c

    lax.fori_loop(0, ntok, issue, 0)
    for _ in range(TOP_K):
        pltpu.make_async_copy(h_ref, xs_hbm.at[pl.ds(0, ntok)], sem).wait()


def _dispatch(fill_start, fill_flag, nused, dest, h2):
    t, d = h2.shape
    td = min(TD_DISPATCH, t)
    r = t * TOP_K + N_EXPERTS * TM_MOE
    grid_spec = pltpu.PrefetchScalarGridSpec(
        num_scalar_prefetch=3,
        grid=(t // td,),
        in_specs=[
            pl.BlockSpec((None, 1, td * TOP_K), lambda i, fs, ff, nu: (i, 0, 0), memory_space=pltpu.SMEM),
            pl.BlockSpec((td, d), lambda i, fs, ff, nu: (i, 0)),
        ],
        out_specs=pl.BlockSpec(memory_space=pl.ANY),
        scratch_shapes=[pltpu.VMEM((TM_MOE, d), F32), pltpu.SemaphoreType.DMA(()), pltpu.SemaphoreType.DMA(())],
    )
    return pl.pallas_call(
        _dispatch_kernel,
        grid_spec=grid_spec,
        out_shape=jax.ShapeDtypeStruct((r, d), F32),
        compiler_params=_params("arbitrary"),
        name="moe_dispatch",
    )(fill_start, fill_flag, nused, dest.reshape(t // td, 1, td * TOP_K), h2)


def _moe_ffn_kernel(blk_e_ref, nused_ref, x_ref, w1_ref, b1_ref, w2_ref, b2_ref, y_ref, w1b, w2b):
    i = pl.program_id(0)
    dff = w2_ref.shape[0]

    @pl.when(i < nused_ref[0])
    def _():
        prev = blk_e_ref[jnp.maximum(i - 1, 0)]

        @pl.when((i == 0) | (blk_e_ref[i] != prev))
        def _():
            w1b[...] = w1_ref[...].astype(BF16)
            w2b[...] = w2_ref[...].astype(BF16)

        u = _dot(x_ref[...].astype(BF16), w1b[...]) + b1_ref[...]
        gl = jnp.minimum(u[:, :dff], SWIGLU_LIMIT)
        lin = jnp.clip(u[:, dff:], -SWIGLU_LIMIT, SWIGLU_LIMIT)
        act = (lin + 1.0) * gl * _sigmoid(SWIGLU_ALPHA * gl)
        y_ref[...] = _dot(act.astype(BF16), w2b[...]) + b2_ref[...]

    @pl.when(i >= nused_ref[0])
    def _():
        y_ref[...] = jnp.zeros(y_ref.shape, y_ref.dtype)


def _moe_ffn(blk_e, nused, xs, w1, b1, w2, b2, layer):
    r, d = xs.shape
    depth, ne, _, f2 = w1.shape
    dff = f2 // 2
    tm = TM_MOE
    rows = lambda i, be, nu: (jnp.minimum(i, nu[0] - 1), 0)
    expert = lambda i, be, nu: (layer, be[i], 0, 0)
    grid_spec = pltpu.PrefetchScalarGridSpec(
        num_scalar_prefetch=2,
        grid=(r // tm,),
        in_specs=[
            pl.BlockSpec((tm, d), rows),
            pl.BlockSpec((None, None, d, f2), expert),
            pl.BlockSpec((None, None, 1, f2), expert),
            pl.BlockSpec((None, None, dff, d), expert),
            pl.BlockSpec((None, None, 1, d), expert),
        ],
        out_specs=pl.BlockSpec((tm, d), lambda i, be, nu: (i, 0)),
        scratch_shapes=[pltpu.VMEM((d, f2), BF16), pltpu.VMEM((dff, d), BF16)],
    )
    return pl.pallas_call(
        _moe_ffn_kernel,
        grid_spec=grid_spec,
        out_shape=jax.ShapeDtypeStruct((r, d), F32),
        compiler_params=_params("arbitrary"),
        name="moe_ffn",
    )(blk_e, nused, xs, w1, b1.reshape(depth, ne, 1, f2), w2, b2.reshape(depth, ne, 1, d))


def _moe_combine_kernel(pos_ref, nxt_ref, y_hbm, gate_ref, x1_ref, g2_ref, o_ref, buf, sems):
    i = pl.program_id(0)
    n = pl.num_programs(0)
    tc = x1_ref.shape[0]
    slot = i % 2

    def fetch(p_ref, s):
        def issue(tl, c):
            for k in range(TOP_K):
                pltpu.make_async_copy(y_hbm.at[pl.ds(p_ref[0, tl * TOP_K + k], 1)],
                                      buf.at[s, k, pl.ds(tl, 1)], sems.at[s]).start(priority=k % 2)
            return c
        lax.fori_loop(0, tc, issue, 0, unroll=2)

    @pl.when(i == 0)
    def _():
        fetch(pos_ref, 0)

    for s in range(2):
        @pl.when((i + 1 < n) & (slot != s))
        def _():
            fetch(nxt_ref, s)

    for s in range(2):
        @pl.when(slot == s)
        def _():
            for k in range(TOP_K):
                pltpu.make_async_copy(y_hbm.at[pl.ds(0, tc)], buf.at[s, k], sems.at[s]).wait()
    gate = gate_ref[...]
    moe = gate[:, 0:1] * buf[slot, 0]
    for k in range(1, TOP_K):
        moe = moe + gate[:, k:k + 1] * buf[slot, k]
    o_ref[...] = x1_ref[...] + g2_ref[...] * moe


def _moe_combine(pos, y, gate, x1, g2, seq):
    t, d = x1.shape
    tc = TC_COMB
    nb = seq // tc
    n = t // tc
    pos3 = pos.reshape(n, 1, tc * TOP_K)
    return pl.pallas_call(
        _moe_combine_kernel,
        grid=(n,),
        in_specs=[
            pl.BlockSpec((None, 1, tc * TOP_K), lambda i: (i, 0, 0), memory_space=pltpu.SMEM),
            pl.BlockSpec((None, 1, tc * TOP_K), lambda i: (jnp.minimum(i + 1, n - 1), 0, 0),
                         memory_space=pltpu.SMEM),
            pl.BlockSpec(memory_space=pl.ANY),
            pl.BlockSpec((tc, TOP_K), lambda i: (i, 0)),
            pl.BlockSpec((tc, d), lambda i: (i, 0)),
            pl.BlockSpec((None, 1, d), lambda i: (i // nb, 0, 0)),
        ],
        out_specs=pl.BlockSpec((tc, d), lambda i: (i, 0)),
        out_shape=jax.ShapeDtypeStruct((t, d), F32),
        scratch_shapes=[pltpu.VMEM((2, TOP_K, tc, d), F32), pltpu.SemaphoreType.DMA((2,))],
        compiler_params=_params("arbitrary"),
        name="moe_combine",
    )(pos3, pos3, y, gate, x1, g2)


def _tile_gain(gain, width, scale=1.0):
    return (jnp.tile(gain, width // HEAD_DIM) * scale).reshape(1, width).astype(F32)


def _layer(x2d, mod, b, s, ln1_g, ln2_g, w_in, nsa_q_gain, nsa_k_gain, cmp_pos, cmp_w1, cmp_w2,
           moba_q_gain, moba_k_gain, w_out, router_w, router_b, exp_w1, exp_b1, exp_w2, exp_b2, layer):
    t, d = x2d.shape
    sh1, sc1, g1, sh2, sc2, g2 = [m.reshape(b, 1, d) for m in jnp.split(mod, 6, axis=-1)]
    scale = HEAD_DIM ** -0.5

    (qt, kcv, ks_t, vs_t, kw_t, vw_t, gates_t, mq_t, mk_t, mv_t, kmean) = _inproj(
        x2d, ln1_g.reshape(1, d), 1.0 + sc1, sh1, _pack_w_in(w_in),
        _tile_gain(nsa_q_gain, 512, scale), _tile_gain(nsa_k_gain, 128),
        _tile_gain(moba_q_gain, 512, scale), _tile_gain(moba_k_gain, 512), b, s)

    g = NSA_GROUPS
    xc = kcv.reshape(b, s, 2, g, HEAD_DIM).transpose(0, 2, 3, 1, 4).reshape(
        b, 2, g, s // CMP_STRIDE, CMP_STRIDE * HEAD_DIM)
    cmp = _compress(xc, cmp_pos.reshape(2, 1, CMP_LEN * HEAD_DIM), cmp_w1.astype(BF16),
                    cmp_w2.astype(BF16), nsa_k_gain.reshape(1, HEAD_DIM))
    kc = cmp[:, 0].astype(BF16)
    vct = cmp[:, 1].transpose(0, 1, 3, 2).astype(BF16)
    acc_t, sel_t, tile_hits = _nsa_cmp(qt, kc, vct, gates_t)
    nsel = sel_t.shape[2]
    if nsel < MAX_SEL_BLOCKS:
        sel_t = jnp.concatenate([sel_t, jnp.ones((b, g, MAX_SEL_BLOCKS - nsel, s), F32)], axis=2)
    tile_flags = (tile_hits[..., 0] > 0).astype(I32).reshape(-1)
    o_nsa_t = _nsa_selwin(tile_flags, qt, ks_t, vs_t, kw_t, vw_t, sel_t, gates_t, acc_t)

    km = kmean.reshape(b, s // MOBA_BLOCK, MOBA_HEADS, HEAD_DIM).transpose(0, 2, 1, 3)
    o_moba_t = _moba(mq_t, mk_t, mv_t, km)

    wr = jnp.zeros((d, LANES), F32).at[:, :N_EXPERTS].set(router_w)
    wr_hi = wr.astype(BF16)
    wr_lo = (wr - wr_hi.astype(F32)).astype(BF16)
    rb = jnp.zeros((1, LANES), F32).at[0, :N_EXPERTS].set(router_b)
    x1, h2, logits_t = _outproj(o_nsa_t, o_moba_t, w_out.astype(BF16), x2d, g1, ln2_g.reshape(1, d),
                                1.0 + sc2, sh2, wr_hi, wr_lo, rb)

    idx8, gate8, rank8, counts = _route(logits_t)
    dest, blk_e, nused, fill_start, fill_flag = _dispatch_plan(idx8[:TOP_K], rank8[:TOP_K], counts[:, 0])
    xs = _dispatch(fill_start, fill_flag, nused, dest, h2)
    y = _moe_ffn(blk_e, nused, xs, exp_w1, exp_b1, exp_w2, exp_b2, layer)
    return _moe_combine(dest, y, gate8[:TOP_K].T, x1, g2, s)


def kernel(x, c, ada_w, ada_b, ln1_g, ln2_g, w_in, nsa_q_gain, nsa_k_gain, nsa_cmp_pos, nsa_cmp_w1,
           nsa_cmp_w2, moba_q_gain, moba_k_gain, w_out, router_w, router_b, exp_w1, exp_b1, exp_w2, exp_b2):
    b, s, d = x.shape
    depth = ada_w.shape[0]
    mods = _adaln(c, ada_w, ada_b)
    x2d = x.reshape(b * s, d)
    for l in range(depth):
        x2d = _layer(x2d, mods[l], b, s, ln1_g[l], ln2_g[l], w_in[l], nsa_q_gain[l], nsa_k_gain[l],
                     nsa_cmp_pos[l], nsa_cmp_w1[l], nsa_cmp_w2[l], moba_q_gain[l], moba_k_gain[l],
                     w_out[l], router_w[l], router_b[l], exp_w1, exp_b1, exp_w2, exp_b2, l)
    return x2d.reshape(b, s, d)
```

```python
import functools

import numpy as np
import jax
import jax.numpy as jnp
from jax import lax
from jax.experimental import pallas as pl
from jax.experimental.pallas import tpu as pltpu

F32 = jnp.float32
BF16 = jnp.bfloat16
I32 = jnp.int32

HEAD_DIM = 64
NSA_HEADS = 8
NSA_GROUPS = 2
NSA_HPG = NSA_HEADS // NSA_GROUPS
CMP_LEN = 32
CMP_STRIDE = 16
CMP_HID = 128
SEL_BLOCK = 64
SEL_TOP = 16
WINDOW = 512
MOBA_HEADS = 8
MOBA_BLOCK = 256
MOBA_TOP = 3
N_EXPERTS = 32
TOP_K = 4
SWIGLU_LIMIT = 7.0
SWIGLU_ALPHA = 1.702
EPS = 1e-6
NEG = -1e30
FORCE_SCORE = 1e4

LANES = 128
SUBLANES = 8
TQ = 256
TK = 256
POS_SPLIT = 256
BIG = 1e30
MAX_SEL_BLOCKS = 128
MAX_MOBA_BLOCKS = 32
MOBA_HB = 8
VAL_ROWS = HEAD_DIM + 16
RANK_CHUNK = 32
TM_PROJ = 512
TM_MOE = 512
TC_COMB = 256
TT_ROUTE = 512
TD_DISPATCH = 1024
VMEM_LIMIT = 56 * 1024 * 1024


def _dot(a, b):
    return jnp.dot(a, b, preferred_element_type=F32)


def _hi_lo(x):
    hi = x.astype(BF16)
    lo = (x - hi.astype(F32)).astype(BF16)
    return hi, lo


def _sigmoid(x):
    return 1.0 / (1.0 + jnp.exp(-x))


def _params(*sem):
    return pltpu.CompilerParams(dimension_semantics=sem, vmem_limit_bytes=VMEM_LIMIT)


def _adaln_kernel(c_ref, w_ref, b_ref, o_ref):
    c = c_ref[...]
    cond = c * _sigmoid(c)
    ch, cl = _hi_lo(cond)
    wh, wl = _hi_lo(w_ref[...])
    o_ref[...] = _dot(ch, wh) + _dot(ch, wl) + _dot(cl, wh) + b_ref[...]


def _adaln(c, ada_w, ada_b):
    depth, d, n6 = ada_w.shape
    b = c.shape[0]
    tn = 1024
    c_pad = jnp.zeros((SUBLANES, d), F32).at[:b].set(c)
    out = pl.pallas_call(
        _adaln_kernel,
        grid=(depth, n6 // tn),
        in_specs=[
            pl.BlockSpec((SUBLANES, d), lambda l, j: (0, 0)),
            pl.BlockSpec((None, d, tn), lambda l, j: (l, 0, j)),
            pl.BlockSpec((None, 1, tn), lambda l, j: (l, 0, j)),
        ],
        out_specs=pl.BlockSpec((None, SUBLANES, tn), lambda l, j: (l, 0, j)),
        out_shape=jax.ShapeDtypeStruct((depth, SUBLANES, n6), F32),
        compiler_params=_params("arbitrary", "arbitrary"),
        name="adaln",
    )(c_pad, ada_w, ada_b.reshape(depth, 1, n6))
    return out[:, :b]


def _block_diag_ones():
    r = lax.broadcasted_iota(I32, (LANES, LANES), 0) // HEAD_DIM
    c = lax.broadcasted_iota(I32, (LANES, LANES), 1) // HEAD_DIM
    return jnp.where(r == c, 1.0, 0.0).astype(BF16)


def _head_norm(z, bd, gain):
    cols = []
    for c0 in range(0, z.shape[1], LANES):
        zc = z[:, c0:c0 + LANES]
        hi, lo = _hi_lo(zc * zc)
        ss = _dot(hi, bd) + _dot(lo, bd)
        cols.append(zc * lax.rsqrt(ss * (1.0 / HEAD_DIM) + EPS))
    zn = cols[0] if len(cols) == 1 else jnp.concatenate(cols, axis=1)
    return zn * gain


def _modulated_norm(x, g, sc1p, sh):
    ms = jnp.mean(x * x, axis=-1, keepdims=True)
    y = x * lax.rsqrt(ms + EPS)
    return (y * g) * sc1p + sh


def _heads_transposed(z, nh):
    return z.T.astype(BF16).reshape(nh, HEAD_DIM, z.shape[0])


def _store_value_tiles(o_ref, z, nh):
    zt = z.T.astype(BF16)
    extra = VAL_ROWS - HEAD_DIM
    ones_row = jnp.where(lax.broadcasted_iota(I32, (extra, TK), 0) == 0, 1.0, 0.0).astype(BF16)
    for h in range(nh):
        for t in range(z.shape[0] // TK):
            vt = zt[h * HEAD_DIM:(h + 1) * HEAD_DIM, t * TK:(t + 1) * TK]
            o_ref[h, t] = jnp.concatenate([vt, ones_row], axis=0)


def _store_key_tiles(o_ref, z, extra, nh):
    rows = z.shape[0]
    for h in range(nh):
        ka = jnp.concatenate([z[:, h * HEAD_DIM:(h + 1) * HEAD_DIM], extra], axis=1).astype(BF16)
        o_ref[h] = ka.reshape(rows // TK, TK, ka.shape[1])


def _pos_cols(pos, width):
    col = lax.broadcasted_iota(I32, (pos.shape[0], width), 1)
    hi = lax.shift_right_logical(pos, POS_SPLIT.bit_length() - 1).astype(F32)
    lo = (pos & (POS_SPLIT - 1)).astype(F32)
    return jnp.where(col == 0, hi, jnp.where(col == 1, lo, jnp.where(col == 2, 1.0, 0.0)))


def _onehot_cols(pos, block, width):
    col = lax.broadcasted_iota(I32, (pos.shape[0], width), 1)
    return jnp.where(lax.shift_right_logical(pos, block.bit_length() - 1) == col, 1.0, 0.0)


def _inproj_kernel(x_ref, g_ref, sc_ref, sh_ref, w_ref, gq_ref, gk_ref, gmq_ref, gmk_ref,
                   q_ref, kcv_ref, ks_ref, vs_ref, kw_ref, vw_ref, gate_ref,
                   mq_ref, mk_ref, mv_ref, kmean_ref):
    h = _modulated_norm(x_ref[...], g_ref[...], sc_ref[...], sh_ref[...])
    hb = h.astype(BF16)
    bd = _block_diag_ones()
    rows = hb.shape[0]
    pos = pl.program_id(1) * rows + lax.broadcasted_iota(I32, (rows, 1), 0)

    def proj(a, b):
        return _dot(hb, w_ref[:, a:b])

    q_ref[...] = _heads_transposed(_head_norm(proj(0, 512), bd, gq_ref[...]), NSA_HEADS)
    kv = proj(512, 1280)
    kcv_ref[...] = kv[:, 0:256]
    pos_cols = _pos_cols(pos, HEAD_DIM)
    sel_cols = jnp.concatenate([pos_cols, _onehot_cols(pos, SEL_BLOCK, MAX_SEL_BLOCKS)], axis=1)
    _store_key_tiles(ks_ref, _head_norm(kv[:, 256:384], bd, gk_ref[...]), sel_cols, NSA_GROUPS)
    _store_value_tiles(vs_ref, kv[:, 384:512], NSA_GROUPS)
    _store_key_tiles(kw_ref, _head_norm(kv[:, 512:640], bd, gk_ref[...]), pos_cols, NSA_GROUPS)
    _store_value_tiles(vw_ref, kv[:, 640:768], NSA_GROUPS)
    gate_ref[...] = _sigmoid(proj(1280, 1408)).T
    mq_ref[...] = _heads_transposed(_head_norm(proj(1408, 1920), bd, gmq_ref[...]), MOBA_HEADS)
    mkn = _head_norm(proj(1920, 2432), bd, gmk_ref[...])
    moba_cols = jnp.concatenate([_pos_cols(pos, MAX_MOBA_BLOCKS),
                                 _onehot_cols(pos, MOBA_BLOCK, MAX_MOBA_BLOCKS)], axis=1)
    _store_key_tiles(mk_ref, mkn, moba_cols, MOBA_HEADS)
    _store_value_tiles(mv_ref, proj(2432, 2944), MOBA_HEADS)
    means = [jnp.sum(mkn[r:r + MOBA_BLOCK], axis=0, keepdims=True) * (1.0 / MOBA_BLOCK)
             for r in range(0, rows, MOBA_BLOCK)]
    kmean_ref[...] = jnp.concatenate(means, axis=0)


def _pack_w_in(w_in):
    d = w_in.shape[0]
    wg = jnp.zeros((d, LANES), w_in.dtype).at[:, :3 * NSA_HEADS].set(w_in[:, 1280:1304])
    return jnp.concatenate([w_in[:, :1280], wg, w_in[:, 1304:]], axis=1).astype(BF16)


def _inproj(x2d, ln_g, sc1p, sh, w_pack, gq, gk, gmq, gmk, b, seq):
    t, d = x2d.shape
    tm = TM_PROJ
    nb = seq // tm
    nt = seq // TK
    tpt = tm // TK
    g = NSA_GROUPS
    row = lambda i, j: (i * nb + j, 0)
    bat = lambda i, j: (i, 0, 0)
    const = lambda i, j: (0, 0)
    heads_t = lambda nh: pl.BlockSpec((None, nh, HEAD_DIM, tm), lambda i, j: (i, 0, 0, j))
    key_tiles = lambda nh, w: pl.BlockSpec((None, nh, tpt, TK, w), lambda i, j: (i, 0, j, 0, 0))
    val_tiles = lambda nh: pl.BlockSpec((None, nh, tpt, VAL_ROWS, TK), lambda i, j: (i, 0, j, 0, 0))
    sel_w = 2 * HEAD_DIM + MAX_SEL_BLOCKS
    win_w = 2 * HEAD_DIM
    moba_w = HEAD_DIM + 2 * MAX_MOBA_BLOCKS
    nmean = tm // MOBA_BLOCK
    sds = jax.ShapeDtypeStruct
    out_shape = [
        sds((b, NSA_HEADS, HEAD_DIM, seq), BF16), sds((t, 256), F32),
        sds((b, g, nt, TK, sel_w), BF16), sds((b, g, nt, VAL_ROWS, TK), BF16),
        sds((b, g, nt, TK, win_w), BF16), sds((b, g, nt, VAL_ROWS, TK), BF16),
        sds((b, LANES, seq), F32),
        sds((b, MOBA_HEADS, HEAD_DIM, seq), BF16),
        sds((b, MOBA_HEADS, nt, TK, moba_w), BF16), sds((b, MOBA_HEADS, nt, VAL_ROWS, TK), BF16),
        sds((t // tm, nmean, 512), F32),
    ]
    out_specs = [
        heads_t(NSA_HEADS), pl.BlockSpec((tm, 256), row),
        key_tiles(g, sel_w), val_tiles(g), key_tiles(g, win_w), val_tiles(g),
        pl.BlockSpec((None, LANES, tm), lambda i, j: (i, 0, j)),
        heads_t(MOBA_HEADS), key_tiles(MOBA_HEADS, moba_w), val_tiles(MOBA_HEADS),
        pl.BlockSpec((None, nmean, 512), lambda i, j: (i * nb + j, 0, 0)),
    ]
    return pl.pallas_call(
        _inproj_kernel,
        grid=(b, nb),
        in_specs=[
            pl.BlockSpec((tm, d), row),
            pl.BlockSpec((1, d), const),
            pl.BlockSpec((None, 1, d), bat),
            pl.BlockSpec((None, 1, d), bat),
            pl.BlockSpec(w_pack.shape, const),
            pl.BlockSpec((1, 512), const),
            pl.BlockSpec((1, 128), const),
            pl.BlockSpec((1, 512), const),
            pl.BlockSpec((1, 512), const),
        ],
        out_specs=out_specs,
        out_shape=out_shape,
        compiler_params=_params("arbitrary", "arbitrary"),
        name="inproj",
    )(x2d, ln_g, sc1p, sh, w_pack, gq, gk, gmq, gmk)


def _compress_kernel(x_ref, pos_ref, w1_ref, w2_ref, gk_ref, o_ref):
    kv = pl.program_id(1)
    half = CMP_STRIDE * HEAD_DIM
    x = x_ref[...]
    xl = (x + pos_ref[:, :half]).astype(BF16)
    xh = (x + pos_ref[:, half:]).astype(BF16)
    a = _dot(xl, w1_ref[:half, :])
    b = _dot(xh, w1_ref[half:, :])
    hid = a + pltpu.roll(b, b.shape[0] - 1, 0)
    out = _dot(jax.nn.gelu(hid, approximate=True).astype(BF16), w2_ref[...])
    ms = jnp.mean(out * out, axis=-1, keepdims=True)
    normed = (out * lax.rsqrt(ms + EPS)) * gk_ref[...]
    o_ref[...] = jnp.where(kv == 0, normed, out)


def _compress(xc, pos, w1, w2, gk):
    b, _, g, nr, w = xc.shape
    return pl.pallas_call(
        _compress_kernel,
        grid=(b, 2, g),
        in_specs=[
            pl.BlockSpec((None, None, None, nr, w), lambda i, k, j: (i, k, j, 0, 0)),
            pl.BlockSpec((None, 1, w * 2), lambda i, k, j: (k, 0, 0)),
            pl.BlockSpec((None, w * 2, CMP_HID), lambda i, k, j: (k, 0, 0)),
            pl.BlockSpec((None, CMP_HID, HEAD_DIM), lambda i, k, j: (k, 0, 0)),
            pl.BlockSpec((1, HEAD_DIM), lambda i, k, j: (0, 0)),
        ],
        out_specs=pl.BlockSpec((None, None, None, nr, HEAD_DIM), lambda i, k, j: (i, k, j, 0, 0)),
        out_shape=jax.ShapeDtypeStruct((b, 2, g, nr, HEAD_DIM), F32),
        compiler_params=_params("arbitrary", "arbitrary", "arbitrary"),
        name="nsa_compress",
    )(xc, pos, w1, w2, gk)


def _nsa_slope(g, h):
    return 2.0 ** (-(g * NSA_HPG + h + 1))


def _rank_counts(imp_ref, cnt_ref, last_blk):
    nb = imp_ref.shape[0]
    chunk = min(RANK_CHUNK, nb)
    cnt_ref[...] = jnp.zeros(cnt_ref.shape, cnt_ref.dtype)
    sub = lax.broadcasted_iota(I32, (SUBLANES, imp_ref.shape[1]), 0)
    for c in range(nb // chunk):
        for d in range(nb // chunk):
            @pl.when(max(c, d) * chunk <= last_blk)
            def _():
                rows0 = range(d * chunk, (d + 1) * chunk, SUBLANES)
                pieces = [imp_ref[r:r + SUBLANES] for r in rows0]
                counts = [jnp.zeros(p.shape, I32) for p in pieces]
                for jp in range(c * chunk, (c + 1) * chunk):
                    row = imp_ref[jp:jp + 1]
                    for k, r0 in enumerate(rows0):
                        p = pieces[k]
                        if r0 + SUBLANES - 1 < jp:
                            inc = jnp.where(row > p, 1, 0)
                        elif r0 > jp:
                            inc = jnp.where(row >= p, 1, 0)
                        else:
                            inc = jnp.where(sub > jp - r0, jnp.where(row >= p, 1, 0), jnp.where(row > p, 1, 0))
                        counts[k] = counts[k] + inc
                rows = slice(d * chunk, (d + 1) * chunk)
                cnt_ref[rows] = cnt_ref[rows] + jnp.concatenate(counts, axis=0)


def _nsa_cmp_kernel(q_ref, kc_ref, vct_ref, inct_ref, gate_ref, acc_ref, sel_ref, flag_ref,
                    imp_ref, cnt_ref, *, n_top):
    g = pl.program_id(1)
    q0 = pl.program_id(2) * TQ
    nc = kc_ref.shape[0]
    nsel = inct_ref.shape[0]
    t = q0 + lax.broadcasted_iota(I32, (1, TQ), 1)
    width = NSA_HPG * TQ
    tw = q0 + (lax.broadcasted_iota(I32, (1, width), 1) & (TQ - 1))
    n = lax.broadcasted_iota(I32, (nc, 1), 0)
    center = (CMP_STRIDE * n).astype(F32) + 0.5 * (CMP_LEN - 1)
    dist = tw.astype(F32) - center
    mask = (CMP_STRIDE * n + (CMP_LEN - 1)) <= tw
    slopes = jnp.concatenate(
        [jnp.full((1, TQ), 1.0, F32) * jnp.where(g == 0, _nsa_slope(0, h), _nsa_slope(1, h))
         for h in range(NSA_HPG)], axis=1)
    qw = jnp.concatenate([q_ref[h] for h in range(NSA_HPG)], axis=1)
    s = _dot(kc_ref[...], qw) - slopes * dist
    s = jnp.where(mask, s, NEG)
    m = jnp.max(s, axis=0, keepdims=True)
    e = jnp.exp(s - m)
    l = jnp.sum(e, axis=0, keepdims=True)
    p = jnp.where(mask, e, 0.0) * (1.0 / l)
    o = _dot(vct_ref[...], p.astype(BF16))
    psum = jnp.zeros((nc, TQ), F32)
    for h in range(NSA_HPG):
        lanes = slice(h * TQ, (h + 1) * TQ)
        gate = gate_ref[pl.ds(g * (3 * NSA_HPG) + 3 * h, 1), :]
        acc_ref[h] = gate * o[:, lanes]
        psum = psum + p[:, lanes]
    hi, lo = _hi_lo(psum)
    inct = inct_ref[...]
    imp = _dot(inct, hi) + _dot(inct, lo)
    blk = lax.broadcasted_iota(I32, (nsel, 1), 0)
    cur = t // SEL_BLOCK
    forced = (blk == 0) | (blk == cur) | (blk == cur - 1)
    imp = jnp.where(forced, FORCE_SCORE, imp)
    imp_ref[...] = jnp.where(blk <= cur, imp, NEG)
    _rank_counts(imp_ref, cnt_ref, (q0 + TQ - 1) // SEL_BLOCK)
    chosen = (cnt_ref[...] < n_top) & (blk <= cur)
    sel = jnp.where(chosen, 1.0, 0.0)
    sel_ref[...] = sel
    bpt = TK // SEL_BLOCK
    ntile = nsel // bpt
    trow = lax.broadcasted_iota(I32, (ntile, nsel), 0)
    tcol = lax.broadcasted_iota(I32, (ntile, nsel), 1)
    member = jnp.where(lax.shift_right_logical(tcol, bpt.bit_length() - 1) == trow, 1.0, 0.0).astype(BF16)
    hits = jnp.sum(_dot(member, sel.astype(BF16)), axis=1, keepdims=True)
    flag_ref[...] = jnp.broadcast_to(hits, (ntile, LANES))


def _incidence_t(nc, nsel):
    cmp_start = CMP_STRIDE * np.arange(nc)
    sel_start = SEL_BLOCK * np.arange(nsel)
    inc = ((cmp_start[:, None] <= sel_start[None, :] + SEL_BLOCK - 1)
           & (cmp_start[:, None] + CMP_LEN - 1 >= sel_start[None, :]))
    inc[nc - 1] = False
    return jnp.asarray(inc.T, dtype=BF16)


def _nsa_cmp(qt, kc, vct, gates_t):
    b, _, _, s = qt.shape
    nc = kc.shape[2]
    nsel = s // SEL_BLOCK
    n_top = min(SEL_TOP, nsel)
    ntile = s // TK
    inct = _incidence_t(nc, nsel)
    return pl.pallas_call(
        functools.partial(_nsa_cmp_kernel, n_top=n_top),
        grid=(b, NSA_GROUPS, s // TQ),
        in_specs=[
            pl.BlockSpec((None, NSA_HPG, HEAD_DIM, TQ), lambda i, g, j: (i, g, 0, j)),
            pl.BlockSpec((None, None, nc, HEAD_DIM), lambda i, g, j: (i, g, 0, 0)),
            pl.BlockSpec((None, None, HEAD_DIM, nc), lambda i, g, j: (i, g, 0, 0)),
            pl.BlockSpec((nsel, nc), lambda i, g, j: (0, 0)),
            pl.BlockSpec((None, LANES, TQ), lambda i, g, j: (i, 0, j)),
        ],
        out_specs=[
            pl.BlockSpec((None, NSA_HPG, HEAD_DIM, TQ), lambda i, g, j: (i, g, 0, j)),
            pl.BlockSpec((None, None, nsel, TQ), lambda i, g, j: (i, g, 0, j)),
            pl.BlockSpec((None, None, None, ntile, LANES), lambda i, g, j: (i, g, j, 0, 0)),
        ],
        out_shape=[
            jax.ShapeDtypeStruct((b, NSA_HEADS, HEAD_DIM, s), F32),
            jax.ShapeDtypeStruct((b, NSA_GROUPS, nsel, s), F32),
            jax.ShapeDtypeStruct((b, NSA_GROUPS, s // TQ, ntile, LANES), F32),
        ],
        scratch_shapes=[pltpu.VMEM((nsel, TQ), F32), pltpu.VMEM((nsel, TQ), I32)],
        compiler_params=_params("arbitrary", "arbitrary", "arbitrary"),
        name="nsa_cmp_select",
    )(qt, kc, vct, inct, gates_t)


def _alibi_rows(slope, q0, rows):
    r = lax.broadcasted_iota(I32, (rows, TQ), 0)
    return jnp.where(r == 0, slope * POS_SPLIT, jnp.where(r == 1, slope, jnp.where(r == 2, -slope * q0, 0.0)))


def _wide_dist(width):
    lane = lax.broadcasted_iota(I32, (TK, width), 1) & (TQ - 1)
    return lane - lax.broadcasted_iota(I32, (TK, width), 0)


def _exp_weights(s, m):
    return jnp.exp((s - m).astype(BF16))


class _FlashPipe:
    def __init__(self, scores, values, m_ref, acc_ref, s_ref, p_ref, alpha_ref, tile_of=lambda j: j):
        self.scores, self.values, self.tile_of = scores, values, tile_of
        self.m_ref, self.acc_ref = m_ref, acc_ref
        self.s_ref, self.p_ref, self.alpha_ref = s_ref, p_ref, alpha_ref

    def run(self, first_scores, first_tile, n_past):
        m = jnp.max(first_scores, axis=0, keepdims=True)
        self.m_ref[...] = m
        self.p_ref[...] = _exp_weights(first_scores, m)
        self.alpha_ref[...] = jnp.ones(self.alpha_ref.shape, F32)
        self.acc_ref[...] = jnp.zeros(self.acc_ref.shape, F32)

        @pl.when(n_past > 0)
        def _():
            self.s_ref[...] = self.scores(self.tile_of(0))

        def step(j, carry):
            pending = jnp.where(j == 0, first_tile, self.tile_of(jnp.maximum(j - 1, 0)))
            s = self.s_ref[...]
            self.s_ref[...] = self.scores(self.tile_of(jnp.minimum(j + 1, n_past - 1)))
            pv = self.values(pending, self.p_ref[...])
            self.acc_ref[...] = self.alpha_ref[...] * self.acc_ref[...] + pv
            m_old = self.m_ref[...]
            m_new = jnp.maximum(m_old, jnp.max(s, axis=0, keepdims=True))
            self.m_ref[...] = m_new
            self.alpha_ref[...] = jnp.exp(m_old - m_new)
            self.p_ref[...] = _exp_weights(s, m_new)
            return carry

        lax.fori_loop(0, n_past, step, 0)
        pending = jnp.where(n_past == 0, first_tile, self.tile_of(jnp.maximum(n_past - 1, 0)))
        acc = self.alpha_ref[...] * self.acc_ref[...] + self.values(pending, self.p_ref[...])
        return acc[:HEAD_DIM] / acc[HEAD_DIM:HEAD_DIM + 1]


def _nsa_selwin_kernel(flags_ref, q_ref, ks_ref, vst_ref, kw_ref, vwt_ref, sel_ref, gate_ref, accin_ref,
                       o_ref, qa_ref, ms_ref, as_ref, s_ref, p_ref, alpha_ref, visit_ref):
    g = pl.program_id(1)
    i = pl.program_id(2)
    ntile = ks_ref.shape[0]
    base = ((pl.program_id(0) * NSA_GROUPS + g) * pl.num_programs(2) + i) * ntile

    def note(jt, n):
        visit_ref[n] = jt
        return n + flags_ref[base + jt]

    n_visit = lax.fori_loop(0, i, note, 0)
    q0 = (i * TQ).astype(F32)
    width = NSA_HPG * TQ
    selbias = ((sel_ref[...] - 1.0) * BIG).astype(BF16)
    for h in range(NSA_HPG):
        slope = jnp.where(g == 0, _nsa_slope(0, h), _nsa_slope(1, h))
        lanes = slice(h * TQ, (h + 1) * TQ)
        qa_ref[0:HEAD_DIM, lanes] = q_ref[h]
        qa_ref[HEAD_DIM:2 * HEAD_DIM, lanes] = _alibi_rows(slope, q0, HEAD_DIM).astype(BF16)
        qa_ref[2 * HEAD_DIM:, lanes] = selbias
    qa = qa_ref[...]
    qw = qa_ref[0:2 * HEAD_DIM, :]
    dist = _wide_dist(width)

    causal = dist >= 0
    tiles = [i]
    scores = [jnp.where(causal, _dot(kw_ref[i], qw), NEG)]
    for d in range(1, WINDOW // TK + 1):
        tiles.append(jnp.maximum(i - d, 0))
        limit = jnp.where(i >= d, WINDOW, -(2 ** 30))
        scores.append(jnp.where(dist + d * TK < limit, _dot(kw_ref[tiles[d]], qw), NEG))
    m_win = jnp.max(scores[0], axis=0, keepdims=True)
    for s in scores[1:]:
        m_win = jnp.maximum(m_win, jnp.max(s, axis=0, keepdims=True))
    a_win = jnp.zeros((VAL_ROWS, width), F32)
    for jt, s in zip(tiles, scores):
        a_win = a_win + _dot(vwt_ref[jt], _exp_weights(s, m_win))
    o_win = a_win[:HEAD_DIM] / a_win[HEAD_DIM:HEAD_DIM + 1]

    pipe = _FlashPipe(lambda j: _dot(ks_ref[j], qa_ref[...]), lambda j, p: _dot(vst_ref[j], p),
                      ms_ref, as_ref, s_ref, p_ref, alpha_ref, tile_of=lambda j: visit_ref[j])
    o_sel = pipe.run(jnp.where(causal, _dot(ks_ref[i], qa), NEG), i, n_visit)
    for h in range(NSA_HPG):
        lanes = slice(h * TQ, (h + 1) * TQ)
        row = g * (3 * NSA_HPG) + 3 * h
        g_sel = gate_ref[pl.ds(row + 1, 1), :]
        g_win = gate_ref[pl.ds(row + 2, 1), :]
        o_ref[h] = (accin_ref[h] + g_sel * o_sel[:, lanes] + g_win * o_win[:, lanes]).astype(o_ref.dtype)


def _nsa_selwin(tile_flags, qt, ks_t, vs_t, kw_t, vw_t, sel_t, gates_t, acc_t):
    b, _, _, s = qt.shape
    nt = s // TK
    width = NSA_HPG * TQ
    ksw = ks_t.shape[-1]
    kww = kw_t.shape[-1]
    vspec = pl.BlockSpec((None, None, nt, VAL_ROWS, TK), lambda i, g, j, fl: (i, g, 0, 0, 0))
    qspec = pl.BlockSpec((None, NSA_HPG, HEAD_DIM, TQ), lambda i, g, j, fl: (i, g, 0, j))
    stat = pltpu.VMEM((1, width), F32)
    accs = pltpu.VMEM((VAL_ROWS, width), F32)
    grid_spec = pltpu.PrefetchScalarGridSpec(
        num_scalar_prefetch=1,
        grid=(b, NSA_GROUPS, s // TQ),
        in_specs=[
            qspec,
            pl.BlockSpec((None, None, nt, TK, ksw), lambda i, g, j, fl: (i, g, 0, 0, 0)), vspec,
            pl.BlockSpec((None, None, nt, TK, kww), lambda i, g, j, fl: (i, g, 0, 0, 0)), vspec,
            pl.BlockSpec((None, None, MAX_SEL_BLOCKS, TQ), lambda i, g, j, fl: (i, g, 0, j)),
            pl.BlockSpec((None, LANES, TQ), lambda i, g, j, fl: (i, 0, j)),
            qspec,
        ],
        out_specs=qspec,
        scratch_shapes=[pltpu.VMEM((ksw, width), BF16), stat, accs,
                        pltpu.VMEM((TK, width), F32), pltpu.VMEM((TK, width), BF16), stat,
                        pltpu.SMEM((nt,), I32)],
    )
    return pl.pallas_call(
        _nsa_selwin_kernel,
        grid_spec=grid_spec,
        out_shape=jax.ShapeDtypeStruct((b, NSA_HEADS, HEAD_DIM, s), BF16),
        compiler_params=_params("arbitrary", "arbitrary", "arbitrary"),
        name="nsa_select_window",
    )(tile_flags, qt, ks_t, vs_t, kw_t, vw_t, sel_t, gates_t, acc_t)


def _moba_kernel(q_ref, k_ref, vt_ref, kmean_ref, o_ref, qa_ref, m_ref, acc_ref,
                 s_ref, p_ref, alpha_ref, *, n_top):
    hg = pl.program_id(1)
    i = pl.program_id(2)
    nblk = kmean_ref.shape[1]
    q0 = (i * TQ).astype(F32)
    blk = lax.broadcasted_iota(I32, (nblk, 1), 0)
    for h in range(MOBA_HB):
        head = hg * MOBA_HB + h
        slope = jnp.float32(2.0 ** -MOBA_HEADS)
        for hh in range(MOBA_HEADS - 1):
            slope = jnp.where(head == hh, 2.0 ** -(hh + 1), slope)
        q = q_ref[h]
        gate = jnp.where(blk < i, _dot(kmean_ref[h].astype(BF16), q), NEG)
        bias = jnp.full(gate.shape, -BIG, F32)
        for _ in range(n_top):
            mx = jnp.max(gate, axis=0, keepdims=True)
            first = jnp.min(jnp.where(gate == mx, blk, nblk), axis=0, keepdims=True)
            pick = blk == first
            bias = jnp.where(pick, 0.0, bias)
            gate = jnp.where(pick, -jnp.inf, gate)
        bias = jnp.where(blk < i, bias, 0.0)
        if nblk < MAX_MOBA_BLOCKS:
            bias = jnp.concatenate([bias, jnp.zeros((MAX_MOBA_BLOCKS - nblk, TQ), F32)], axis=0)
        qa_ref[h, 0:HEAD_DIM, :] = q
        qa_ref[h, HEAD_DIM:HEAD_DIM + MAX_MOBA_BLOCKS, :] = _alibi_rows(slope, q0, MAX_MOBA_BLOCKS).astype(BF16)
        qa_ref[h, HEAD_DIM + MAX_MOBA_BLOCKS:, :] = bias.astype(BF16)

    def scores(j):
        return jnp.concatenate([_dot(k_ref[h, j], qa_ref[h]) for h in range(MOBA_HB)], axis=1)

    def values(j, p):
        return jnp.concatenate(
            [_dot(vt_ref[h, j], p[:, h * TQ:(h + 1) * TQ]) for h in range(MOBA_HB)], axis=1)

    pipe = _FlashPipe(scores, values, m_ref, acc_ref, s_ref, p_ref, alpha_ref)
    o = pipe.run(jnp.where(_wide_dist(MOBA_HB * TQ) >= 0, scores(i), NEG), i, i)
    for h in range(MOBA_HB):
        o_ref[h] = o[:, h * TQ:(h + 1) * TQ].astype(o_ref.dtype)


def _moba(qt, k_t, v_t, kmean):
    b, nh, _, s = qt.shape
    nt = s // TK
    n_top = min(MOBA_TOP, nt - 1)
    hb = MOBA_HB
    kw = k_t.shape[-1]
    width = hb * TQ
    return pl.pallas_call(
        functools.partial(_moba_kernel, n_top=n_top),
        grid=(b, nh // hb, s // TQ),
        in_specs=[
            pl.BlockSpec((None, hb, HEAD_DIM, TQ), lambda i, h, j: (i, h, 0, j)),
            pl.BlockSpec((None, hb, nt, TK, kw), lambda i, h, j: (i, h, 0, 0, 0),
                         pipeline_mode=pl.Buffered(1)),
            pl.BlockSpec((None, hb, nt, VAL_ROWS, TK), lambda i, h, j: (i, h, 0, 0, 0),
                         pipeline_mode=pl.Buffered(1)),
            pl.BlockSpec((None, hb, nt, HEAD_DIM), lambda i, h, j: (i, h, 0, 0)),
        ],
        out_specs=pl.BlockSpec((None, hb, HEAD_DIM, TQ), lambda i, h, j: (i, h, 0, j)),
        out_shape=jax.ShapeDtypeStruct((b, nh, HEAD_DIM, s), BF16),
        scratch_shapes=[
            pltpu.VMEM((hb, kw, TQ), BF16), pltpu.VMEM((1, width), F32),
            pltpu.VMEM((VAL_ROWS, width), F32),
            pltpu.VMEM((TK, width), F32), pltpu.VMEM((TK, width), BF16), pltpu.VMEM((1, width), F32),
        ],
        compiler_params=_params("arbitrary", "arbitrary", "arbitrary"),
        name="moba",
    )(qt, k_t, v_t, kmean)


def _outproj_kernel(on_ref, om_ref, w_ref, x_ref, g1_ref, ln_ref, sc_ref, sh_ref,
                    wrh_ref, wrl_ref, rb_ref, x1_ref, h2_ref, lg_ref):
    nh, dh, rows = on_ref.shape
    half = nh * dh

    def rows_major(o_ref):
        return o_ref[...].reshape(half, rows).astype(F32).T.astype(BF16)

    y = _dot(rows_major(on_ref), w_ref[:half, :]) + _dot(rows_major(om_ref), w_ref[half:, :])
    x1 = x_ref[...] + g1_ref[...] * y
    x1_ref[...] = x1
    h2 = _modulated_norm(x1, ln_ref[...], sc_ref[...], sh_ref[...])
    h2_ref[...] = h2
    hh, hl = _hi_lo(h2)
    logits = _dot(hh, wrh_ref[...]) + _dot(hh, wrl_ref[...]) + _dot(hl, wrh_ref[...]) + rb_ref[...]
    lg_ref[...] = logits.T


def _outproj(o_nsa_t, o_moba_t, w_out, x2d, g1, ln_g, sc2p, sh2, wr_hi, wr_lo, rb):
    t, d = x2d.shape
    b, nh, dh, seq = o_nsa_t.shape
    tm = TM_PROJ
    nb = seq // tm
    row = lambda i, j: (i * nb + j, 0)
    bat = lambda i, j: (i, 0, 0)
    const = lambda i, j: (0, 0)
    heads = pl.BlockSpec((None, nh, dh, tm), lambda i, j: (i, 0, 0, j))
    return pl.pallas_call(
        _outproj_kernel,
        grid=(b, nb),
        in_specs=[
            heads, heads,
            pl.BlockSpec(w_out.shape, const),
            pl.BlockSpec((tm, d), row),
            pl.BlockSpec((None, 1, d), bat),
            pl.BlockSpec((1, d), const),
            pl.BlockSpec((None, 1, d), bat), pl.BlockSpec((None, 1, d), bat),
            pl.BlockSpec(wr_hi.shape, const), pl.BlockSpec(wr_lo.shape, const),
            pl.BlockSpec((1, LANES), const),
        ],
        out_specs=[pl.BlockSpec((tm, d), row), pl.BlockSpec((tm, d), row),
                   pl.BlockSpec((LANES, tm), lambda i, j: (0, i * nb + j))],
        out_shape=[jax.ShapeDtypeStruct((t, d), F32), jax.ShapeDtypeStruct((t, d), F32),
                   jax.ShapeDtypeStruct((LANES, t), F32)],
        compiler_params=_params("arbitrary", "arbitrary"),
        name="outproj_router",
    )(o_nsa_t, o_moba_t, w_out, x2d, g1, ln_g, sc2p, sh2, wr_hi, wr_lo, rb)


def _route_kernel(lg_ref, idx_ref, gate_ref, rank_ref, cnt_ref, carry_ref):
    @pl.when(pl.program_id(0) == 0)
    def _():
        carry_ref[...] = jnp.zeros(carry_ref.shape, carry_ref.dtype)

    lg = lg_ref[...]
    ne, tt = lg.shape
    eidx = lax.broadcasted_iota(I32, lg.shape, 0)
    vals, idxs, hots = [], [], []
    for _ in range(TOP_K):
        mx = jnp.max(lg, axis=0, keepdims=True)
        first = jnp.min(jnp.where(lg == mx, eidx, ne), axis=0, keepdims=True)
        hot = eidx == first
        vals.append(mx)
        idxs.append(first)
        hots.append(hot)
        lg = jnp.where(hot, -jnp.inf, lg)
    e = [jnp.exp(v - vals[0]) for v in vals]
    tot = e[0]
    for x in e[1:]:
        tot = tot + x
    cnt = jnp.where(hots[0], 1.0, 0.0)
    for hot in hots[1:]:
        cnt = cnt + jnp.where(hot, 1.0, 0.0)
    cntb = cnt.astype(BF16)
    r = lax.broadcasted_iota(I32, (tt, tt), 0)
    c = lax.broadcasted_iota(I32, (tt, tt), 1)
    before = jnp.where(r < c, 1.0, 0.0).astype(BF16)
    carry = carry_ref[...]
    prefix = _dot(cntb, before) + carry
    ranks = [jnp.sum(jnp.where(hot, prefix, 0.0), axis=0, keepdims=True) for hot in hots]
    carry = carry + _dot(cntb, jnp.ones((tt, tt), BF16))
    carry_ref[...] = carry
    cnt_ref[...] = carry[:, :LANES]
    pad_i = jnp.zeros((SUBLANES - TOP_K, tt), I32)
    pad_f = jnp.zeros((SUBLANES - TOP_K, tt), F32)
    idx_ref[...] = jnp.concatenate(idxs + [pad_i], axis=0)
    gate_ref[...] = jnp.concatenate([x / tot for x in e] + [pad_f], axis=0)
    rank_ref[...] = jnp.concatenate([x.astype(I32) for x in ranks] + [pad_i], axis=0)


def _route(logits_t):
    ne, t = N_EXPERTS, logits_t.shape[1]
    tt = min(TT_ROUTE, t)
    tok = pl.BlockSpec((SUBLANES, tt), lambda i: (0, i))
    return pl.pallas_call(
        _route_kernel,
        grid=(t // tt,),
        in_specs=[pl.BlockSpec((ne, tt), lambda i: (0, i))],
        out_specs=[tok, tok, tok, pl.BlockSpec((ne, LANES), lambda i: (0, 0))],
        out_shape=[jax.ShapeDtypeStruct((SUBLANES, t), I32), jax.ShapeDtypeStruct((SUBLANES, t), F32),
                   jax.ShapeDtypeStruct((SUBLANES, t), I32), jax.ShapeDtypeStruct((ne, LANES), F32)],
        scratch_shapes=[pltpu.VMEM((ne, tt), F32)],
        compiler_params=_params("arbitrary"),
        name="route_topk_rank",
    )(logits_t)


def _dispatch_plan(idx, rank, counts):
    tm = TM_MOE
    t = idx.shape[1]
    nb = (t * TOP_K + N_EXPERTS * tm) // tm
    experts = jnp.arange(N_EXPERTS, dtype=I32)
    counts = counts.astype(I32)
    padded = (counts + tm - 1) // tm * tm
    pend = jnp.sum(jnp.where(experts[None, :] <= experts[:, None], padded[None, :], 0), axis=1)
    pstart = pend - padded
    base = jnp.sum(jnp.where(idx[:, :, None] == experts, pstart, 0), axis=-1)
    dest = (rank + base).T.reshape(t * TOP_K).astype(I32)
    blk_e = jnp.sum(jnp.where(pend[None, :] <= (jnp.arange(nb, dtype=I32) * tm)[:, None], 1, 0), axis=1)
    blk_e = jnp.minimum(blk_e, N_EXPERTS - 1).astype(I32)
    nused = (pend[-1] // tm).astype(I32).reshape(1)
    fill_start = jnp.maximum(pend - tm, 0).astype(I32)
    fill_flag = (padded > counts).astype(I32)
    return dest, blk_e, nused, fill_start, fill_flag


def _dispatch_kernel(fill_start_ref, fill_flag_ref, nused_ref, dest_ref, h_ref, xs_hbm, zbuf, zsem, sem):
    i = pl.program_id(0)
    nrow = dest_ref.shape[1]
    ntok = nrow // TOP_K

    @pl.when(i == 0)
    def _():
        zbuf[...] = jnp.zeros(zbuf.shape, zbuf.dtype)

        def fill(start):
            return pltpu.make_async_copy(zbuf, xs_hbm.at[pl.ds(pl.multiple_of(start, TM_MOE), TM_MOE)], zsem)

        for e in range(N_EXPERTS):
            @pl.when(fill_flag_ref[e] != 0)
            def _():
                fill(fill_start_ref[e]).start()
        for e in range(N_EXPERTS):
            @pl.when(fill_flag_ref[e] != 0)
            def _():
                fill(fill_start_ref[e]).wait()

        nblk = xs_hbm.shape[0] // TM_MOE

        def tail_start(b, c):
            fill(b * TM_MOE).start()
            return c

        def tail_wait(b, c):
            fill(b * TM_MOE).wait()
            return c

        lax.fori_loop(nused_ref[0], nblk, tail_start, 0)
        lax.fori_loop(nused_ref[0], nblk, tail_wait, 0)

    def issue(tl, c):
        src = h_ref.at[pl.ds(tl, 1)]
        for k in range(TOP_K):
            pltpu.make_async_copy(src, xs_hbm.at[pl.ds(dest_ref[0, tl * TOP_K + k], 1)], sem).start(
                priority=k % 2)
        return c

    lax.fori_loop(0, ntok, issue, 0)
    for _ in range(TOP_K):
        pltpu.make_async_copy(h_ref, xs_hbm.at[pl.ds(0, ntok)], sem).wait()


def _dispatch(fill_start, fill_flag, nused, dest, h2):
    t, d = h2.shape
    td = min(TD_DISPATCH, t)
    r = t * TOP_K + N_EXPERTS * TM_MOE
    grid_spec = pltpu.PrefetchScalarGridSpec(
        num_scalar_prefetch=3,
        grid=(t // td,),
        in_specs=[
            pl.BlockSpec((None, 1, td * TOP_K), lambda i, fs, ff, nu: (i, 0, 0), memory_space=pltpu.SMEM),
            pl.BlockSpec((td, d), lambda i, fs, ff, nu: (i, 0)),
        ],
        out_specs=pl.BlockSpec(memory_space=pl.ANY),
        scratch_shapes=[pltpu.VMEM((TM_MOE, d), F32), pltpu.SemaphoreType.DMA(()), pltpu.SemaphoreType.DMA(())],
    )
    return pl.pallas_call(
        _dispatch_kernel,
        grid_spec=grid_spec,
        out_shape=jax.ShapeDtypeStruct((r, d), F32),
        compiler_params=_params("arbitrary"),
        name="moe_dispatch",
    )(fill_start, fill_flag, nused, dest.reshape(t // td, 1, td * TOP_K), h2)


def _moe_ffn_kernel(blk_e_ref, nused_ref, x_ref, w1_ref, b1_ref, w2_ref, b2_ref, y_ref, w1b, w2b):
    i = pl.program_id(0)
    dff = w2_ref.shape[0]

    @pl.when(i < nused_ref[0])
    def _():
        prev = blk_e_ref[jnp.maximum(i - 1, 0)]

        @pl.when((i == 0) | (blk_e_ref[i] != prev))
        def _():
            w1b[...] = w1_ref[...].astype(BF16)
            w2b[...] = w2_ref[...].astype(BF16)

        u = _dot(x_ref[...].astype(BF16), w1b[...]) + b1_ref[...]
        gl = jnp.minimum(u[:, :dff], SWIGLU_LIMIT)
        lin = jnp.clip(u[:, dff:], -SWIGLU_LIMIT, SWIGLU_LIMIT)
        act = (lin + 1.0) * gl * _sigmoid(SWIGLU_ALPHA * gl)
        y_ref[...] = _dot(act.astype(BF16), w2b[...]) + b2_ref[...]

    @pl.when(i >= nused_ref[0])
    def _():
        y_ref[...] = jnp.zeros(y_ref.shape, y_ref.dtype)


def _moe_ffn(blk_e, nused, xs, w1, b1, w2, b2, layer):
    r, d = xs.shape
    depth, ne, _, f2 = w1.shape
    dff = f2 // 2
    tm = TM_MOE
    rows = lambda i, be, nu: (jnp.minimum(i, nu[0] - 1), 0)
    expert = lambda i, be, nu: (layer, be[i], 0, 0)
    grid_spec = pltpu.PrefetchScalarGridSpec(
        num_scalar_prefetch=2,
        grid=(r // tm,),
        in_specs=[
            pl.BlockSpec((tm, d), rows),
            pl.BlockSpec((None, None, d, f2), expert),
            pl.BlockSpec((None, None, 1, f2), expert),
            pl.BlockSpec((None, None, dff, d), expert),
            pl.BlockSpec((None, None, 1, d), expert),
        ],
        out_specs=pl.BlockSpec((tm, d), lambda i, be, nu: (i, 0)),
        scratch_shapes=[pltpu.VMEM((d, f2), BF16), pltpu.VMEM((dff, d), BF16)],
    )
    return pl.pallas_call(
        _moe_ffn_kernel,
        grid_spec=grid_spec,
        out_shape=jax.ShapeDtypeStruct((r, d), F32),
        compiler_params=_params("arbitrary"),
        name="moe_ffn",
    )(blk_e, nused, xs, w1, b1.reshape(depth, ne, 1, f2), w2, b2.reshape(depth, ne, 1, d))


def _moe_combine_kernel(pos_ref, nxt_ref, y_hbm, gate_ref, x1_ref, g2_ref, o_ref, buf, sems):
    i = pl.program_id(0)
    n = pl.num_programs(0)
    tc = x1_ref.shape[0]
    slot = i % 2

    def fetch(p_ref, s):
        def issue(tl, c):
            for k in range(TOP_K):
                pltpu.make_async_copy(y_hbm.at[pl.ds(p_ref[0, tl * TOP_K + k], 1)],
                                      buf.at[s, k, pl.ds(tl, 1)], sems.at[s]).start(priority=k % 2)
            return c
        lax.fori_loop(0, tc, issue, 0, unroll=2)

    @pl.when(i == 0)
    def _():
        fetch(pos_ref, 0)

    for s in range(2):
        @pl.when((i + 1 < n) & (slot != s))
        def _():
            fetch(nxt_ref, s)

    for s in range(2):
        @pl.when(slot == s)
        def _():
            for k in range(TOP_K):
                pltpu.make_async_copy(y_hbm.at[pl.ds(0, tc)], buf.at[s, k], sems.at[s]).wait()
    gate = gate_ref[...]
    moe = gate[:, 0:1] * buf[slot, 0]
    for k in range(1, TOP_K):
        moe = moe + gate[:, k:k + 1] * buf[slot, k]
    o_ref[...] = x1_ref[...] + g2_ref[...] * moe


def _moe_combine(pos, y, gate, x1, g2, seq):
    t, d = x1.shape
    tc = TC_COMB
    nb = seq // tc
    n = t // tc
    pos3 = pos.reshape(n, 1, tc * TOP_K)
    return pl.pallas_call(
        _moe_combine_kernel,
        grid=(n,),
        in_specs=[
            pl.BlockSpec((None, 1, tc * TOP_K), lambda i: (i, 0, 0), memory_space=pltpu.SMEM),
            pl.BlockSpec((None, 1, tc * TOP_K), lambda i: (jnp.minimum(i + 1, n - 1), 0, 0),
                         memory_space=pltpu.SMEM),
            pl.BlockSpec(memory_space=pl.ANY),
            pl.BlockSpec((tc, TOP_K), lambda i: (i, 0)),
            pl.BlockSpec((tc, d), lambda i: (i, 0)),
            pl.BlockSpec((None, 1, d), lambda i: (i // nb, 0, 0)),
        ],
        out_specs=pl.BlockSpec((tc, d), lambda i: (i, 0)),
        out_shape=jax.ShapeDtypeStruct((t, d), F32),
        scratch_shapes=[pltpu.VMEM((2, TOP_K, tc, d), F32), pltpu.SemaphoreType.DMA((2,))],
        compiler_params=_params("arbitrary"),
        name="moe_combine",
    )(pos3, pos3, y, gate, x1, g2)


def _tile_gain(gain, width, scale=1.0):
    return (jnp.tile(gain, width // HEAD_DIM) * scale).reshape(1, width).astype(F32)


def _layer(x2d, mod, b, s, ln1_g, ln2_g, w_in, nsa_q_gain, nsa_k_gain, cmp_pos, cmp_w1, cmp_w2,
           moba_q_gain, moba_k_gain, w_out, router_w, router_b, exp_w1, exp_b1, exp_w2, exp_b2, layer):
    t, d = x2d.shape
    sh1, sc1, g1, sh2, sc2, g2 = [m.reshape(b, 1, d) for m in jnp.split(mod, 6, axis=-1)]
    scale = HEAD_DIM ** -0.5

    (qt, kcv, ks_t, vs_t, kw_t, vw_t, gates_t, mq_t, mk_t, mv_t, kmean) = _inproj(
        x2d, ln1_g.reshape(1, d), 1.0 + sc1, sh1, _pack_w_in(w_in),
        _tile_gain(nsa_q_gain, 512, scale), _tile_gain(nsa_k_gain, 128),
        _tile_gain(moba_q_gain, 512, scale), _tile_gain(moba_k_gain, 512), b, s)

    g = NSA_GROUPS
    xc = kcv.reshape(b, s, 2, g, HEAD_DIM).transpose(0, 2, 3, 1, 4).reshape(
        b, 2, g, s // CMP_STRIDE, CMP_STRIDE * HEAD_DIM)
    cmp = _compress(xc, cmp_pos.reshape(2, 1, CMP_LEN * HEAD_DIM), cmp_w1.astype(BF16),
                    cmp_w2.astype(BF16), nsa_k_gain.reshape(1, HEAD_DIM))
    kc = cmp[:, 0].astype(BF16)
    vct = cmp[:, 1].transpose(0, 1, 3, 2).astype(BF16)
    acc_t, sel_t, tile_hits = _nsa_cmp(qt, kc, vct, gates_t)
    nsel = sel_t.shape[2]
    if nsel < MAX_SEL_BLOCKS:
        sel_t = jnp.concatenate([sel_t, jnp.ones((b, g, MAX_SEL_BLOCKS - nsel, s), F32)], axis=2)
    tile_flags = (tile_hits[..., 0] > 0).astype(I32).reshape(-1)
    o_nsa_t = _nsa_selwin(tile_flags, qt, ks_t, vs_t, kw_t, vw_t, sel_t, gates_t, acc_t)

    km = kmean.reshape(b, s // MOBA_BLOCK, MOBA_HEADS, HEAD_DIM).transpose(0, 2, 1, 3)
    o_moba_t = _moba(mq_t, mk_t, mv_t, km)

    wr = jnp.zeros((d, LANES), F32).at[:, :N_EXPERTS].set(router_w)
    wr_hi = wr.astype(BF16)
    wr_lo = (wr - wr_hi.astype(F32)).astype(BF16)
    rb = jnp.zeros((1, LANES), F32).at[0, :N_EXPERTS].set(router_b)
    x1, h2, logits_t = _outproj(o_nsa_t, o_moba_t, w_out.astype(BF16), x2d, g1, ln2_g.reshape(1, d),
                                1.0 + sc2, sh2, wr_hi, wr_lo, rb)

    idx8, gate8, rank8, counts = _route(logits_t)
    dest, blk_e, nused, fill_start, fill_flag = _dispatch_plan(idx8[:TOP_K], rank8[:TOP_K], counts[:, 0])
    xs = _dispatch(fill_start, fill_flag, nused, dest, h2)
    y = _moe_ffn(blk_e, nused, xs, exp_w1, exp_b1, exp_w2, exp_b2, layer)
    return _moe_combine(dest, y, gate8[:TOP_K].T, x1, g2, s)


def kernel(x, c, ada_w, ada_b, ln1_g, ln2_g, w_in, nsa_q_gain, nsa_k_gain, nsa_cmp_pos, nsa_cmp_w1,
           nsa_cmp_w2, moba_q_gain, moba_k_gain, w_out, router_w, router_b, exp_w1, exp_b1, exp_w2, exp_b2):
    b, s, d = x.shape
    depth = ada_w.shape[0]
    mods = _adaln(c, ada_w, ada_b)
    x2d = x.reshape(b * s, d)
    for l in range(depth):
        x2d = _layer(x2d, mods[l], b, s, ln1_g[l], ln2_g[l], w_in[l], nsa_q_gain[l], nsa_k_gain[l],
                     nsa_cmp_pos[l], nsa_cmp_w1[l], nsa_cmp_w2[l], moba_q_gain[l], moba_k_gain[l],
                     w_out[l], router_w[l], router_b[l], exp_w1, exp_b1, exp_w2, exp_b2, l)
    return x2d.reshape(b, s, d)
```
